```python
import math
import jax, jax.numpy as jnp
from jax import lax
import numpy as np

D_MODEL = 1024
BATCH = 2
SEQ = 8192
DEPTH = 2

D_FF = 2816
LN_EPS = 1e-5
DN_ALPHA = (2 * DEPTH) ** 0.25
DN_BETA = (8 * DEPTH) ** -0.25

GLA_H = 4
GLA_DK = 32
GLA_DV = 64
GLA_GATE_RANK = 16
GLA_GATE_NORM = 16.0
GLA_CHUNK = 64
GLA_QK = GLA_H * GLA_DK
GLA_W = GLA_H * GLA_DV

RW_H = 4
RW_DH = 64
RW_W = RW_H * RW_DH
RW_DECAY_LORA = 64
RW_A_LORA = 64
RW_V_LORA = 32
RW_G_LORA = 128
RW_GN_EPS = 64e-5

MB_H = 8
MB_P = 64
MB_DI = MB_H * MB_P
MB_G = 2
MB_N = 128
MB_CONV = 4
MB_CHUNK = 128
MB_CONV_DIM = MB_DI + 2 * MB_G * MB_N

D_MIX = GLA_W + RW_W + MB_DI

GLA_SPLITS = [GLA_QK, GLA_QK, GLA_W, GLA_GATE_RANK, GLA_W]
RW_SPLITS = [RW_W, RW_W, RW_W, RW_DECAY_LORA, RW_A_LORA, RW_G_LORA]
MB_SPLITS = [MB_DI, MB_CONV_DIM, MB_H]
GLA_COLS = sum(GLA_SPLITS)
RW_COLS = sum(RW_SPLITS)
MB_COLS = sum(MB_SPLITS)
N_IN = GLA_COLS + RW_COLS + MB_COLS

kernel_name = "hybrid_gla_rwkv7_mamba2_macaron_deepnorm"


def _split(t, sizes):
    return jnp.split(t, np.cumsum(sizes)[:-1].tolist(), axis=-1)


def layer_norm(x, g, b):
    xf = x.astype(jnp.float32)
    mu = jnp.mean(xf, axis=-1, keepdims=True)
    var = jnp.mean(jnp.square(xf - mu), axis=-1, keepdims=True)
    return ((xf - mu) * lax.rsqrt(var + LN_EPS) * g + b).astype(x.dtype)


def swiglu(x, w_in, w_down):
    gate, up = jnp.split(x @ w_in, 2, axis=-1)
    return (jax.nn.silu(gate) * up) @ w_down


def token_shift(p, mu):
    prev = jnp.pad(p, ((0, 0), (1, 0), (0, 0)))[:, :-1]
    return p + (prev - p) * mu


def gla_mixer(q, k, v, gate_low, out_gate, gk_up, gk_bias, norm_g):
    f32 = jnp.float32
    bsz, T, _ = q.shape
    C = GLA_CHUNK
    nc = T // C
    gk = jax.nn.log_sigmoid((gate_low @ gk_up + gk_bias).astype(f32)) / GLA_GATE_NORM
    q = q.astype(f32).reshape(bsz, nc, C, GLA_H, GLA_DK) * (GLA_DK ** -0.5)
    k = k.astype(f32).reshape(bsz, nc, C, GLA_H, GLA_DK)
    v = v.astype(f32).reshape(bsz, nc, C, GLA_H, GLA_DV)
    b = jnp.cumsum(gk.reshape(bsz, nc, C, GLA_H, GLA_DK), axis=2)
    b_last = b[:, :, -1:]
    q_e = q * jnp.exp(b)
    k_e = k * jnp.exp(-b)
    k_s = k * jnp.exp(b_last - b)
    causal = jnp.tril(jnp.ones((C, C), dtype=bool))
    att = jnp.where(causal, jnp.einsum('bnihk,bnjhk->bnhij', q_e, k_e), 0.0)
    o = jnp.einsum('bnhij,bnjhv->bnihv', att, v)
    u = jnp.einsum('bnjhk,bnjhv->bnhkv', k_s, v)
    dec = jnp.exp(b_last[:, :, 0])

    def step(S, inp):
        dec_n, u_n = inp
        return dec_n[..., None] * S + u_n, S

    S0 = jnp.zeros((bsz, GLA_H, GLA_DK, GLA_DV), f32)
    _, S_in = lax.scan(step, S0, (jnp.moveaxis(dec, 1, 0), jnp.moveaxis(u, 1, 0)))
    S_in = jnp.moveaxis(S_in, 0, 1)
    o = o + jnp.einsum('bnihk,bnhkv->bnihv', q_e, S_in)
    o = o.reshape(bsz, T, GLA_H, GLA_DV)
    o = o * lax.rsqrt(jnp.mean(jnp.square(o), axis=-1, keepdims=True) + LN_EPS) * norm_g
    o = o * jax.nn.silu(out_gate.astype(f32).reshape(bsz, T, GLA_H, GLA_DV))
    return o.reshape(bsz, T, GLA_W)


def rwkv7_mixer(r, k, v, w_low, a_low, g_low, w0, w_up, a0, a_up, g_up, k_k, k_a, r_k, gn_g, gn_b):
    f32 = jnp.float32
    bsz, T, _ = r.shape
    heads = lambda t: t.astype(f32).reshape(bsz, T, RW_H, RW_DH)
    w = -jax.nn.softplus(-(w0 + jnp.tanh(w_low) @ w_up).astype(f32)) - 0.5
    decay = jnp.exp(-jnp.exp(w))
    a = jax.nn.sigmoid(a0 + a_low @ a_up)
    g = jax.nn.sigmoid(g_low) @ g_up
    kk = heads(k * k_k)
    kk = kk / jnp.maximum(jnp.sqrt(jnp.sum(jnp.square(kk), axis=-1, keepdims=True)), 1e-12)
    k = k * (1 + (a - 1) * k_a)
    rh, kh, vh, ah, wh = heads(r), heads(k), heads(v), heads(a), heads(decay)
    tm = lambda t: jnp.moveaxis(t, 1, 0)
    xs = (tm(rh), tm(wh), tm(kh), tm(vh), tm(-kk), tm(kk * ah))

    def step(S, inp):
        r_t, w_t, k_t, v_t, a_t, b_t = inp
        sa = jnp.einsum('bhij,bhj->bhi', S, a_t)
        S = S * w_t[:, :, None, :] + sa[..., None] * b_t[:, :, None, :] + v_t[..., None] * k_t[:, :, None, :]
        return S, jnp.einsum('bhij,bhj->bhi', S, r_t)

    S0 = jnp.zeros((bsz, RW_H, RW_DH, RW_DH), f32)
    _, y = lax.scan(step, S0, xs)
    y = jnp.moveaxis(y, 0, 1)
    mu = jnp.mean(y, axis=-1, keepdims=True)
    var = jnp.mean(jnp.square(y - mu), axis=-1, keepdims=True)
    y = ((y - mu) * lax.rsqrt(var + RW_GN_EPS)).reshape(bsz, T, RW_W) * gn_g + gn_b
    bonus = jnp.sum(rh * kh * r_k, axis=-1, keepdims=True) * vh
    return (y + bonus.reshape(bsz, T, RW_W)) * g


def mamba2_mixer(z, xbc, dt, conv_w, conv_b, dt_bias, A_log, D, norm_g):
    f32 = jnp.float32
    bsz, T, _ = z.shape
    L = MB_CHUNK
    nc = T // L
    R = MB_H // MB_G
    xpad = jnp.pad(xbc, ((0, 0), (MB_CONV - 1, 0), (0, 0)))
    conv = conv_b + sum(xpad[:, i:i + T] * conv_w[i] for i in range(MB_CONV))
    xbc = jax.nn.silu(conv.astype(f32))
    xs, Bm, Cm = _split(xbc, [MB_DI, MB_G * MB_N, MB_G * MB_N])
    dt = jax.nn.softplus(dt.astype(f32) + dt_bias)
    A = -jnp.exp(A_log.astype(f32))
    xh = xs.reshape(bsz, T, MB_H, MB_P)
    X = (xh * dt[..., None]).reshape(bsz, nc, L, MB_G, R, MB_P)
    dA = jnp.transpose((dt * A).reshape(bsz, nc, L, MB_G, R), (0, 3, 4, 1, 2))
    Bc = Bm.reshape(bsz, nc, L, MB_G, MB_N)
    Cc = Cm.reshape(bsz, nc, L, MB_G, MB_N)
    A_cs = jnp.cumsum(dA, axis=-1)
    causal = jnp.tril(jnp.ones((L, L), dtype=bool))
    Lmat = jnp.exp(jnp.where(causal, A_cs[..., :, None] - A_cs[..., None, :], -jnp.inf))
    cb = jnp.einsum('bclgn,bcsgn->bgcls', Cc, Bc)
    y_diag = jnp.einsum('bgrcls,bcsgrp->bclgrp', cb[:, :, None] * Lmat, X)
    decay_states = jnp.exp(A_cs[..., -1:] - A_cs)
    states = jnp.einsum('bclgn,bgrcl,bclgrp->bcgrpn', Bc, decay_states, X)
    chunk_decay = jnp.exp(A_cs[..., -1])

    def step(S, inp):
        d, s = inp
        return d[..., None, None] * S + s, S

    S0 = jnp.zeros((bsz, MB_G, R, MB_P, MB_N), f32)
    _, S_in = lax.scan(step, S0, (jnp.moveaxis(chunk_decay, -1, 0), jnp.moveaxis(states, 1, 0)))
    S_in = jnp.moveaxis(S_in, 0, 1)
    y_off = jnp.einsum('bclgn,bcgrpn,bgrcl->bclgrp', Cc, S_in, jnp.exp(A_cs))
    y = (y_diag + y_off).reshape(bsz, T, MB_H, MB_P) + D[:, None] * xh
    y = y.reshape(bsz, T, MB_DI) * jax.nn.silu(z.astype(f32))
    yg = y.reshape(bsz, T, MB_G, MB_DI // MB_G)
    yg = yg * lax.rsqrt(jnp.mean(jnp.square(yg), axis=-1, keepdims=True) + LN_EPS)
    return yg.reshape(bsz, T, MB_DI) * norm_g


def setup_inputs(seed: int = 0) -> dict:
    key = jax.random.key(seed)
    ks = iter(jax.random.split(key, 40))
    nrm = lambda shape, s: jax.random.normal(next(ks), shape, jnp.float32) * s
    uni = lambda shape, lo, hi: jax.random.uniform(next(ks), shape, jnp.float32, lo, hi)
    dt0 = jnp.exp(uni((DEPTH, MB_H), math.log(1e-3), math.log(1e-1)))
    return {
        "x": nrm((BATCH, SEQ, D_MODEL), 1.0),
        "ln_g": 1.0 + nrm((DEPTH, 3, D_MODEL), 0.02),
        "ln_b": nrm((DEPTH, 3, D_MODEL), 0.02),
        "ffn_w_in": nrm((DEPTH, 2, D_MODEL, 2 * D_FF), D_MODEL ** -0.5),
        "ffn_w_down": nrm((DEPTH, 2, D_FF, D_MODEL), D_FF ** -0.5 * DN_BETA),
        "w_in": nrm((DEPTH, D_MODEL, N_IN), D_MODEL ** -0.5),
        "w_in_vres": nrm((DEPTH - 1, D_MODEL, RW_V_LORA), D_MODEL ** -0.5),
        "w_out": nrm((DEPTH, D_MIX, D_MODEL), D_MIX ** -0.5 * DN_BETA),
        "gla_gk_up": nrm((DEPTH, GLA_GATE_RANK, GLA_QK), GLA_GATE_RANK ** -0.5),
        "gla_gk_bias": nrm((DEPTH, GLA_QK), 0.1),
        "gla_norm_g": 1.0 + nrm((DEPTH, GLA_DV), 0.02),
        "rw_mu": uni((DEPTH, RW_COLS), 0.0, 1.0),
        "rw_mu_vres": uni((DEPTH - 1, RW_V_LORA), 0.0, 1.0),
        "rw_w0": uni((DEPTH, RW_W), -6.0, -1.0),
        "rw_w_up": nrm((DEPTH, RW_DECAY_LORA, RW_W), 0.5 * RW_DECAY_LORA ** -0.5),
        "rw_a0": nrm((DEPTH, RW_W), 0.1),
        "rw_a_up": nrm((DEPTH, RW_A_LORA, RW_W), RW_A_LORA ** -0.5),
        "rw_g_up": nrm((DEPTH, RW_G_LORA, RW_W), RW_G_LORA ** -0.5),
        "rw_k_k": 0.85 + nrm((DEPTH, RW_W), 0.02),
        "rw_k_a": 1.0 + nrm((DEPTH, RW_W), 0.02),
        "rw_r_k": nrm((DEPTH, RW_H, RW_DH), 0.1),
        "rw_gn_g": 1.0 + nrm((DEPTH, RW_W), 0.02),
        "rw_gn_b": nrm((DEPTH, RW_W), 0.02),
        "rw_v0": nrm((DEPTH - 1, RW_W), 0.1),
        "rw_v_up": nrm((DEPTH - 1, RW_V_LORA, RW_W), RW_V_LORA ** -0.5),
        "mb_conv_w": nrm((DEPTH, MB_CONV, MB_CONV_DIM), 0.5),
        "mb_conv_b": nrm((DEPTH, MB_CONV_DIM), 0.02),
        "mb_dt_bias": dt0 + jnp.log(-jnp.expm1(-dt0)),
        "mb_A_log": jnp.log(uni((DEPTH, MB_H), 1.0, 16.0)),
        "mb_D": 1.0 + nrm((DEPTH, MB_H), 0.02),
        "mb_norm_g": 1.0 + nrm((DEPTH, MB_DI), 0.02),
    }


def reference(x, ln_g, ln_b, ffn_w_in, ffn_w_down, w_in, w_in_vres, w_out,
              gla_gk_up, gla_gk_bias, gla_norm_g,
              rw_mu, rw_mu_vres, rw_w0, rw_w_up, rw_a0, rw_a_up, rw_g_up, rw_k_k, rw_k_a, rw_r_k,
              rw_gn_g, rw_gn_b, rw_v0, rw_v_up,
              mb_conv_w, mb_conv_b, mb_dt_bias, mb_A_log, mb_D, mb_norm_g):
    v_first = None
    for l in range(DEPTH):
        x = layer_norm(DN_ALPHA * x + 0.5 * swiglu(x, ffn_w_in[l, 0], ffn_w_down[l, 0]), ln_g[l, 0], ln_b[l, 0])

        if l == 0:
            w_comb, mu = w_in[l], rw_mu[l]
        else:
            w_comb = jnp.concatenate([w_in[l], w_in_vres[l - 1]], axis=1)
            mu = jnp.concatenate([rw_mu[l], rw_mu_vres[l - 1]])
        proj = x @ w_comb
        gla_p = proj[..., :GLA_COLS]
        mb_p = proj[..., GLA_COLS + RW_COLS:N_IN]
        rw_p = token_shift(jnp.concatenate([proj[..., GLA_COLS:GLA_COLS + RW_COLS], proj[..., N_IN:]], axis=-1), mu)

        g_q, g_k, g_v, g_low, g_gate = _split(gla_p, GLA_SPLITS)
        gla_o = gla_mixer(g_q, g_k, g_v, g_low, g_gate, gla_gk_up[l], gla_gk_bias[l], gla_norm_g[l])

        r_r, r_k, r_v, r_wl, r_al, r_gl = _split(rw_p[..., :RW_COLS], RW_SPLITS)
        if l == 0:
            v_first = r_v
        else:
            r_v = r_v + (v_first - r_v) * jax.nn.sigmoid(rw_v0[l - 1] + rw_p[..., RW_COLS:] @ rw_v_up[l - 1])
        rw_o = rwkv7_mixer(r_r, r_k, r_v, r_wl, r_al, r_gl, rw_w0[l], rw_w_up[l], rw_a0[l], rw_a_up[l],
                           rw_g_up[l], rw_k_k[l], rw_k_a[l], rw_r_k[l], rw_gn_g[l], rw_gn_b[l])

        m_z, m_xbc, m_dt = _split(mb_p, MB_SPLITS)
        mb_o = mamba2_mixer(m_z, m_xbc, m_dt, mb_conv_w[l], mb_conv_b[l], mb_dt_bias[l], mb_A_log[l],
                            mb_D[l], mb_norm_g[l])

        mix = jnp.concatenate([gla_o.astype(x.dtype), rw_o.astype(x.dtype), mb_o.astype(x.dtype)], axis=-1)
        x = layer_norm(DN_ALPHA * x + mix @ w_out[l], ln_g[l, 1], ln_b[l, 1])

        x = layer_norm(DN_ALPHA * x + 0.5 * swiglu(x, ffn_w_in[l, 1], ffn_w_down[l, 1]), ln_g[l, 2], ln_b[l, 2])
    return x
```

```python
import functools

import jax
import jax.numpy as jnp
from jax import lax
from jax.experimental import pallas as pl
from jax.experimental.pallas import tpu as pltpu

F32 = jnp.float32
BF16 = jnp.bfloat16

D_MODEL = 1024
DEPTH = 2
D_FF = 2816
LN_EPS = 1e-5
DN_ALPHA = (2 * DEPTH) ** 0.25

GLA_H, GLA_DK, GLA_DV = 4, 32, 64
GLA_GATE_RANK = 16
GLA_GATE_NORM = 16.0
GLA_QK = GLA_H * GLA_DK
GLA_W = GLA_H * GLA_DV
GLA_COLS = 784
GLA_PAD = 896

RW_H, RW_DH = 4, 64
RW_W = 256
RW_COLS = 1024
RW_V_LORA = 32
RW_GN_EPS = 64e-5
RW_PAD0 = 1152
RW_PAD1 = 1280

MB_H, MB_P, MB_DI, MB_G, MB_N = 8, 64, 512, 2, 128
MB_CONV = 4
MB_CONV_DIM = 1024
MB_COLS = 1544
MB_PAD = 1664
N_IN = GLA_COLS + RW_COLS + MB_COLS

LANE = 128
VMEM_LIMIT = 56 * 1024 * 1024

FFN_TM = 512
FFN_FC = 256
PROJ_TM = 512
GLA_TB, GLA_C = 512, 64
RW_TB, RW_C = 256, 64
MB_TB, MB_L = 256, 128


def _mm(a, b):
    return jnp.dot(a.astype(BF16), b.astype(BF16), preferred_element_type=F32)


def _mm_nt(a, b):
    return lax.dot_general(a.astype(BF16), b.astype(BF16), (((1,), (1,)), ((), ())),
                           preferred_element_type=F32)


def _mm_tn(a, b):
    return lax.dot_general(a.astype(BF16), b.astype(BF16), (((0,), (0,)), ((), ())),
                           preferred_element_type=F32)


def _mmh(a, b):
    return jnp.dot(a, b, preferred_element_type=F32, precision=lax.Precision.HIGHEST)


def _layer_norm(y, g, b):
    mu = jnp.mean(y, axis=-1, keepdims=True)
    d = y - mu
    var = jnp.mean(d * d, axis=-1, keepdims=True)
    return d * lax.rsqrt(var + LN_EPS) * g + b


def _silu(x):
    return x * jax.nn.sigmoid(x)


def _softplus(x):
    return jnp.maximum(x, 0.0) + jnp.log(1.0 + jnp.exp(-jnp.abs(x)))


def _iota(shape, dim):
    return lax.broadcasted_iota(jnp.int32, shape, dim)


def _const_spec(shape):
    n = len(shape)
    return pl.BlockSpec(shape, lambda *_: (0,) * n, pipeline_mode=pl.Buffered(1))


def _ffn_ln_kernel(x_ref, wg_ref, wu_ref, wd_ref, g_ref, b_ref, o_ref, acc_ref, *, n_chunks):
    x = x_ref[...]
    xb = x.astype(BF16)
    acc_ref[...] = jnp.zeros_like(acc_ref)

    def body(c, carry):
        gate = jnp.dot(xb, wg_ref[c], preferred_element_type=F32)
        up = jnp.dot(xb, wu_ref[c], preferred_element_type=F32)
        act = (_silu(gate) * up).astype(BF16)
        acc_ref[...] += jnp.dot(act, wd_ref[c], preferred_element_type=F32)
        return carry

    lax.fori_loop(0, n_chunks, body, 0)
    y = DN_ALPHA * x + 0.5 * acc_ref[...]
    o_ref[...] = _layer_norm(y, g_ref[...], b_ref[...])


def _ffn_ln(x2, w_in, w_down, g, b):
    n = x2.shape[0]
    nc = D_FF // FFN_FC
    wg = w_in[:, :D_FF].reshape(D_MODEL, nc, FFN_FC).transpose(1, 0, 2).astype(BF16)
    wu = w_in[:, D_FF:].reshape(D_MODEL, nc, FFN_FC).transpose(1, 0, 2).astype(BF16)
    wd = w_down.reshape(nc, FFN_FC, D_MODEL).astype(BF16)
    return pl.pallas_call(
        functools.partial(_ffn_ln_kernel, n_chunks=nc),
        out_shape=jax.ShapeDtypeStruct((n, D_MODEL), F32),
        grid=(n // FFN_TM,),
        in_specs=[
            pl.BlockSpec((FFN_TM, D_MODEL), lambda i: (i, 0)),
            _const_spec((nc, D_MODEL, FFN_FC)),
            _const_spec((nc, D_MODEL, FFN_FC)),
            _const_spec((nc, FFN_FC, D_MODEL)),
            _const_spec((1, D_MODEL)),
            _const_spec((1, D_MODEL)),
        ],
        out_specs=pl.BlockSpec((FFN_TM, D_MODEL), lambda i: (i, 0)),
        scratch_shapes=[pltpu.VMEM((FFN_TM, D_MODEL), F32)],
        compiler_params=pltpu.CompilerParams(dimension_semantics=("parallel",),
                                             vmem_limit_bytes=VMEM_LIMIT),
        name="ffn_ln",
    )(x2, wg, wu, wd, g.reshape(1, -1), b.reshape(1, -1))


def _in_proj_kernel(x_ref, wa_ref, wb_ref, wc_ref, oa_ref, ob_ref, oc_ref):
    xb = x_ref[...].astype(BF16)
    oa_ref[...] = jnp.dot(xb, wa_ref[...], preferred_element_type=F32)
    ob_ref[...] = jnp.dot(xb, wb_ref[...], preferred_element_type=F32)
    oc_ref[...] = jnp.dot(xb, wc_ref[...], preferred_element_type=F32)


def _in_proj(x2, wa, wb, wc):
    n = x2.shape[0]
    widths = (wa.shape[1], wb.shape[1], wc.shape[1])
    return pl.pallas_call(
        _in_proj_kernel,
        out_shape=[jax.ShapeDtypeStruct((n, w), F32) for w in widths],
        grid=(n // PROJ_TM,),
        in_specs=[pl.BlockSpec((PROJ_TM, D_MODEL), lambda i: (i, 0))]
        + [_const_spec((D_MODEL, w)) for w in widths],
        out_specs=[pl.BlockSpec((PROJ_TM, w), lambda i: (i, 0)) for w in widths],
        compiler_params=pltpu.CompilerParams(dimension_semantics=("parallel",),
                                             vmem_limit_bytes=VMEM_LIMIT),
        name="in_proj",
    )(x2, wa, wb, wc)


def _out_proj_ln_kernel(x_ref, ma_ref, mb_ref, mc_ref, wa_ref, wb_ref, wc_ref, g_ref, b_ref, o_ref):
    acc = jnp.dot(ma_ref[...].astype(BF16), wa_ref[...], preferred_element_type=F32)
    acc += jnp.dot(mb_ref[...].astype(BF16), wb_ref[...], preferred_element_type=F32)
    acc += jnp.dot(mc_ref[...].astype(BF16), wc_ref[...], preferred_element_type=F32)
    o_ref[...] = _layer_norm(DN_ALPHA * x_ref[...] + acc, g_ref[...], b_ref[...])


def _out_proj_ln(x2, ma, mb, mc, w_out, g, b):
    n = x2.shape[0]
    wo = w_out.astype(BF16)
    wa, wb, wc = wo[:GLA_W], wo[GLA_W:GLA_W + RW_W], wo[GLA_W + RW_W:]
    tm = PROJ_TM
    row = lambda w: pl.BlockSpec((tm, w), lambda i: (i, 0))
    return pl.pallas_call(
        _out_proj_ln_kernel,
        out_shape=jax.ShapeDtypeStruct((n, D_MODEL), F32),
        grid=(n // tm,),
        in_specs=[row(D_MODEL), row(GLA_W), row(RW_W), row(MB_DI),
                  _const_spec((GLA_W, D_MODEL)), _const_spec((RW_W, D_MODEL)),
                  _const_spec((MB_DI, D_MODEL)), _const_spec((1, D_MODEL)), _const_spec((1, D_MODEL))],
        out_specs=row(D_MODEL),
        compiler_params=pltpu.CompilerParams(dimension_semantics=("parallel",),
                                             vmem_limit_bytes=VMEM_LIMIT),
        name="out_proj_ln",
    )(x2, ma, mb, mc, wa, wb, wc, g.reshape(1, -1), b.reshape(1, -1))


def _gla_kernel(p_ref, gkup_ref, gkb_ref, ng_ref, o_ref, st_ref):
    C = GLA_C

    @pl.when(pl.program_id(1) == 0)
    def _():
        st_ref[...] = jnp.zeros_like(st_ref)

    tril = _iota((C, C), 0) >= _iota((C, C), 1)
    tril_f = tril.astype(F32)
    tril4 = (_iota((GLA_H * C, C), 0) % C) >= _iota((GLA_H * C, C), 1)
    qhead = _iota((1, GLA_QK), 1) // GLA_DK
    vhead = _iota((1, GLA_W), 1) // GLA_DV
    smask = (_iota((GLA_W, GLA_QK), 0) // GLA_DV) == (_iota((GLA_W, GLA_QK), 1) // GLA_DK)
    gmat = ((_iota((GLA_W, GLA_W), 0) // GLA_DV) == (_iota((GLA_W, GLA_W), 1) // GLA_DV)).astype(F32) / GLA_DV
    gkup = gkup_ref[...]
    gkb = gkb_ref[...]
    ng = ng_ref[...]

    for c in range(GLA_TB // C):
        rows = pl.ds(c * C, C)
        q = p_ref[0, rows, 0:128]
        k = p_ref[0, rows, 128:256]
        v = p_ref[0, rows, 256:512]
        og = p_ref[0, rows, 512:768]
        gl = p_ref[0, rows, 768:896]
        z = _mm(gl, gkup) + gkb
        gk = -_softplus(-z) / GLA_GATE_NORM
        b = _mmh(tril_f, gk)
        b_last = b[C - 1:C, :]
        q_e = q * (GLA_DK ** -0.5) * jnp.exp(b)
        k_e = k * jnp.exp(-b)
        k_s = k * jnp.exp(b_last - b)
        q_st = jnp.concatenate([jnp.where(qhead == h, q_e, 0.0) for h in range(GLA_H)], axis=0)
        att = jnp.where(tril4, _mm_nt(q_st, k_e), 0.0)
        r = _mm(att, v)
        o = jnp.where(vhead == 0, r[0:C], 0.0)
        for h in range(1, GLA_H):
            o = o + jnp.where(vhead == h, r[h * C:(h + 1) * C], 0.0)
        st = st_ref[...]
        o = o + _mm_nt(q_e, st)
        st_ref[...] = st * jnp.exp(b_last) + jnp.where(smask, _mm_tn(v, k_s), 0.0)
        ms = _mmh(o * o, gmat)
        o = o * lax.rsqrt(ms + LN_EPS) * ng
        o_ref[0, rows, :] = o * _silu(og)


def _gla(p, gk_up, gk_bias, norm_g):
    bsz, T, _ = p.shape
    gkup = jnp.zeros((LANE, GLA_QK), F32).at[:GLA_GATE_RANK].set(gk_up)
    return pl.pallas_call(
        _gla_kernel,
        out_shape=jax.ShapeDtypeStruct((bsz, T, GLA_W), F32),
        grid=(bsz, T // GLA_TB),
        in_specs=[pl.BlockSpec((1, GLA_TB, GLA_PAD), lambda b, t: (b, t, 0)),
                  _const_spec((LANE, GLA_QK)), _const_spec((1, GLA_QK)), _const_spec((1, GLA_W))],
        out_specs=pl.BlockSpec((1, GLA_TB, GLA_W), lambda b, t: (b, t, 0)),
        scratch_shapes=[pltpu.VMEM((GLA_W, GLA_QK), F32)],
        compiler_params=pltpu.CompilerParams(dimension_semantics=("parallel", "arbitrary"),
                                             vmem_limit_bytes=VMEM_LIMIT),
        name="gla_mixer",
    )(p, gkup, gk_bias.reshape(1, -1), jnp.tile(norm_g, GLA_H).reshape(1, -1))


def _rwkv_kernel(*refs, has_vres):
    if has_vres:
        (p_ref, vf_ref, mu_ref, w0_ref, wup_ref, a0_ref, aup_ref, gup_ref, kk_ref, ka_ref, rk_ref,
         gng_ref, gnb_ref, v0_ref, vup_ref, o_ref,
         ht_ref, prev_ref, r_s, k_s, v_s, a_s, b_s, lw_s, y_s) = refs
    else:
        (p_ref, mu_ref, w0_ref, wup_ref, a0_ref, aup_ref, gup_ref, kk_ref, ka_ref, rk_ref,
         gng_ref, gnb_ref, o_ref, vf_out_ref,
         ht_ref, prev_ref, r_s, k_s, v_s, a_s, b_s, lw_s, y_s) = refs
    C, TB, W, H = RW_C, RW_TB, RW_W, RW_H

    @pl.when(pl.program_id(1) == 0)
    def _():
        ht_ref[...] = jnp.zeros_like(ht_ref)
        prev_ref[...] = jnp.zeros_like(prev_ref)

    gmat = ((_iota((W, W), 0) // RW_DH) == (_iota((W, W), 1) // RW_DH)).astype(F32)

    p = p_ref[0]
    shifted = jnp.where(_iota(p.shape, 0) == 0, prev_ref[...], pltpu.roll(p, 1, 0))
    prev_ref[...] = p[TB - 1:TB, :]
    xs = p + (shifted - p) * mu_ref[...]
    r = xs[:, 0:256]
    k = xs[:, 256:512]
    v = xs[:, 512:768]
    w_low = xs[:, 768:896]
    a_low = xs[:, 896:1024]
    g_low = xs[:, 1024:1152]
    if has_vres:
        v_low = xs[:, 1152:1280]
        v = v + (vf_ref[0] - v) * jax.nn.sigmoid(v0_ref[...] + _mm(v_low, vup_ref[...]))
    else:
        vf_out_ref[0] = v
    w = -_softplus(-(w0_ref[...] + _mm(jnp.tanh(w_low), wup_ref[...]))) - 0.5
    lw_s[...] = -jnp.exp(w)
    a = jax.nn.sigmoid(a0_ref[...] + _mm(a_low, aup_ref[...]))
    g = _mm(jax.nn.sigmoid(g_low), gup_ref[...])
    kk = k * kk_ref[...]
    kk = kk / jnp.maximum(jnp.sqrt(_mmh(kk * kk, gmat)), 1e-12)
    k = k * (1.0 + (a - 1.0) * ka_ref[...])
    r_s[...] = r
    k_s[...] = k
    v_s[...] = v
    a_s[...] = -kk
    b_s[...] = kk * a

    HC = H * C
    lane_head = _iota((1, W), 1) // RW_DH
    rb = _iota((HC, HC), 0)
    cb = _iota((HC, HC), 1)
    same = (rb // C) == (cb // C)
    strict = same & ((rb % C) > (cb % C))
    incl = same & ((rb % C) >= (cb % C))
    eye = (rb == cb).astype(F32)
    tril_f = (_iota((C, C), 0) >= _iota((C, C), 1)).astype(F32)

    def stack(x):
        return jnp.concatenate([jnp.where(lane_head == h, x, 0.0) for h in range(H)], axis=0)

    def chunk(c, carry):
        rows = pl.ds(pl.multiple_of(c * C, C), C)
        lw = lw_s[rows, :]
        cs = _mmh(tril_f, lw)
        cs_last = cs[C - 1:C, :]
        e_neg = jnp.exp(-cs)
        e_rem = jnp.exp(cs_last - cs)
        kc = k_s[rows, :]
        bc = b_s[rows, :]
        at = stack(a_s[rows, :] * jnp.exp(cs - lw))
        rt = stack(r_s[rows, :] * jnp.exp(cs))
        kt = stack(kc * e_neg)
        bt = stack(bc * e_neg)
        kh = stack(kc * e_rem)
        bh = stack(bc * e_rem)
        vs = stack(v_s[rows, :])
        a_ab = jnp.where(strict, _mm_nt(at, bt), 0.0)
        a_ak = jnp.where(strict, _mm_nt(at, kt), 0.0)
        a_rk = jnp.where(incl, _mm_nt(rt, kt), 0.0)
        a_rb = jnp.where(incl, _mm_nt(rt, bt), 0.0)
        t = eye + a_ab
        pw = a_ab
        for _ in range(5):
            pw = _mm(pw, pw)
            t = t + _mm(t, pw)
        ht = ht_ref[...]
        u = _mm(t, _mm_nt(at, ht) + _mm(a_ak, vs))
        yb = _mm_nt(rt, ht) + _mm(a_rk, vs) + _mm(a_rb, u)
        y = yb[0:C]
        for h in range(1, H):
            y = y + yb[h * C:(h + 1) * C]
        y_s[rows, :] = y
        ht_ref[...] = ht * jnp.exp(cs_last) + _mm_tn(vs, kh) + _mm_tn(u, bh)
        return carry

    lax.fori_loop(0, TB // C, chunk, 0)

    y = y_s[...]
    gm = gmat / RW_DH
    mu = _mmh(y, gm)
    d = y - mu
    var = _mmh(d * d, gm)
    y = d * lax.rsqrt(var + RW_GN_EPS) * gng_ref[...] + gnb_ref[...]
    r = r_s[...]
    bonus = _mmh(r * k_s[...] * rk_ref[...], gmat) * v_s[...]
    o_ref[0] = (y + bonus) * g


def _rwkv(p, v_first, mu, w0, w_up, a0, a_up, g_up, k_k, k_a, r_k, gn_g, gn_b, v0, v_up):
    bsz, T, width = p.shape
    has_vres = v_first is not None
    row = lambda x: x.reshape(1, -1)
    pad_rows = lambda m: jnp.zeros((LANE, m.shape[1]), F32).at[:m.shape[0]].set(m)
    blk = lambda w: pl.BlockSpec((1, RW_TB, w), lambda b, t: (b, t, 0))
    vec = _const_spec((1, RW_W))
    args = [p]
    specs = [blk(width)]
    if has_vres:
        args.append(v_first)
        specs.append(blk(RW_W))
    args += [row(mu), row(w0), pad_rows(w_up), row(a0), pad_rows(a_up), g_up, row(k_k), row(k_a),
             row(r_k), row(gn_g), row(gn_b)]
    specs += [_const_spec((1, width)), vec, _const_spec((LANE, RW_W)), vec, _const_spec((LANE, RW_W)),
              _const_spec((LANE, RW_W)), vec, vec, vec, vec, vec]
    if has_vres:
        args += [row(v0), pad_rows(v_up)]
        specs += [vec, _const_spec((LANE, RW_W))]
        out_shape = jax.ShapeDtypeStruct((bsz, T, RW_W), F32)
        out_specs = blk(RW_W)
    else:
        out_shape = [jax.ShapeDtypeStruct((bsz, T, RW_W), F32)] * 2
        out_specs = [blk(RW_W)] * 2
    seq = pltpu.VMEM((RW_TB, RW_W), F32)
    return pl.pallas_call(
        functools.partial(_rwkv_kernel, has_vres=has_vres),
        out_shape=out_shape,
        grid=(bsz, T // RW_TB),
        in_specs=specs,
        out_specs=out_specs,
        scratch_shapes=[pltpu.VMEM((RW_W, RW_W), F32), pltpu.VMEM((1, width), F32)] + [seq] * 7,
        compiler_params=pltpu.CompilerParams(dimension_semantics=("parallel", "arbitrary"),
                                             vmem_limit_bytes=VMEM_LIMIT),
        name="rwkv7_mixer",
    )(*args)


def _mamba_kernel(p_ref, cw_ref, cb_ref, dtb_ref, alog_ref, d_ref, ng_ref, o_ref, st_ref, tail_ref):
    L, TB = MB_L, MB_TB

    @pl.when(pl.program_id(1) == 0)
    def _():
        st_ref[...] = jnp.zeros_like(st_ref)
        tail_ref[...] = jnp.zeros_like(tail_ref)

    z = p_ref[0, :, 0:512]
    xbc = p_ref[0, :, 512:1536]
    dt_raw = p_ref[0, :, 1536:1664]

    tail = tail_ref[...]
    row8 = _iota((8, MB_CONV_DIM), 0)
    conv = cb_ref[...] + xbc * cw_ref[MB_CONV - 1:MB_CONV, :]
    for s in range(1, MB_CONV):
        rolled = pltpu.roll(xbc, s, 0)
        head = jnp.where(row8 < s, pltpu.roll(tail, s, 0), rolled[0:8])
        shifted = jnp.concatenate([head, rolled[8:]], axis=0)
        conv = conv + shifted * cw_ref[MB_CONV - 1 - s:MB_CONV - s, :]
    tail_ref[...] = xbc[TB - 8:TB, :]
    xbc = _silu(conv)
    x = xbc[:, 0:512]
    bm = xbc[:, 512:768]
    cm = xbc[:, 768:1024]
    dt = _softplus(dt_raw + dtb_ref[...])
    a_neg = -jnp.exp(alog_ref[...])

    tril = _iota((L, L), 0) >= _iota((L, L), 1)
    tril_f = tril.astype(F32)
    expand = (_iota((LANE, MB_DI), 0) == (_iota((LANE, MB_DI), 1) // MB_P)).astype(F32)
    lane_lo = _iota((1, LANE), 1) < MB_P
    gw = MB_DI // MB_G

    ys = []
    for c in range(TB // L):
        sl = slice(c * L, (c + 1) * L)
        dtc = dt[sl]
        cs = _mmh(tril_f, dtc * a_neg)
        cs_t = cs.T
        cs_x = _mmh(cs, expand)
        dt_x = _mmh(dtc, expand)
        cs_last = cs_x[L - 1:L, :]
        xc = x[sl]
        xdt = xc * dt_x
        xdec = xdt * jnp.exp(cs_last - cs_x)
        e_in = jnp.exp(cs_x)
        st = st_ref[...]
        y_parts = []
        st_parts = []
        for g in range(MB_G):
            bg = bm[sl, g * MB_N:(g + 1) * MB_N]
            cg = cm[sl, g * MB_N:(g + 1) * MB_N]
            cbm = _mm_nt(cg, bg)
            for hp in range(2):
                pair = g * 2 + hp
                xp = xdt[:, pair * LANE:(pair + 1) * LANE]
                yh = []
                for e in range(2):
                    h = pair * 2 + e
                    diff = cs[:, h:h + 1] - cs_t[h:h + 1, :]
                    lmat = jnp.exp(jnp.where(tril, diff, -1e30))
                    yh.append(_mm(cbm * lmat, xp))
                y_parts.append(jnp.where(lane_lo, yh[0], yh[1]))
            st_g = st[:, g * gw:(g + 1) * gw]
            y_parts[-2] = y_parts[-2] + _mm(cg, st_g[:, 0:LANE]) * e_in[:, g * gw:g * gw + LANE]
            y_parts[-1] = y_parts[-1] + _mm(cg, st_g[:, LANE:gw]) * e_in[:, g * gw + LANE:(g + 1) * gw]
            st_parts.append(st_g * jnp.exp(cs_last[:, g * gw:(g + 1) * gw])
                            + _mm_tn(bg, xdec[:, g * gw:(g + 1) * gw]))
        st_ref[...] = jnp.concatenate(st_parts, axis=1)
        ys.append(jnp.concatenate(y_parts, axis=1) + d_ref[...] * xc)
    y = jnp.concatenate(ys, axis=0) * _silu(z)
    outs = []
    for g in range(MB_G):
        yg = y[:, g * gw:(g + 1) * gw]
        outs.append(yg * lax.rsqrt(jnp.mean(yg * yg, axis=-1, keepdims=True) + LN_EPS))
    o_ref[0] = jnp.concatenate(outs, axis=1) * ng_ref[...]


def _mamba(p, conv_w, conv_b, dt_bias, a_log, d, norm_g):
    bsz, T, _ = p.shape
    pad = lambda vec: jnp.zeros((1, LANE), F32).at[0, :MB_H].set(vec)
    return pl.pallas_call(
        _mamba_kernel,
        out_shape=jax.ShapeDtypeStruct((bsz, T, MB_DI), F32),
        grid=(bsz, T // MB_TB),
        in_specs=[pl.BlockSpec((1, MB_TB, MB_PAD), lambda b, t: (b, t, 0)),
                  _const_spec((MB_CONV, MB_CONV_DIM)), _const_spec((1, MB_CONV_DIM)),
                  _const_spec((1, LANE)), _const_spec((1, LANE)), _const_spec((1, MB_DI)),
                  _const_spec((1, MB_DI))],
        out_specs=pl.BlockSpec((1, MB_TB, MB_DI), lambda b, t: (b, t, 0)),
        scratch_shapes=[pltpu.VMEM((MB_N, MB_DI), F32), pltpu.VMEM((8, MB_CONV_DIM), F32)],
        compiler_params=pltpu.CompilerParams(dimension_semantics=("parallel", "arbitrary"),
                                             vmem_limit_bytes=VMEM_LIMIT),
        name="mamba2_mixer",
    )(p, conv_w, conv_b.reshape(1, -1), pad(dt_bias), pad(a_log),
      jnp.repeat(d, MB_P).reshape(1, -1), norm_g.reshape(1, -1))


def _pad_cols(w, width):
    return jnp.pad(w, ((0, 0), (0, width - w.shape[1])))


def _proj_weights(w_in_l, w_vres_l):
    g = w_in_l[:, :GLA_COLS]
    gla = jnp.concatenate([g[:, 0:512], g[:, 528:784], _pad_cols(g[:, 512:528], LANE)], axis=1)
    r = w_in_l[:, GLA_COLS:GLA_COLS + RW_COLS]
    parts = [r[:, 0:768], _pad_cols(r[:, 768:832], LANE), _pad_cols(r[:, 832:896], LANE), r[:, 896:1024]]
    if w_vres_l is not None:
        parts.append(_pad_cols(w_vres_l, LANE))
    rw = jnp.concatenate(parts, axis=1)
    mb = _pad_cols(w_in_l[:, GLA_COLS + RW_COLS:N_IN], MB_PAD)
    return gla.astype(BF16), rw.astype(BF16), mb.astype(BF16)


def _rw_mu(mu_l, mu_vres_l):
    pad = lambda vec: jnp.pad(vec, (0, LANE - vec.shape[0]))
    parts = [mu_l[0:768], pad(mu_l[768:832]), pad(mu_l[832:896]), mu_l[896:1024]]
    if mu_vres_l is not None:
        parts.append(pad(mu_vres_l))
    return jnp.concatenate(parts)


def kernel(x, ln_g, ln_b, ffn_w_in, ffn_w_down, w_in, w_in_vres, w_out, gla_gk_up, gla_gk_bias, gla_norm_g, rw_mu, rw_mu_vres, rw_w0, rw_w_up, rw_a0, rw_a_up, rw_g_up, rw_k_k, rw_k_a, rw_r_k, rw_gn_g, rw_gn_b, rw_v0, rw_v_up, mb_conv_w, mb_conv_b, mb_dt_bias, mb_A_log, mb_D, mb_norm_g):
    bsz, T, _ = x.shape
    n = bsz * T
    x2 = x.reshape(n, D_MODEL)
    v_first = None
    for l in range(DEPTH):
        x2 = _ffn_ln(x2, ffn_w_in[l, 0], ffn_w_down[l, 0], ln_g[l, 0], ln_b[l, 0])
        vres = l > 0
        wa, wb, wc = _proj_weights(w_in[l], w_in_vres[l - 1] if vres else None)
        gla_p, rw_p, mb_p = _in_proj(x2, wa, wb, wc)
        gla_o = _gla(gla_p.reshape(bsz, T, -1), gla_gk_up[l], gla_gk_bias[l], gla_norm_g[l])
        mu = _rw_mu(rw_mu[l], rw_mu_vres[l - 1] if vres else None)
        rw_args = (rw_p.reshape(bsz, T, -1), v_first, mu, rw_w0[l], rw_w_up[l], rw_a0[l], rw_a_up[l],
                   rw_g_up[l], rw_k_k[l], rw_k_a[l], rw_r_k[l].reshape(-1), rw_gn_g[l], rw_gn_b[l])
        if vres:
            rw_o = _rwkv(*rw_args, rw_v0[l - 1], rw_v_up[l - 1])
        else:
            rw_o, v_first = _rwkv(*rw_args, None, None)
        mb_o = _mamba(mb_p.reshape(bsz, T, -1), mb_conv_w[l], mb_conv_b[l], mb_dt_bias[l], mb_A_log[l],
                      mb_D[l], mb_norm_g[l])
        x2 = _out_proj_ln(x2, gla_o.reshape(n, -1), rw_o.reshape(n, -1), mb_o.reshape(n, -1),
                          w_out[l], ln_g[l, 1], ln_b[l, 1])
        x2 = _ffn_ln(x2, ffn_w_in[l, 1], ffn_w_down[l, 1], ln_g[l, 2], ln_b[l, 2])
    return x2.reshape(bsz, T, D_MODEL)
```

```python
import functools

import jax
import jax.numpy as jnp
from jax import lax
from jax.experimental import pallas as pl
from jax.experimental.pallas import tpu as pltpu

F32 = jnp.float32
BF16 = jnp.bfloat16

D_MODEL = 1024
DEPTH = 2
D_FF = 2816
LN_EPS = 1e-5
DN_ALPHA = (2 * DEPTH) ** 0.25

GLA_H, GLA_DK, GLA_DV = 4, 32, 64
GLA_GATE_RANK = 16
GLA_GATE_NORM = 16.0
GLA_QK = GLA_H * GLA_DK
GLA_W = GLA_H * GLA_DV
GLA_COLS = 784
GLA_PAD = 896

RW_H, RW_DH = 4, 64
RW_W = 256
RW_COLS = 1024
RW_V_LORA = 32
RW_GN_EPS = 64e-5
RW_PAD0 = 1152
RW_PAD1 = 1280

MB_H, MB_P, MB_DI, MB_G, MB_N = 8, 64, 512, 2, 128
MB_CONV = 4
MB_CONV_DIM = 1024
MB_COLS = 1544
MB_PAD = 1664
N_IN = GLA_COLS + RW_COLS + MB_COLS

LANE = 128
VMEM_LIMIT = 56 * 1024 * 1024

FFN_TM = 512
FFN_FC = 256
PROJ_TM = 512
GLA_TB, GLA_C = 512, 64
RW_TB, RW_C = 256, 64
MB_TB, MB_L = 256, 128


def _mm(a, b):
    return jnp.dot(a.astype(BF16), b.astype(BF16), preferred_element_type=F32)


def _mm_nt(a, b):
    return lax.dot_general(a.astype(BF16), b.astype(BF16), (((1,), (1,)), ((), ())),
                           preferred_element_type=F32)


def _mm_tn(a, b):
    return lax.dot_general(a.astype(BF16), b.astype(BF16), (((0,), (0,)), ((), ())),
                           preferred_element_type=F32)


def _split(x, parts):
    out = []
    for _ in range(parts - 1):
        piece = x.astype(BF16)
        out.append(piece)
        x = x - piece.astype(F32)
    out.append(x.astype(BF16))
    return out


def _mm_sel_l(x, sel, parts):
    acc = None
    for piece in _split(x, parts):
        term = jnp.dot(piece, sel, preferred_element_type=F32)
        acc = term if acc is None else acc + term
    return acc


def _mm_sel_r(sel, x, parts):
    acc = None
    for piece in _split(x, parts):
        term = jnp.dot(sel, piece, preferred_element_type=F32)
        acc = term if acc is None else acc + term
    return acc


def _layer_norm(y, g, b):
    mu = jnp.mean(y, axis=-1, keepdims=True)
    d = y - mu
    var = jnp.mean(d * d, axis=-1, keepdims=True)
    return d * lax.rsqrt(var + LN_EPS) * g + b


def _silu(x):
    return x * jax.nn.sigmoid(x)


def _softplus(x):
    return jnp.maximum(x, 0.0) + jnp.log(1.0 + jnp.exp(-jnp.abs(x)))


def _iota(shape, dim):
    return lax.broadcasted_iota(jnp.int32, shape, dim)


def _const_spec(shape):
    n = len(shape)
    return pl.BlockSpec(shape, lambda *_: (0,) * n, pipeline_mode=pl.Buffered(1))


def _ffn_ln_kernel(x_ref, wg_ref, wu_ref, wd_ref, g_ref, b_ref, o_ref, acc_ref, *, n_chunks):
    x = x_ref[...]
    xb = x.astype(BF16)
    acc_ref[...] = jnp.zeros_like(acc_ref)

    def body(c, carry):
        gate = jnp.dot(xb, wg_ref[c], preferred_element_type=F32)
        up = jnp.dot(xb, wu_ref[c], preferred_element_type=F32)
        act = (_silu(gate) * up).astype(BF16)
        acc_ref[...] += jnp.dot(act, wd_ref[c], preferred_element_type=F32)
        return carry

    lax.fori_loop(0, n_chunks, body, 0)
    y = DN_ALPHA * x + 0.5 * acc_ref[...]
    o_ref[...] = _layer_norm(y, g_ref[...], b_ref[...])


def _ffn_ln(x2, w_in, w_down, g, b):
    n = x2.shape[0]
    nc = D_FF // FFN_FC
    wg = w_in[:, :D_FF].reshape(D_MODEL, nc, FFN_FC).transpose(1, 0, 2).astype(BF16)
    wu = w_in[:, D_FF:].reshape(D_MODEL, nc, FFN_FC).transpose(1, 0, 2).astype(BF16)
    wd = w_down.reshape(nc, FFN_FC, D_MODEL).astype(BF16)
    return pl.pallas_call(
        functools.partial(_ffn_ln_kernel, n_chunks=nc),
        out_shape=jax.ShapeDtypeStruct((n, D_MODEL), F32),
        grid=(n // FFN_TM,),
        in_specs=[
            pl.BlockSpec((FFN_TM, D_MODEL), lambda i: (i, 0)),
            _const_spec((nc, D_MODEL, FFN_FC)),
            _const_spec((nc, D_MODEL, FFN_FC)),
            _const_spec((nc, FFN_FC, D_MODEL)),
            _const_spec((1, D_MODEL)),
            _const_spec((1, D_MODEL)),
        ],
        out_specs=pl.BlockSpec((FFN_TM, D_MODEL), lambda i: (i, 0)),
        scratch_shapes=[pltpu.VMEM((FFN_TM, D_MODEL), F32)],
        compiler_params=pltpu.CompilerParams(dimension_semantics=("parallel",),
                                             vmem_limit_bytes=VMEM_LIMIT),
        name="ffn_ln",
    )(x2, wg, wu, wd, g.reshape(1, -1), b.reshape(1, -1))


def _in_proj_kernel(x_ref, wa_ref, wb_ref, wc_ref, oa_ref, ob_ref, oc_ref):
    xb = x_ref[...].astype(BF16)
    oa_ref[...] = jnp.dot(xb, wa_ref[...], preferred_element_type=F32)
    ob_ref[...] = jnp.dot(xb, wb_ref[...], preferred_element_type=F32)
    oc_ref[...] = jnp.dot(xb, wc_ref[...], preferred_element_type=F32)


def _in_proj(x2, wa, wb, wc):
    n = x2.shape[0]
    widths = (wa.shape[1], wb.shape[1], wc.shape[1])
    return pl.pallas_call(
        _in_proj_kernel,
        out_shape=[jax.ShapeDtypeStruct((n, w), F32) for w in widths],
        grid=(n // PROJ_TM,),
        in_specs=[pl.BlockSpec((PROJ_TM, D_MODEL), lambda i: (i, 0))]
        + [_const_spec((D_MODEL, w)) for w in widths],
        out_specs=[pl.BlockSpec((PROJ_TM, w), lambda i: (i, 0)) for w in widths],
        compiler_params=pltpu.CompilerParams(dimension_semantics=("parallel",),
                                             vmem_limit_bytes=VMEM_LIMIT),
        name="in_proj",
    )(x2, wa, wb, wc)


def _out_proj_ln_kernel(x_ref, ma_ref, mb_ref, mc_ref, wa_ref, wb_ref, wc_ref, g_ref, b_ref, o_ref):
    acc = jnp.dot(ma_ref[...].astype(BF16), wa_ref[...], preferred_element_type=F32)
    acc += jnp.dot(mb_ref[...].astype(BF16), wb_ref[...], preferred_element_type=F32)
    acc += jnp.dot(mc_ref[...].astype(BF16), wc_ref[...], preferred_element_type=F32)
    o_ref[...] = _layer_norm(DN_ALPHA * x_ref[...] + acc, g_ref[...], b_ref[...])


def _out_proj_ln(x2, ma, mb, mc, w_out, g, b):
    n = x2.shape[0]
    wo = w_out.astype(BF16)
    wa, wb, wc = wo[:GLA_W], wo[GLA_W:GLA_W + RW_W], wo[GLA_W + RW_W:]
    tm = PROJ_TM
    row = lambda w: pl.BlockSpec((tm, w), lambda i: (i, 0))
    return pl.pallas_call(
        _out_proj_ln_kernel,
        out_shape=jax.ShapeDtypeStruct((n, D_MODEL), F32),
        grid=(n // tm,),
        in_specs=[row(D_MODEL), row(GLA_W), row(RW_W), row(MB_DI),
                  _const_spec((GLA_W, D_MODEL)), _const_spec((RW_W, D_MODEL)),
                  _const_spec((MB_DI, D_MODEL)), _const_spec((1, D_MODEL)), _const_spec((1, D_MODEL))],
        out_specs=row(D_MODEL),
        compiler_params=pltpu.CompilerParams(dimension_semantics=("parallel",),
                                             vmem_limit_bytes=VMEM_LIMIT),
        name="out_proj_ln",
    )(x2, ma, mb, mc, wa, wb, wc, g.reshape(1, -1), b.reshape(1, -1))


def _gla_kernel(p_ref, gkup_ref, gkb_ref, ng_ref, o_ref, st_ref):
    C = GLA_C

    @pl.when(pl.program_id(1) == 0)
    def _():
        st_ref[...] = jnp.zeros_like(st_ref)

    tril = _iota((C, C), 0) >= _iota((C, C), 1)
    tril_b = tril.astype(BF16)
    tril4 = (_iota((GLA_H * C, C), 0) % C) >= _iota((GLA_H * C, C), 1)
    qhead = _iota((1, GLA_QK), 1) // GLA_DK
    vhead = _iota((1, GLA_W), 1) // GLA_DV
    smask = (_iota((GLA_W, GLA_QK), 0) // GLA_DV) == (_iota((GLA_W, GLA_QK), 1) // GLA_DK)
    gmat = ((_iota((GLA_W, GLA_W), 0) // GLA_DV) == (_iota((GLA_W, GLA_W), 1) // GLA_DV)).astype(BF16) / GLA_DV
    gkup = gkup_ref[...]
    gkb = gkb_ref[...]
    ng = ng_ref[...]

    for c in range(GLA_TB // C):
        rows = pl.ds(c * C, C)
        q = p_ref[0, rows, 0:128]
        k = p_ref[0, rows, 128:256]
        v = p_ref[0, rows, 256:512]
        og = p_ref[0, rows, 512:768]
        gl = p_ref[0, rows, 768:896]
        z = _mm(gl, gkup) + gkb
        gk = -_softplus(-z) / GLA_GATE_NORM
        b = _mm_sel_r(tril_b, gk, 3)
        b_last = b[C - 1:C, :]
        q_e = q * (GLA_DK ** -0.5) * jnp.exp(b)
        k_e = k * jnp.exp(-b)
        k_s = k * jnp.exp(b_last - b)
        q_st = jnp.concatenate([jnp.where(qhead == h, q_e, 0.0) for h in range(GLA_H)], axis=0)
        att = jnp.where(tril4, _mm_nt(q_st, k_e), 0.0)
        r = _mm(att, v)
        o = jnp.where(vhead == 0, r[0:C], 0.0)
        for h in range(1, GLA_H):
            o = o + jnp.where(vhead == h, r[h * C:(h + 1) * C], 0.0)
        st = st_ref[...]
        o = o + _mm_nt(q_e, st)
        st_ref[...] = st * jnp.exp(b_last) + jnp.where(smask, _mm_tn(v, k_s), 0.0)
        ms = _mm_sel_l(o * o, gmat, 2)
        o = o * lax.rsqrt(ms + LN_EPS) * ng
        o_ref[0, rows, :] = o * _silu(og)


def _gla(p, gk_up, gk_bias, norm_g):
    bsz, T, _ = p.shape
    gkup = jnp.zeros((LANE, GLA_QK), F32).at[:GLA_GATE_RANK].set(gk_up)
    return pl.pallas_call(
        _gla_kernel,
        out_shape=jax.ShapeDtypeStruct((bsz, T, GLA_W), F32),
        grid=(bsz, T // GLA_TB),
        in_specs=[pl.BlockSpec((1, GLA_TB, GLA_PAD), lambda b, t: (b, t, 0)),
                  _const_spec((LANE, GLA_QK)), _const_spec((1, GLA_QK)), _const_spec((1, GLA_W))],
        out_specs=pl.BlockSpec((1, GLA_TB, GLA_W), lambda b, t: (b, t, 0)),
        scratch_shapes=[pltpu.VMEM((GLA_W, GLA_QK), F32)],
        compiler_params=pltpu.CompilerParams(dimension_semantics=("parallel", "arbitrary"),
                                             vmem_limit_bytes=VMEM_LIMIT),
        name="gla_mixer",
    )(p, gkup, gk_bias.reshape(1, -1), jnp.tile(norm_g, GLA_H).reshape(1, -1))


def _rwkv_kernel(*refs, has_vres, bsz):
    if has_vres:
        (p_ref, vf_ref, mu_ref, w0_ref, wup_ref, a0_ref, aup_ref, gup_ref, kk_ref, ka_ref, rk_ref,
         gng_ref, gnb_ref, v0_ref, vup_ref, o_ref,
         ht_ref, prev_ref, r_s, k_s, v_s, a_s, b_s, lw_s, cs_s, g_s, y_s) = refs
    else:
        (p_ref, mu_ref, w0_ref, wup_ref, a0_ref, aup_ref, gup_ref, kk_ref, ka_ref, rk_ref,
         gng_ref, gnb_ref, o_ref, vf_out_ref,
         ht_ref, prev_ref, r_s, k_s, v_s, a_s, b_s, lw_s, cs_s, g_s, y_s) = refs
    C, TB, W, H = RW_C, RW_TB, RW_W, RW_H

    @pl.when(pl.program_id(0) == 0)
    def _():
        ht_ref[...] = jnp.zeros_like(ht_ref)
        prev_ref[...] = jnp.zeros_like(prev_ref)

    head_eq = (_iota((W, W), 0) // RW_DH) == (_iota((W, W), 1) // RW_DH)
    gsum = head_eq.astype(BF16)
    gmean = (head_eq.astype(F32) / RW_DH).astype(BF16)
    tril_blk = (((_iota((TB, TB), 0) // C) == (_iota((TB, TB), 1) // C))
                & (_iota((TB, TB), 0) >= _iota((TB, TB), 1))).astype(BF16)

    for bi in range(bsz):
        p = p_ref[bi]
        shifted = jnp.where(_iota(p.shape, 0) == 0, prev_ref[bi], pltpu.roll(p, 1, 0))
        prev_ref[bi] = p[TB - 1:TB, :]
        xs = p + (shifted - p) * mu_ref[...]
        r = xs[:, 0:256]
        k = xs[:, 256:512]
        v = xs[:, 512:768]
        w_low = xs[:, 768:896]
        a_low = xs[:, 896:1024]
        g_low = xs[:, 1024:1152]
        if has_vres:
            v_low = xs[:, 1152:1280]
            v = v + (vf_ref[bi] - v) * jax.nn.sigmoid(v0_ref[...] + _mm(v_low, vup_ref[...]))
        else:
            vf_out_ref[bi] = v
        w = -_softplus(-(w0_ref[...] + _mm(jnp.tanh(w_low), wup_ref[...]))) - 0.5
        lw = -jnp.exp(w)
        lw_s[bi] = lw
        cs_s[bi] = _mm_sel_r(tril_blk, lw, 3)
        a = jax.nn.sigmoid(a0_ref[...] + _mm(a_low, aup_ref[...]))
        g_s[bi] = _mm(jax.nn.sigmoid(g_low), gup_ref[...])
        kk = k * kk_ref[...]
        kk = kk / jnp.maximum(jnp.sqrt(_mm_sel_l(kk * kk, gsum, 2)), 1e-12)
        r_s[bi] = r
        k_s[bi] = k * (1.0 + (a - 1.0) * ka_ref[...])
        v_s[bi] = v
        a_s[bi] = -kk
        b_s[bi] = kk * a

    ri = _iota((C, W), 0)
    li = _iota((C, W), 1) % C
    strict = ri > li
    incl = ri >= li
    eye = (ri == li).astype(F32)
    blk_eq = (_iota((H * C, W), 0) // C) == (_iota((H * C, W), 1) // RW_DH)

    def bdiag(m):
        return jnp.where(blk_eq, jnp.concatenate([m] * H, axis=0), 0.0).astype(BF16)

    chains = [(bi, c) for c in range(TB // C) for bi in range(bsz)]
    ew = lambda f: [f(i) for i in range(len(chains))]
    sl = [(bi, slice(c * C, (c + 1) * C)) for bi, c in chains]
    cs = ew(lambda i: cs_s[sl[i]])
    kc = ew(lambda i: k_s[sl[i]])
    bc = ew(lambda i: b_s[sl[i]])
    vc = ew(lambda i: v_s[sl[i]])
    e_neg = ew(lambda i: jnp.exp(-cs[i]))
    at = ew(lambda i: a_s[sl[i]] * jnp.exp(cs[i] - lw_s[sl[i]]))
    rt = ew(lambda i: (r_s[sl[i]] * jnp.exp(cs[i])).astype(BF16))
    ar = ew(lambda i: jnp.concatenate([at[i].astype(BF16), rt[i]], axis=0))
    pk = ew(lambda i: _mm_nt(ar[i], bdiag(kc[i] * e_neg[i])))
    pb = ew(lambda i: _mm_nt(ar[i], bdiag(bc[i] * e_neg[i])))
    a_ak = ew(lambda i: jnp.where(strict, pk[i][0:C], 0.0))
    a_rk = ew(lambda i: jnp.where(incl, pk[i][C:2 * C], 0.0))
    a_ab = ew(lambda i: jnp.where(strict, pb[i][0:C], 0.0))
    a_rb = ew(lambda i: jnp.where(incl, pb[i][C:2 * C], 0.0).astype(BF16))
    t = ew(lambda i: eye + a_ab[i])
    pw = a_ab
    pw_bd = ew(lambda i: bdiag(pw[i]))
    for _ in range(5):
        pw = ew(lambda i: _mm(pw[i], pw_bd[i]))
        pw_bd = ew(lambda i: bdiag(pw[i]))
        t = ew(lambda i: t[i] + _mm(t[i], pw_bd[i]))
    v_bd = ew(lambda i: bdiag(vc[i]))
    wt = ew(lambda i: _mm(t[i], bdiag(at[i])).astype(BF16))
    u0 = ew(lambda i: _mm(t[i], bdiag(_mm(a_ak[i], v_bd[i]))))
    y0 = ew(lambda i: _mm(a_rk[i], v_bd[i]))
    wr = ew(lambda i: jnp.concatenate([wt[i], rt[i]], axis=0))
    e_rem = ew(lambda i: jnp.exp(cs[i][C - 1:C, :] - cs[i]))
    kb = ew(lambda i: jnp.concatenate([kc[i] * e_rem[i], bc[i] * e_rem[i]], axis=0).astype(BF16))

    ht = [ht_ref[bi] for bi in range(bsz)]
    for i, (bi, c) in enumerate(chains):
        sh = _mm_nt(wr[i], ht[bi])
        u = sh[0:C] + u0[i]
        y_s[sl[i]] = sh[C:2 * C] + y0[i] + _mm(a_rb[i], bdiag(u))
        upd = _mm_tn(jnp.concatenate([vc[i], u], axis=0), kb[i])
        ht[bi] = ht[bi] * jnp.exp(cs[i][C - 1:C, :]) + jnp.where(head_eq, upd, 0.0)
    for bi in range(bsz):
        ht_ref[bi] = ht[bi]

    for bi in range(bsz):
        y = y_s[bi]
        mu = _mm_sel_l(y, gmean, 2)
        d = y - mu
        var = _mm_sel_l(d * d, gmean, 2)
        y = d * lax.rsqrt(var + RW_GN_EPS) * gng_ref[...] + gnb_ref[...]
        bonus = _mm_sel_l(r_s[bi] * k_s[bi] * rk_ref[...], gsum, 2) * v_s[bi]
        o_ref[bi] = (y + bonus) * g_s[bi]


def _rwkv(p, v_first, mu, w0, w_up, a0, a_up, g_up, k_k, k_a, r_k, gn_g, gn_b, v0, v_up):
    bsz, T, width = p.shape
    has_vres = v_first is not None
    row = lambda x: x.reshape(1, -1)
    pad_rows = lambda m: jnp.zeros((LANE, m.shape[1]), F32).at[:m.shape[0]].set(m)
    blk = lambda w: pl.BlockSpec((bsz, RW_TB, w), lambda t: (0, t, 0))
    vec = _const_spec((1, RW_W))
    args = [p]
    specs = [blk(width)]
    if has_vres:
        args.append(v_first)
        specs.append(blk(RW_W))
    args += [row(mu), row(w0), pad_rows(w_up), row(a0), pad_rows(a_up), g_up, row(k_k), row(k_a),
             row(r_k), row(gn_g), row(gn_b)]
    specs += [_const_spec((1, width)), vec, _const_spec((LANE, RW_W)), vec, _const_spec((LANE, RW_W)),
              _const_spec((LANE, RW_W)), vec, vec, vec, vec, vec]
    if has_vres:
        args += [row(v0), pad_rows(v_up)]
        specs += [vec, _const_spec((LANE, RW_W))]
        out_shape = jax.ShapeDtypeStruct((bsz, T, RW_W), F32)
        out_specs = blk(RW_W)
    else:
        out_shape = [jax.ShapeDtypeStruct((bsz, T, RW_W), F32)] * 2
        out_specs = [blk(RW_W)] * 2
    seq = pltpu.VMEM((bsz, RW_TB, RW_W), F32)
    return pl.pallas_call(
        functools.partial(_rwkv_kernel, has_vres=has_vres, bsz=bsz),
        out_shape=out_shape,
        grid=(T // RW_TB,),
        in_specs=specs,
        out_specs=out_specs,
        scratch_shapes=[pltpu.VMEM((bsz, RW_W, RW_W), F32), pltpu.VMEM((bsz, 1, width), F32)] + [seq] * 9,
        compiler_params=pltpu.CompilerParams(dimension_semantics=("arbitrary",),
                                             vmem_limit_bytes=VMEM_LIMIT),
        name="rwkv7_mixer",
    )(*args)


def _mamba_kernel(p_ref, cw_ref, cb_ref, dtb_ref, alog_ref, d_ref, ng_ref, o_ref, st_ref, tail_ref):
    L, TB = MB_L, MB_TB

    @pl.when(pl.program_id(1) == 0)
    def _():
        st_ref[...] = jnp.zeros_like(st_ref)
        tail_ref[...] = jnp.zeros_like(tail_ref)

    z = p_ref[0, :, 0:512]
    xbc = p_ref[0, :, 512:1536]
    dt_raw = p_ref[0, :, 1536:1664]

    tail = tail_ref[...]
    row8 = _iota((8, MB_CONV_DIM), 0)
    conv = cb_ref[...] + xbc * cw_ref[MB_CONV - 1:MB_CONV, :]
    for s in range(1, MB_CONV):
        rolled = pltpu.roll(xbc, s, 0)
        head = jnp.where(row8 < s, pltpu.roll(tail, s, 0), rolled[0:8])
        shifted = jnp.concatenate([head, rolled[8:]], axis=0)
        conv = conv + shifted * cw_ref[MB_CONV - 1 - s:MB_CONV - s, :]
    tail_ref[...] = xbc[TB - 8:TB, :]
    xbc = _silu(conv)
    x = xbc[:, 0:512]
    bm = xbc[:, 512:768]
    cm = xbc[:, 768:1024]
    dt = _softplus(dt_raw + dtb_ref[...])
    a_neg = -jnp.exp(alog_ref[...])

    tril = _iota((L, L), 0) >= _iota((L, L), 1)
    tril_b = tril.astype(BF16)
    expand = (_iota((LANE, MB_DI), 0) == (_iota((LANE, MB_DI), 1) // MB_P)).astype(BF16)
    lane_lo = _iota((1, LANE), 1) < MB_P
    gw = MB_DI // MB_G

    ys = []
    for c in range(TB // L):
        sl = slice(c * L, (c + 1) * L)
        dtc = dt[sl]
        cs = _mm_sel_r(tril_b, dtc * a_neg, 3)
        cs_t = cs.T
        cs_x = _mm_sel_l(cs, expand, 3)
        dt_x = _mm_sel_l(dtc, expand, 3)
        cs_last = cs_x[L - 1:L, :]
        xc = x[sl]
        xdt = xc * dt_x
        xdec = xdt * jnp.exp(cs_last - cs_x)
        e_in = jnp.exp(cs_x)
        st = st_ref[...]
        y_parts = []
        st_parts = []
        for g in range(MB_G):
            bg = bm[sl, g * MB_N:(g + 1) * MB_N]
            cg = cm[sl, g * MB_N:(g + 1) * MB_N]
            cbm = _mm_nt(cg, bg)
            for hp in range(2):
                pair = g * 2 + hp
                xp = xdt[:, pair * LANE:(pair + 1) * LANE]
                yh = []
                for e in range(2):
                    h = pair * 2 + e
                    diff = cs[:, h:h + 1] - cs_t[h:h + 1, :]
                    lmat = jnp.exp(jnp.where(tril, diff, -1e30))
                    yh.append(_mm(cbm * lmat, xp))
                y_parts.append(jnp.where(lane_lo, yh[0], yh[1]))
            st_g = st[:, g * gw:(g + 1) * gw]
            y_parts[-2] = y_parts[-2] + _mm(cg, st_g[:, 0:LANE]) * e_in[:, g * gw:g * gw + LANE]
            y_parts[-1] = y_parts[-1] + _mm(cg, st_g[:, LANE:gw]) * e_in[:, g * gw + LANE:(g + 1) * gw]
            st_parts.append(st_g * jnp.exp(cs_last[:, g * gw:(g + 1) * gw])
                            + _mm_tn(bg, xdec[:, g * gw:(g + 1) * gw]))
        st_ref[...] = jnp.concatenate(st_parts, axis=1)
        ys.append(jnp.concatenate(y_parts, axis=1) + d_ref[...] * xc)
    y = jnp.concatenate(ys, axis=0) * _silu(z)
    outs = []
    for g in range(MB_G):
        yg = y[:, g * gw:(g + 1) * gw]
        outs.append(yg * lax.rsqrt(jnp.mean(yg * yg, axis=-1, keepdims=True) + LN_EPS))
    o_ref[0] = jnp.concatenate(outs, axis=1) * ng_ref[...]


def _mamba(p, conv_w, conv_b, dt_bias, a_log, d, norm_g):
    bsz, T, _ = p.shape
    pad = lambda vec: jnp.zeros((1, LANE), F32).at[0, :MB_H].set(vec)
    return pl.pallas_call(
        _mamba_kernel,
        out_shape=jax.ShapeDtypeStruct((bsz, T, MB_DI), F32),
        grid=(bsz, T // MB_TB),
        in_specs=[pl.BlockSpec((1, MB_TB, MB_PAD), lambda b, t: (b, t, 0)),
                  _const_spec((MB_CONV, MB_CONV_DIM)), _const_spec((1, MB_CONV_DIM)),
                  _const_spec((1, LANE)), _const_spec((1, LANE)), _const_spec((1, MB_DI)),
                  _const_spec((1, MB_DI))],
        out_specs=pl.BlockSpec((1, MB_TB, MB_DI), lambda b, t: (b, t, 0)),
        scratch_shapes=[pltpu.VMEM((MB_N, MB_DI), F32), pltpu.VMEM((8, MB_CONV_DIM), F32)],
        compiler_params=pltpu.CompilerParams(dimension_semantics=("parallel", "arbitrary"),
                                             vmem_limit_bytes=VMEM_LIMIT),
        name="mamba2_mixer",
    )(p, conv_w, conv_b.reshape(1, -1), pad(dt_bias), pad(a_log),
      jnp.repeat(d, MB_P).reshape(1, -1), norm_g.reshape(1, -1))


def _pad_cols(w, width):
    return jnp.pad(w, ((0, 0), (0, width - w.shape[1])))


def _proj_weights(w_in_l, w_vres_l):
    g = w_in_l[:, :GLA_COLS]
    gla = jnp.concatenate([g[:, 0:512], g[:, 528:784], _pad_cols(g[:, 512:528], LANE)], axis=1)
    r = w_in_l[:, GLA_COLS:GLA_COLS + RW_COLS]
    parts = [r[:, 0:768], _pad_cols(r[:, 768:832], LANE), _pad_cols(r[:, 832:896], LANE), r[:, 896:1024]]
    if w_vres_l is not None:
        parts.append(_pad_cols(w_vres_l, LANE))
    rw = jnp.concatenate(parts, axis=1)
    mb = _pad_cols(w_in_l[:, GLA_COLS + RW_COLS:N_IN], MB_PAD)
    return gla.astype(BF16), rw.astype(BF16), mb.astype(BF16)


def _rw_mu(mu_l, mu_vres_l):
    pad = lambda vec: jnp.pad(vec, (0, LANE - vec.shape[0]))
    parts = [mu_l[0:768], pad(mu_l[768:832]), pad(mu_l[832:896]), mu_l[896:1024]]
    if mu_vres_l is not None:
        parts.append(pad(mu_vres_l))
    return jnp.concatenate(parts)


def kernel(x, ln_g, ln_b, ffn_w_in, ffn_w_down, w_in, w_in_vres, w_out, gla_gk_up, gla_gk_bias, gla_norm_g, rw_mu, rw_mu_vres, rw_w0, rw_w_up, rw_a0, rw_a_up, rw_g_up, rw_k_k, rw_k_a, rw_r_k, rw_gn_g, rw_gn_b, rw_v0, rw_v_up, mb_conv_w, mb_conv_b, mb_dt_bias, mb_A_log, mb_D, mb_norm_g):
    bsz, T, _ = x.shape
    n = bsz * T
    x2 = x.reshape(n, D_MODEL)
    v_first = None
    for l in range(DEPTH):
        x2 = _ffn_ln(x2, ffn_w_in[l, 0], ffn_w_down[l, 0], ln_g[l, 0], ln_b[l, 0])
        vres = l > 0
        wa, wb, wc = _proj_weights(w_in[l], w_in_vres[l - 1] if vres else None)
        gla_p, rw_p, mb_p = _in_proj(x2, wa, wb, wc)
        gla_o = _gla(gla_p.reshape(bsz, T, -1), gla_gk_up[l], gla_gk_bias[l], gla_norm_g[l])
        mu = _rw_mu(rw_mu[l], rw_mu_vres[l - 1] if vres else None)
        rw_args = (rw_p.reshape(bsz, T, -1), v_first, mu, rw_w0[l], rw_w_up[l], rw_a0[l], rw_a_up[l],
                   rw_g_up[l], rw_k_k[l], rw_k_a[l], rw_r_k[l].reshape(-1), rw_gn_g[l], rw_gn_b[l])
        if vres:
            rw_o = _rwkv(*rw_args, rw_v0[l - 1], rw_v_up[l - 1])
        else:
            rw_o, v_first = _rwkv(*rw_args, None, None)
        mb_o = _mamba(mb_p.reshape(bsz, T, -1), mb_conv_w[l], mb_conv_b[l], mb_dt_bias[l], mb_A_log[l],
                      mb_D[l], mb_norm_g[l])
        x2 = _out_proj_ln(x2, gla_o.reshape(n, -1), rw_o.reshape(n, -1), mb_o.reshape(n, -1),
                          w_out[l], ln_g[l, 1], ln_b[l, 1])
        x2 = _ffn_ln(x2, ffn_w_in[l, 1], ffn_w_down[l, 1], ln_g[l, 2], ln_b[l, 2])
    return x2.reshape(bsz, T, D_MODEL)
```

```python
import functools

import jax
import jax.numpy as jnp
from jax import lax
from jax.experimental import pallas as pl
from jax.experimental.pallas import tpu as pltpu

F32 = jnp.float32
BF16 = jnp.bfloat16

D_MODEL = 1024
DEPTH = 2
D_FF = 2816
LN_EPS = 1e-5
DN_ALPHA = (2 * DEPTH) ** 0.25

GLA_H, GLA_DK, GLA_DV = 4, 32, 64
GLA_GATE_RANK = 16
GLA_GATE_NORM = 16.0
GLA_QK = GLA_H * GLA_DK
GLA_W = GLA_H * GLA_DV
GLA_COLS = 784
GLA_PAD = 896

RW_H, RW_DH = 4, 64
RW_W = 256
RW_COLS = 1024
RW_V_LORA = 32
RW_GN_EPS = 64e-5
RW_PAD0 = 1152
RW_PAD1 = 1280

MB_H, MB_P, MB_DI, MB_G, MB_N = 8, 64, 512, 2, 128
MB_CONV = 4
MB_CONV_DIM = 1024
MB_COLS = 1544
MB_PAD = 1664
N_IN = GLA_COLS + RW_COLS + MB_COLS

LANE = 128
VMEM_LIMIT = 56 * 1024 * 1024

FFN_TM = 512
FFN_FC = 256
FFN_LN_ROWS = 32
PROJ_TM = 512
GLA_TB, GLA_C = 512, 64
RW_TB, RW_C = 256, 64
MB_TB, MB_L = 256, 128


def _mm(a, b):
    return jnp.dot(a.astype(BF16), b.astype(BF16), preferred_element_type=F32)


def _mm_nt(a, b):
    return lax.dot_general(a.astype(BF16), b.astype(BF16), (((1,), (1,)), ((), ())),
                           preferred_element_type=F32)


def _mm_tn(a, b):
    return lax.dot_general(a.astype(BF16), b.astype(BF16), (((0,), (0,)), ((), ())),
                           preferred_element_type=F32)


def _split(x, parts):
    out = []
    for _ in range(parts - 1):
        piece = x.astype(BF16)
        out.append(piece)
        x = x - piece.astype(F32)
    out.append(x.astype(BF16))
    return out


def _mm_sel_l(x, sel, parts):
    acc = None
    for piece in _split(x, parts):
        term = jnp.dot(piece, sel, preferred_element_type=F32)
        acc = term if acc is None else acc + term
    return acc


def _mm_sel_r(sel, x, parts):
    acc = None
    for piece in _split(x, parts):
        term = jnp.dot(sel, piece, preferred_element_type=F32)
        acc = term if acc is None else acc + term
    return acc


def _layer_norm(y, g, b):
    mu = jnp.mean(y, axis=-1, keepdims=True)
    d = y - mu
    var = jnp.mean(d * d, axis=-1, keepdims=True)
    return d * lax.rsqrt(var + LN_EPS) * g + b


def _silu(x):
    return x * jax.nn.sigmoid(x)


def _softplus(x):
    return jnp.maximum(x, 0.0) + jnp.log(1.0 + jnp.exp(-jnp.abs(x)))


def _iota(shape, dim):
    return lax.broadcasted_iota(jnp.int32, shape, dim)


def _const_spec(shape):
    n = len(shape)
    return pl.BlockSpec(shape, lambda *_: (0,) * n, pipeline_mode=pl.Buffered(1))


def _ffn_ln_kernel(x_ref, win_ref, wd_ref, g_ref, b_ref, o_ref, acc_ref, *, n_chunks):
    xb = x_ref[...].astype(BF16)
    for c in range(n_chunks):
        gate = jnp.dot(xb, win_ref[:, c * FFN_FC:(c + 1) * FFN_FC], preferred_element_type=F32)
        up = jnp.dot(xb, win_ref[:, D_FF + c * FFN_FC:D_FF + (c + 1) * FFN_FC], preferred_element_type=F32)
        act = (_silu(gate) * up).astype(BF16)
        part = jnp.dot(act, wd_ref[c], preferred_element_type=F32)
        if c == 0:
            acc_ref[...] = part
        else:
            acc_ref[...] += part
    for r in range(0, FFN_TM, FFN_LN_ROWS):
        rows = pl.ds(r, FFN_LN_ROWS)
        y = DN_ALPHA * x_ref[rows, :] + 0.5 * acc_ref[rows, :]
        o_ref[rows, :] = _layer_norm(y, g_ref[...], b_ref[...])


def _ffn_ln(x2, w_in, w_down, g, b):
    n = x2.shape[0]
    nc = D_FF // FFN_FC
    wd = w_down.astype(BF16).reshape(nc, FFN_FC, D_MODEL)
    return pl.pallas_call(
        functools.partial(_ffn_ln_kernel, n_chunks=nc),
        out_shape=jax.ShapeDtypeStruct((n, D_MODEL), F32),
        grid=(n // FFN_TM,),
        in_specs=[
            pl.BlockSpec((FFN_TM, D_MODEL), lambda i: (i, 0)),
            _const_spec((D_MODEL, 2 * D_FF)),
            _const_spec((nc, FFN_FC, D_MODEL)),
            _const_spec((1, D_MODEL)),
            _const_spec((1, D_MODEL)),
        ],
        out_specs=pl.BlockSpec((FFN_TM, D_MODEL), lambda i: (i, 0)),
        scratch_shapes=[pltpu.VMEM((FFN_TM, D_MODEL), F32)],
        compiler_params=pltpu.CompilerParams(dimension_semantics=("parallel",),
                                             vmem_limit_bytes=VMEM_LIMIT),
        name="ffn_ln",
    )(x2, w_in.astype(BF16), wd, g.reshape(1, -1), b.reshape(1, -1))


def _in_proj_kernel(x_ref, wa_ref, wb_ref, wc_ref, oa_ref, ob_ref, oc_ref):
    xb = x_ref[...].astype(BF16)
    oa_ref[...] = jnp.dot(xb, wa_ref[...], preferred_element_type=F32)
    ob_ref[...] = jnp.dot(xb, wb_ref[...], preferred_element_type=F32)
    oc_ref[...] = jnp.dot(xb, wc_ref[...], preferred_element_type=F32)


def _in_proj(x2, wa, wb, wc):
    n = x2.shape[0]
    widths = (wa.shape[1], wb.shape[1], wc.shape[1])
    return pl.pallas_call(
        _in_proj_kernel,
        out_shape=[jax.ShapeDtypeStruct((n, w), F32) for w in widths],
        grid=(n // PROJ_TM,),
        in_specs=[pl.BlockSpec((PROJ_TM, D_MODEL), lambda i: (i, 0))]
        + [_const_spec((D_MODEL, w)) for w in widths],
        out_specs=[pl.BlockSpec((PROJ_TM, w), lambda i: (i, 0)) for w in widths],
        compiler_params=pltpu.CompilerParams(dimension_semantics=("parallel",),
                                             vmem_limit_bytes=VMEM_LIMIT),
        name="in_proj",
    )(x2, wa, wb, wc)


def _out_proj_ln_kernel(x_ref, ma_ref, mb_ref, mc_ref, wa_ref, wb_ref, wc_ref, g_ref, b_ref, o_ref):
    acc = jnp.dot(ma_ref[...].astype(BF16), wa_ref[...], preferred_element_type=F32)
    acc += jnp.dot(mb_ref[...].astype(BF16), wb_ref[...], preferred_element_type=F32)
    acc += jnp.dot(mc_ref[...].astype(BF16), wc_ref[...], preferred_element_type=F32)
    o_ref[...] = _layer_norm(DN_ALPHA * x_ref[...] + acc, g_ref[...], b_ref[...])


def _out_proj_ln(x2, ma, mb, mc, w_out, g, b):
    n = x2.shape[0]
    wo = w_out.astype(BF16)
    wa, wb, wc = wo[:GLA_W], wo[GLA_W:GLA_W + RW_W], wo[GLA_W + RW_W:]
    tm = PROJ_TM
    row = lambda w: pl.BlockSpec((tm, w), lambda i: (i, 0))
    return pl.pallas_call(
        _out_proj_ln_kernel,
        out_shape=jax.ShapeDtypeStruct((n, D_MODEL), F32),
        grid=(n // tm,),
        in_specs=[row(D_MODEL), row(GLA_W), row(RW_W), row(MB_DI),
                  _const_spec((GLA_W, D_MODEL)), _const_spec((RW_W, D_MODEL)),
                  _const_spec((MB_DI, D_MODEL)), _const_spec((1, D_MODEL)), _const_spec((1, D_MODEL))],
        out_specs=row(D_MODEL),
        compiler_params=pltpu.CompilerParams(dimension_semantics=("parallel",),
                                             vmem_limit_bytes=VMEM_LIMIT),
        name="out_proj_ln",
    )(x2, ma, mb, mc, wa, wb, wc, g.reshape(1, -1), b.reshape(1, -1))


def _gla_kernel(p_ref, gkup_ref, gkb_ref, ng_ref, o_ref, st_ref, *, bsz):
    C = GLA_C

    @pl.when(pl.program_id(0) == 0)
    def _():
        st_ref[...] = jnp.zeros_like(st_ref)

    TB = GLA_TB
    tril_blk = (((_iota((TB, TB), 0) // C) == (_iota((TB, TB), 1) // C))
                & (_iota((TB, TB), 0) >= _iota((TB, TB), 1))).astype(BF16)
    tril4 = (_iota((GLA_H * C, C), 0) % C) >= _iota((GLA_H * C, C), 1)
    qhead = _iota((1, GLA_QK), 1) // GLA_DK
    vhead = _iota((1, GLA_W), 1) // GLA_DV
    smask = (_iota((GLA_W, GLA_QK), 0) // GLA_DV) == (_iota((GLA_W, GLA_QK), 1) // GLA_DK)
    gmat = ((_iota((GLA_W, GLA_W), 0) // GLA_DV) == (_iota((GLA_W, GLA_W), 1) // GLA_DV)).astype(BF16) / GLA_DV

    eb = lambda f: [f(bi) for bi in range(bsz)]
    z = eb(lambda bi: _mm(p_ref[bi, :, 768:896], gkup_ref[...]) + gkb_ref[...])
    gk = eb(lambda bi: -_softplus(-z[bi]) / GLA_GATE_NORM)
    b_all = eb(lambda bi: _mm_sel_r(tril_blk, gk[bi], 3))
    qe_all = eb(lambda bi: p_ref[bi, :, 0:128] * (GLA_DK ** -0.5) * jnp.exp(b_all[bi]))
    ke_all = eb(lambda bi: p_ref[bi, :, 128:256] * jnp.exp(-b_all[bi]))

    n = TB // C
    chains = [(bi, c) for c in range(n) for bi in range(bsz)]
    ew = lambda f: [f(bi, slice(c * C, (c + 1) * C)) for bi, c in chains]
    b_last = ew(lambda bi, s: b_all[bi][s.stop - 1:s.stop, :])
    v = ew(lambda bi, s: p_ref[bi, s, 256:512])
    q_e = ew(lambda bi, s: qe_all[bi][s])
    k_e = ew(lambda bi, s: ke_all[bi][s])
    kdec = ew(lambda bi, s: p_ref[bi, s, 128:256] * jnp.exp(b_all[bi][s.stop - 1:s.stop, :] - b_all[bi][s]))
    ei = lambda f: [f(i) for i in range(len(chains))]
    q_st = ei(lambda i: jnp.concatenate([jnp.where(qhead == h, q_e[i], 0.0) for h in range(GLA_H)], axis=0))
    att = ei(lambda i: jnp.where(tril4, _mm_nt(q_st[i], k_e[i]), 0.0))
    r = ei(lambda i: _mm(att[i], v[i]))
    upd = ei(lambda i: jnp.where(smask, _mm_tn(v[i], kdec[i]), 0.0))

    st = [st_ref[bi] for bi in range(bsz)]
    outs = [[] for _ in range(bsz)]
    for i, (bi, c) in enumerate(chains):
        o = _mm_nt(q_e[i], st[bi])
        for h in range(GLA_H):
            o = o + jnp.where(vhead == h, r[i][h * C:(h + 1) * C], 0.0)
        outs[bi].append(o)
        st[bi] = st[bi] * jnp.exp(b_last[i]) + upd[i]
    for bi in range(bsz):
        st_ref[bi] = st[bi]
        o = jnp.concatenate(outs[bi], axis=0)
        ms = _mm_sel_l(o * o, gmat, 2)
        o = o * lax.rsqrt(ms + LN_EPS) * ng_ref[...]
        o_ref[bi] = o * _silu(p_ref[bi, :, 512:768])


def _gla(p, gk_up, gk_bias, norm_g):
    bsz, T, _ = p.shape
    gkup = jnp.zeros((LANE, GLA_QK), F32).at[:GLA_GATE_RANK].set(gk_up)
    return pl.pallas_call(
        functools.partial(_gla_kernel, bsz=bsz),
        out_shape=jax.ShapeDtypeStruct((bsz, T, GLA_W), F32),
        grid=(T // GLA_TB,),
        in_specs=[pl.BlockSpec((bsz, GLA_TB, GLA_PAD), lambda t: (0, t, 0)),
                  _const_spec((LANE, GLA_QK)), _const_spec((1, GLA_QK)), _const_spec((1, GLA_W))],
        out_specs=pl.BlockSpec((bsz, GLA_TB, GLA_W), lambda t: (0, t, 0)),
        scratch_shapes=[pltpu.VMEM((bsz, GLA_W, GLA_QK), F32)],
        compiler_params=pltpu.CompilerParams(dimension_semantics=("arbitrary",),
                                             vmem_limit_bytes=VMEM_LIMIT),
        name="gla_mixer",
    )(p, gkup, gk_bias.reshape(1, -1), jnp.tile(norm_g, GLA_H).reshape(1, -1))


def _rwkv_kernel(*refs, has_vres, bsz):
    if has_vres:
        (p_ref, vf_ref, mu_ref, w0_ref, wup_ref, a0_ref, aup_ref, gup_ref, kk_ref, ka_ref, rk_ref,
         gng_ref, gnb_ref, v0_ref, vup_ref, o_ref,
         ht_ref, prev_ref, r_s, k_s, v_s, a_s, b_s, lw_s, cs_s, g_s, y_s) = refs
    else:
        (p_ref, mu_ref, w0_ref, wup_ref, a0_ref, aup_ref, gup_ref, kk_ref, ka_ref, rk_ref,
         gng_ref, gnb_ref, o_ref, vf_out_ref,
         ht_ref, prev_ref, r_s, k_s, v_s, a_s, b_s, lw_s, cs_s, g_s, y_s) = refs
    C, TB, W, H = RW_C, RW_TB, RW_W, RW_H

    @pl.when(pl.program_id(0) == 0)
    def _():
        ht_ref[...] = jnp.zeros_like(ht_ref)
        prev_ref[...] = jnp.zeros_like(prev_ref)

    head_eq = (_iota((W, W), 0) // RW_DH) == (_iota((W, W), 1) // RW_DH)
    gsum = head_eq.astype(BF16)
    gmean = (head_eq.astype(F32) / RW_DH).astype(BF16)
    tril_blk = (((_iota((TB, TB), 0) // C) == (_iota((TB, TB), 1) // C))
                & (_iota((TB, TB), 0) >= _iota((TB, TB), 1))).astype(BF16)

    for bi in range(bsz):
        p = p_ref[bi]
        shifted = jnp.where(_iota(p.shape, 0) == 0, prev_ref[bi], pltpu.roll(p, 1, 0))
        prev_ref[bi] = p[TB - 1:TB, :]
        xs = p + (shifted - p) * mu_ref[...]
        r = xs[:, 0:256]
        k = xs[:, 256:512]
        v = xs[:, 512:768]
        w_low = xs[:, 768:896]
        a_low = xs[:, 896:1024]
        g_low = xs[:, 1024:1152]
        if has_vres:
            v_low = xs[:, 1152:1280]
            v = v + (vf_ref[bi] - v) * jax.nn.sigmoid(v0_ref[...] + _mm(v_low, vup_ref[...]))
        else:
            vf_out_ref[bi] = v
        w = -_softplus(-(w0_ref[...] + _mm(jnp.tanh(w_low), wup_ref[...]))) - 0.5
        lw = -jnp.exp(w)
        lw_s[bi] = lw
        cs_s[bi] = _mm_sel_r(tril_blk, lw, 3)
        a = jax.nn.sigmoid(a0_ref[...] + _mm(a_low, aup_ref[...]))
        g_s[bi] = _mm(jax.nn.sigmoid(g_low), gup_ref[...])
        kk = k * kk_ref[...]
        kk = kk / jnp.maximum(jnp.sqrt(_mm_sel_l(kk * kk, gsum, 2)), 1e-12)
        r_s[bi] = r
        k_s[bi] = k * (1.0 + (a - 1.0) * ka_ref[...])
        v_s[bi] = v
        a_s[bi] = -kk
        b_s[bi] = kk * a

    ri = _iota((C, W), 0)
    li = _iota((C, W), 1) % C
    strict = ri > li
    incl = ri >= li
    eye = (ri == li).astype(F32)
    blk_eq = (_iota((H * C, W), 0) // C) == (_iota((H * C, W), 1) // RW_DH)

    def bdiag(m):
        return jnp.where(blk_eq, jnp.concatenate([m] * H, axis=0), 0.0).astype(BF16)

    chains = [(bi, c) for c in range(TB // C) for bi in range(bsz)]
    ew = lambda f: [f(i) for i in range(len(chains))]
    sl = [(bi, slice(c * C, (c + 1) * C)) for bi, c in chains]
    cs = ew(lambda i: cs_s[sl[i]])
    kc = ew(lambda i: k_s[sl[i]])
    bc = ew(lambda i: b_s[sl[i]])
    vc = ew(lambda i: v_s[sl[i]])
    e_neg = ew(lambda i: jnp.exp(-cs[i]))
    at = ew(lambda i: a_s[sl[i]] * jnp.exp(cs[i] - lw_s[sl[i]]))
    rt = ew(lambda i: (r_s[sl[i]] * jnp.exp(cs[i])).astype(BF16))
    ar = ew(lambda i: jnp.concatenate([at[i].astype(BF16), rt[i]], axis=0))
    pk = ew(lambda i: _mm_nt(ar[i], bdiag(kc[i] * e_neg[i])))
    pb = ew(lambda i: _mm_nt(ar[i], bdiag(bc[i] * e_neg[i])))
    a_ak = ew(lambda i: jnp.where(strict, pk[i][0:C], 0.0))
    a_rk = ew(lambda i: jnp.where(incl, pk[i][C:2 * C], 0.0))
    a_ab = ew(lambda i: jnp.where(strict, pb[i][0:C], 0.0))
    a_rb = ew(lambda i: jnp.where(incl, pb[i][C:2 * C], 0.0).astype(BF16))
    t = ew(lambda i: eye + a_ab[i])
    pw = a_ab
    pw_bd = ew(lambda i: bdiag(pw[i]))
    for _ in range(5):
        pw = ew(lambda i: _mm(pw[i], pw_bd[i]))
        pw_bd = ew(lambda i: bdiag(pw[i]))
        t = ew(lambda i: t[i] + _mm(t[i], pw_bd[i]))
    v_bd = ew(lambda i: bdiag(vc[i]))
    wt = ew(lambda i: _mm(t[i], bdiag(at[i])).astype(BF16))
    u0 = ew(lambda i: _mm(t[i], bdiag(_mm(a_ak[i], v_bd[i]))))
    y0 = ew(lambda i: _mm(a_rk[i], v_bd[i]))
    wr = ew(lambda i: jnp.concatenate([wt[i], rt[i]], axis=0))
    e_rem = ew(lambda i: jnp.exp(cs[i][C - 1:C, :] - cs[i]))
    kb = ew(lambda i: jnp.concatenate([kc[i] * e_rem[i], bc[i] * e_rem[i]], axis=0).astype(BF16))

    ht = [ht_ref[bi] for bi in range(bsz)]
    for i, (bi, c) in enumerate(chains):
        sh = _mm_nt(wr[i], ht[bi])
        u = sh[0:C] + u0[i]
        y_s[sl[i]] = sh[C:2 * C] + y0[i] + _mm(a_rb[i], bdiag(u))
        upd = _mm_tn(jnp.concatenate([vc[i], u], axis=0), kb[i])
        ht[bi] = ht[bi] * jnp.exp(cs[i][C - 1:C, :]) + jnp.where(head_eq, upd, 0.0)
    for bi in range(bsz):
        ht_ref[bi] = ht[bi]

    for bi in range(bsz):
        y = y_s[bi]
        mu = _mm_sel_l(y, gmean, 2)
        d = y - mu
        var = _mm_sel_l(d * d, gmean, 2)
        y = d * lax.rsqrt(var + RW_GN_EPS) * gng_ref[...] + gnb_ref[...]
        bonus = _mm_sel_l(r_s[bi] * k_s[bi] * rk_ref[...], gsum, 2) * v_s[bi]
        o_ref[bi] = (y + bonus) * g_s[bi]


def _rwkv(p, v_first, mu, w0, w_up, a0, a_up, g_up, k_k, k_a, r_k, gn_g, gn_b, v0, v_up):
    bsz, T, width = p.shape
    has_vres = v_first is not None
    row = lambda x: x.reshape(1, -1)
    pad_rows = lambda m: jnp.zeros((LANE, m.shape[1]), F32).at[:m.shape[0]].set(m)
    blk = lambda w: pl.BlockSpec((bsz, RW_TB, w), lambda t: (0, t, 0))
    vec = _const_spec((1, RW_W))
    args = [p]
    specs = [blk(width)]
    if has_vres:
        args.append(v_first)
        specs.append(blk(RW_W))
    args += [row(mu), row(w0), pad_rows(w_up), row(a0), pad_rows(a_up), g_up, row(k_k), row(k_a),
             row(r_k), row(gn_g), row(gn_b)]
    specs += [_const_spec((1, width)), vec, _const_spec((LANE, RW_W)), vec, _const_spec((LANE, RW_W)),
              _const_spec((LANE, RW_W)), vec, vec, vec, vec, vec]
    if has_vres:
        args += [row(v0), pad_rows(v_up)]
        specs += [vec, _const_spec((LANE, RW_W))]
        out_shape = jax.ShapeDtypeStruct((bsz, T, RW_W), F32)
        out_specs = blk(RW_W)
    else:
        out_shape = [jax.ShapeDtypeStruct((bsz, T, RW_W), F32)] * 2
        out_specs = [blk(RW_W)] * 2
    seq = pltpu.VMEM((bsz, RW_TB, RW_W), F32)
    return pl.pallas_call(
        functools.partial(_rwkv_kernel, has_vres=has_vres, bsz=bsz),
        out_shape=out_shape,
        grid=(T // RW_TB,),
        in_specs=specs,
        out_specs=out_specs,
        scratch_shapes=[pltpu.VMEM((bsz, RW_W, RW_W), F32), pltpu.VMEM((bsz, 1, width), F32)] + [seq] * 9,
        compiler_params=pltpu.CompilerParams(dimension_semantics=("arbitrary",),
                                             vmem_limit_bytes=VMEM_LIMIT),
        name="rwkv7_mixer",
    )(*args)


def _mamba_kernel(p_ref, cw_ref, cb_ref, dtb_ref, alog_ref, d_ref, ng_ref, o_ref, st_ref, xpad_ref, *, bsz):
    L, TB = MB_L, MB_TB

    @pl.when(pl.program_id(0) == 0)
    def _():
        st_ref[...] = jnp.zeros_like(st_ref)
        xpad_ref[:, 0:8, :] = jnp.zeros((bsz, 8, MB_CONV_DIM), F32)

    tril = _iota((L, L), 0) >= _iota((L, L), 1)
    tril_b = tril.astype(BF16)
    expand = (_iota((LANE, MB_DI), 0) == (_iota((LANE, MB_DI), 1) // MB_P)).astype(BF16)
    lane_lo = _iota((1, LANE), 1) < MB_P
    gw = MB_DI // MB_G
    a_neg = -jnp.exp(alog_ref[...])

    xs, bms, cms, dts = [], [], [], []
    for bi in range(bsz):
        xpad_ref[bi, 8:8 + TB, :] = p_ref[bi, :, 512:1536]
        conv = cb_ref[...] + xpad_ref[bi, 8:8 + TB, :] * cw_ref[MB_CONV - 1:MB_CONV, :]
        for s in range(1, MB_CONV):
            conv = conv + xpad_ref[bi, 8 - s:8 - s + TB, :] * cw_ref[MB_CONV - 1 - s:MB_CONV - s, :]
        xpad_ref[bi, 0:8, :] = xpad_ref[bi, TB:TB + 8, :]
        xbc = _silu(conv)
        xs.append(xbc[:, 0:512])
        bms.append(xbc[:, 512:768])
        cms.append(xbc[:, 768:1024])
        dts.append(_softplus(p_ref[bi, :, 1536:1664] + dtb_ref[...]))

    chains = [(bi, c) for c in range(TB // L) for bi in range(bsz)]
    pre = []
    for bi, c in chains:
        sl = slice(c * L, (c + 1) * L)
        dtc = dts[bi][sl]
        cs = _mm_sel_r(tril_b, dtc * a_neg, 3)
        cs_t = cs.T
        cs_x = _mm_sel_l(cs, expand, 3)
        dt_x = _mm_sel_l(dtc, expand, 2)
        cs_last = cs_x[L - 1:L, :]
        xc = xs[bi][sl]
        xdt = xc * dt_x
        xdec = xdt * jnp.exp(cs_last - cs_x)
        y_parts, upd, cgs = [], [], []
        for g in range(MB_G):
            bg = bms[bi][sl, g * MB_N:(g + 1) * MB_N]
            cg = cms[bi][sl, g * MB_N:(g + 1) * MB_N]
            cbm = _mm_nt(cg, bg)
            for hp in range(2):
                pair = g * 2 + hp
                xp = xdt[:, pair * LANE:(pair + 1) * LANE]
                yh = []
                for e in range(2):
                    h = pair * 2 + e
                    diff = cs[:, h:h + 1] - cs_t[h:h + 1, :]
                    lmat = jnp.exp(jnp.where(tril, diff, -1e30))
                    yh.append(_mm(cbm * lmat, xp))
                y_parts.append(jnp.where(lane_lo, yh[0], yh[1]))
            upd.append(_mm_tn(bg, xdec[:, g * gw:(g + 1) * gw]))
            cgs.append(cg.astype(BF16))
        pre.append(dict(y=jnp.concatenate(y_parts, axis=1) + d_ref[...] * xc,
                        upd=jnp.concatenate(upd, axis=1), cg=cgs, e_in=jnp.exp(cs_x),
                        dec=jnp.exp(cs_last)))

    st = [st_ref[bi] for bi in range(bsz)]
    ys = [[] for _ in range(bsz)]
    for (bi, c), q in zip(chains, pre):
        stb = st[bi].astype(BF16)
        y_off = jnp.concatenate([_mm(q["cg"][g], stb[:, g * gw:(g + 1) * gw]) for g in range(MB_G)], axis=1)
        ys[bi].append(q["y"] + y_off * q["e_in"])
        st[bi] = st[bi] * q["dec"] + q["upd"]
    for bi in range(bsz):
        st_ref[bi] = st[bi]
        y = jnp.concatenate(ys[bi], axis=0) * _silu(p_ref[bi, :, 0:512])
        outs = []
        for g in range(MB_G):
            yg = y[:, g * gw:(g + 1) * gw]
            outs.append(yg * lax.rsqrt(jnp.mean(yg * yg, axis=-1, keepdims=True) + LN_EPS))
        o_ref[bi] = jnp.concatenate(outs, axis=1) * ng_ref[...]


def _mamba(p, conv_w, conv_b, dt_bias, a_log, d, norm_g):
    bsz, T, _ = p.shape
    pad = lambda vec: jnp.zeros((1, LANE), F32).at[0, :MB_H].set(vec)
    return pl.pallas_call(
        functools.partial(_mamba_kernel, bsz=bsz),
        out_shape=jax.ShapeDtypeStruct((bsz, T, MB_DI), F32),
        grid=(T // MB_TB,),
        in_specs=[pl.BlockSpec((bsz, MB_TB, MB_PAD), lambda t: (0, t, 0)),
                  _const_spec((MB_CONV, MB_CONV_DIM)), _const_spec((1, MB_CONV_DIM)),
                  _const_spec((1, LANE)), _const_spec((1, LANE)), _const_spec((1, MB_DI)),
                  _const_spec((1, MB_DI))],
        out_specs=pl.BlockSpec((bsz, MB_TB, MB_DI), lambda t: (0, t, 0)),
        scratch_shapes=[pltpu.VMEM((bsz, MB_N, MB_DI), F32),
                        pltpu.VMEM((bsz, MB_TB + 8, MB_CONV_DIM), F32)],
        compiler_params=pltpu.CompilerParams(dimension_semantics=("arbitrary",),
                                             vmem_limit_bytes=VMEM_LIMIT),
        name="mamba2_mixer",
    )(p, conv_w, conv_b.reshape(1, -1), pad(dt_bias), pad(a_log),
      jnp.repeat(d, MB_P).reshape(1, -1), norm_g.reshape(1, -1))


def _pad_cols(w, width):
    return jnp.pad(w, ((0, 0), (0, width - w.shape[1])))


def _proj_weights(w_in_l, w_vres_l):
    g = w_in_l[:, :GLA_COLS]
    gla = jnp.concatenate([g[:, 0:512], g[:, 528:784], _pad_cols(g[:, 512:528], LANE)], axis=1)
    r = w_in_l[:, GLA_COLS:GLA_COLS + RW_COLS]
    parts = [r[:, 0:768], _pad_cols(r[:, 768:832], LANE), _pad_cols(r[:, 832:896], LANE), r[:, 896:1024]]
    if w_vres_l is not None:
        parts.append(_pad_cols(w_vres_l, LANE))
    rw = jnp.concatenate(parts, axis=1)
    mb = _pad_cols(w_in_l[:, GLA_COLS + RW_COLS:N_IN], MB_PAD)
    return gla.astype(BF16), rw.astype(BF16), mb.astype(BF16)


def _rw_mu(mu_l, mu_vres_l):
    pad = lambda vec: jnp.pad(vec, (0, LANE - vec.shape[0]))
    parts = [mu_l[0:768], pad(mu_l[768:832]), pad(mu_l[832:896]), mu_l[896:1024]]
    if mu_vres_l is not None:
        parts.append(pad(mu_vres_l))
    return jnp.concatenate(parts)


def kernel(x, ln_g, ln_b, ffn_w_in, ffn_w_down, w_in, w_in_vres, w_out, gla_gk_up, gla_gk_bias, gla_norm_g, rw_mu, rw_mu_vres, rw_w0, rw_w_up, rw_a0, rw_a_up, rw_g_up, rw_k_k, rw_k_a, rw_r_k, rw_gn_g, rw_gn_b, rw_v0, rw_v_up, mb_conv_w, mb_conv_b, mb_dt_bias, mb_A_log, mb_D, mb_norm_g):
    bsz, T, _ = x.shape
    n = bsz * T
    x2 = x.reshape(n, D_MODEL)
    v_first = None
    for l in range(DEPTH):
        x2 = _ffn_ln(x2, ffn_w_in[l, 0], ffn_w_down[l, 0], ln_g[l, 0], ln_b[l, 0])
        vres = l > 0
        wa, wb, wc = _proj_weights(w_in[l], w_in_vres[l - 1] if vres else None)
        gla_p, rw_p, mb_p = _in_proj(x2, wa, wb, wc)
        gla_o = _gla(gla_p.reshape(bsz, T, -1), gla_gk_up[l], gla_gk_bias[l], gla_norm_g[l])
        mu = _rw_mu(rw_mu[l], rw_mu_vres[l - 1] if vres else None)
        rw_args = (rw_p.reshape(bsz, T, -1), v_first, mu, rw_w0[l], rw_w_up[l], rw_a0[l], rw_a_up[l],
                   rw_g_up[l], rw_k_k[l], rw_k_a[l], rw_r_k[l].reshape(-1), rw_gn_g[l], rw_gn_b[l])
        if vres:
            rw_o = _rwkv(*rw_args, rw_v0[l - 1], rw_v_up[l - 1])
        else:
            rw_o, v_first = _rwkv(*rw_args, None, None)
        mb_o = _mamba(mb_p.reshape(bsz, T, -1), mb_conv_w[l], mb_conv_b[l], mb_dt_bias[l], mb_A_log[l],
                      mb_D[l], mb_norm_g[l])
        x2 = _out_proj_ln(x2, gla_o.reshape(n, -1), rw_o.reshape(n, -1), mb_o.reshape(n, -1),
                          w_out[l], ln_g[l, 1], ln_b[l, 1])
        x2 = _ffn_ln(x2, ffn_w_in[l, 1], ffn_w_down[l, 1], ln_g[l, 2], ln_b[l, 2])
    return x2.reshape(bsz, T, D_MODEL)
```

```python
import functools

import jax
import jax.numpy as jnp
from jax import lax
from jax.experimental import pallas as pl
from jax.experimental.pallas import tpu as pltpu

F32 = jnp.float32
BF16 = jnp.bfloat16

D_MODEL = 1024
DEPTH = 2
D_FF = 2816
LN_EPS = 1e-5
DN_ALPHA = (2 * DEPTH) ** 0.25

GLA_H, GLA_DK, GLA_DV = 4, 32, 64
GLA_GATE_RANK = 16
GLA_GATE_NORM = 16.0
GLA_QK = GLA_H * GLA_DK
GLA_W = GLA_H * GLA_DV
GLA_COLS = 784
GLA_PAD = 896

RW_H, RW_DH = 4, 64
RW_W = 256
RW_COLS = 1024
RW_V_LORA = 32
RW_GN_EPS = 64e-5
RW_PAD0 = 1152
RW_PAD1 = 1280

MB_H, MB_P, MB_DI, MB_G, MB_N = 8, 64, 512, 2, 128
MB_CONV = 4
MB_CONV_DIM = 1024
MB_COLS = 1544
MB_PAD = 1664
N_IN = GLA_COLS + RW_COLS + MB_COLS

LANE = 128
VMEM_LIMIT = 56 * 1024 * 1024

FFN_TM = 512
FFN_FC = 256
FFN_LN_ROWS = 32
PROJ_TM = 512
GLA_TB, GLA_C = 512, 64
RW_TB, RW_C = 256, 64
MB_TB, MB_L = 256, 128


def _mm(a, b):
    return jnp.dot(a.astype(BF16), b.astype(BF16), preferred_element_type=F32)


def _mm_nt(a, b):
    return lax.dot_general(a.astype(BF16), b.astype(BF16), (((1,), (1,)), ((), ())),
                           preferred_element_type=F32)


def _mm_tn(a, b):
    return lax.dot_general(a.astype(BF16), b.astype(BF16), (((0,), (0,)), ((), ())),
                           preferred_element_type=F32)


def _split(x, parts):
    out = []
    for _ in range(parts - 1):
        piece = x.astype(BF16)
        out.append(piece)
        x = x - piece.astype(F32)
    out.append(x.astype(BF16))
    return out


def _mm_sel_l(x, sel, parts):
    acc = None
    for piece in _split(x, parts):
        term = jnp.dot(piece, sel, preferred_element_type=F32)
        acc = term if acc is None else acc + term
    return acc


def _mm_sel_r(sel, x, parts):
    acc = None
    for piece in _split(x, parts):
        term = jnp.dot(sel, piece, preferred_element_type=F32)
        acc = term if acc is None else acc + term
    return acc


def _layer_norm(y, g, b):
    mu = jnp.mean(y, axis=-1, keepdims=True)
    d = y - mu
    var = jnp.mean(d * d, axis=-1, keepdims=True)
    return d * lax.rsqrt(var + LN_EPS) * g + b


def _silu(x):
    return x * jax.nn.sigmoid(x)


def _softplus(x):
    return jnp.maximum(x, 0.0) + jnp.log(1.0 + jnp.exp(-jnp.abs(x)))


def _ordering_zero(v, zero_ref, width):
    z = pltpu.bitcast(pltpu.bitcast(v, jnp.uint32) & zero_ref[...], F32)
    return jnp.concatenate([z] * (width // LANE), axis=1)


def _iota(shape, dim):
    return lax.broadcasted_iota(jnp.int32, shape, dim)


def _const_spec(shape):
    n = len(shape)
    return pl.BlockSpec(shape, lambda *_: (0,) * n, pipeline_mode=pl.Buffered(1))


def _ffn_ln_kernel(x_ref, xp_ref, win_ref, wd_ref, g_ref, b_ref, zero_ref, o_ref, acc_ref, accp_ref, *,
                   n_chunks, n_tiles):
    step = pl.program_id(0)

    @pl.when(step == 0)
    def _():
        accp_ref[...] = jnp.zeros_like(accp_ref)

    def finish_previous():
        token = None
        for r in range(0, FFN_TM, FFN_LN_ROWS):
            rows = pl.ds(r, FFN_LN_ROWS)
            out = _layer_norm(DN_ALPHA * xp_ref[rows, :] + accp_ref[rows, :], g_ref[...], b_ref[...])
            o_ref[rows, :] = out
            tiles = [out[i:i + 8, j:j + LANE] for i in range(0, FFN_LN_ROWS, 8) for j in range(0, D_MODEL, LANE)]
            if token is not None:
                tiles.append(token)
            token = functools.reduce(lambda a, b: a + b, tiles)
            yield token

    @pl.when(step < n_tiles)
    def _():
        previous = finish_previous()
        per_chunk = -(-(FFN_TM // FFN_LN_ROWS) // (n_chunks - 1))
        xb = x_ref[...].astype(BF16)
        pin = None
        for c in range(n_chunks):
            gate = jnp.dot(xb, win_ref[:, c * FFN_FC:(c + 1) * FFN_FC], preferred_element_type=F32)
            up = jnp.dot(xb, win_ref[:, D_FF + c * FFN_FC:D_FF + (c + 1) * FFN_FC], preferred_element_type=F32)
            act = _silu(gate) * up
            if pin is not None:
                act = jnp.concatenate([act[0:8, :] + _ordering_zero(pin, zero_ref, FFN_FC), act[8:, :]], axis=0)
            part = jnp.dot(act.astype(BF16), wd_ref[c], preferred_element_type=F32)
            if c == 0:
                acc_ref[...] = part
            elif c < n_chunks - 1:
                acc_ref[...] += part
            else:
                for pin in previous:
                    pass
                accp_ref[...] = acc_ref[...] + part + _ordering_zero(pin, zero_ref, D_MODEL)[0:1, :]
            for _ in range(per_chunk):
                pin = next(previous, pin)

    @pl.when(step == n_tiles)
    def _():
        for _ in finish_previous():
            pass


def _ffn_ln(x2, w_in, w_down, g, b):
    n = x2.shape[0]
    nc = D_FF // FFN_FC
    nt = n // FFN_TM
    wd = (0.5 * w_down).astype(BF16).reshape(nc, FFN_FC, D_MODEL)
    cur = pl.BlockSpec((FFN_TM, D_MODEL), lambda i: (jnp.minimum(i, nt - 1), 0))
    done = pl.BlockSpec((FFN_TM, D_MODEL), lambda i: (jnp.maximum(i - 1, 0), 0))
    return pl.pallas_call(
        functools.partial(_ffn_ln_kernel, n_chunks=nc, n_tiles=nt),
        out_shape=jax.ShapeDtypeStruct((n, D_MODEL), F32),
        grid=(nt + 1,),
        in_specs=[
            cur,
            done,
            _const_spec((D_MODEL, 2 * D_FF)),
            _const_spec((nc, FFN_FC, D_MODEL)),
            _const_spec((1, D_MODEL)),
            _const_spec((1, D_MODEL)),
            _const_spec((8, LANE)),
        ],
        out_specs=done,
        scratch_shapes=[pltpu.VMEM((FFN_TM, D_MODEL), F32), pltpu.VMEM((FFN_TM, D_MODEL), F32)],
        compiler_params=pltpu.CompilerParams(dimension_semantics=("arbitrary",),
                                             vmem_limit_bytes=VMEM_LIMIT),
        name="ffn_ln",
    )(x2, x2, w_in.astype(BF16), wd, g.reshape(1, -1), b.reshape(1, -1), jnp.zeros((8, LANE), jnp.uint32))


def _in_proj_kernel(x_ref, wa_ref, wb_ref, wc_ref, oa_ref, ob_ref, oc_ref):
    xb = x_ref[...].astype(BF16)
    oa_ref[...] = jnp.dot(xb, wa_ref[...], preferred_element_type=F32)
    ob_ref[...] = jnp.dot(xb, wb_ref[...], preferred_element_type=F32)
    oc_ref[...] = jnp.dot(xb, wc_ref[...], preferred_element_type=F32)


def _in_proj(x2, wa, wb, wc):
    n = x2.shape[0]
    widths = (wa.shape[1], wb.shape[1], wc.shape[1])
    return pl.pallas_call(
        _in_proj_kernel,
        out_shape=[jax.ShapeDtypeStruct((n, w), F32) for w in widths],
        grid=(n // PROJ_TM,),
        in_specs=[pl.BlockSpec((PROJ_TM, D_MODEL), lambda i: (i, 0))]
        + [_const_spec((D_MODEL, w)) for w in widths],
        out_specs=[pl.BlockSpec((PROJ_TM, w), lambda i: (i, 0)) for w in widths],
        compiler_params=pltpu.CompilerParams(dimension_semantics=("parallel",),
                                             vmem_limit_bytes=VMEM_LIMIT),
        name="in_proj",
    )(x2, wa, wb, wc)


def _out_proj_ln_kernel(x_ref, ma_ref, mb_ref, mc_ref, wa_ref, wb_ref, wc_ref, g_ref, b_ref, o_ref):
    acc = jnp.dot(ma_ref[...].astype(BF16), wa_ref[...], preferred_element_type=F32)
    acc += jnp.dot(mb_ref[...].astype(BF16), wb_ref[...], preferred_element_type=F32)
    acc += jnp.dot(mc_ref[...].astype(BF16), wc_ref[...], preferred_element_type=F32)
    o_ref[...] = _layer_norm(DN_ALPHA * x_ref[...] + acc, g_ref[...], b_ref[...])


def _out_proj_ln(x2, ma, mb, mc, w_out, g, b):
    n = x2.shape[0]
    wo = w_out.astype(BF16)
    wa, wb, wc = wo[:GLA_W], wo[GLA_W:GLA_W + RW_W], wo[GLA_W + RW_W:]
    tm = PROJ_TM
    row = lambda w: pl.BlockSpec((tm, w), lambda i: (i, 0))
    return pl.pallas_call(
        _out_proj_ln_kernel,
        out_shape=jax.ShapeDtypeStruct((n, D_MODEL), F32),
        grid=(n // tm,),
        in_specs=[row(D_MODEL), row(GLA_W), row(RW_W), row(MB_DI),
                  _const_spec((GLA_W, D_MODEL)), _const_spec((RW_W, D_MODEL)),
                  _const_spec((MB_DI, D_MODEL)), _const_spec((1, D_MODEL)), _const_spec((1, D_MODEL))],
        out_specs=row(D_MODEL),
        compiler_params=pltpu.CompilerParams(dimension_semantics=("parallel",),
                                             vmem_limit_bytes=VMEM_LIMIT),
        name="out_proj_ln",
    )(x2, ma, mb, mc, wa, wb, wc, g.reshape(1, -1), b.reshape(1, -1))


def _gla_kernel(p_ref, gkup_ref, gkb_ref, ng_ref, o_ref, st_ref, *, bsz):
    C = GLA_C

    @pl.when(pl.program_id(0) == 0)
    def _():
        st_ref[...] = jnp.zeros_like(st_ref)

    TB = GLA_TB
    tril_blk = (((_iota((TB, TB), 0) // C) == (_iota((TB, TB), 1) // C))
                & (_iota((TB, TB), 0) >= _iota((TB, TB), 1))).astype(BF16)
    tril4 = (_iota((GLA_H * C, C), 0) % C) >= _iota((GLA_H * C, C), 1)
    qhead = _iota((1, GLA_QK), 1) // GLA_DK
    vhead = _iota((1, GLA_W), 1) // GLA_DV
    smask = (_iota((GLA_W, GLA_QK), 0) // GLA_DV) == (_iota((GLA_W, GLA_QK), 1) // GLA_DK)
    gmat = ((_iota((GLA_W, GLA_W), 0) // GLA_DV) == (_iota((GLA_W, GLA_W), 1) // GLA_DV)).astype(BF16) / GLA_DV

    eb = lambda f: [f(bi) for bi in range(bsz)]
    z = eb(lambda bi: _mm(p_ref[bi, :, 768:896], gkup_ref[...]) + gkb_ref[...])
    gk = eb(lambda bi: -_softplus(-z[bi]) / GLA_GATE_NORM)
    b_all = eb(lambda bi: _mm_sel_r(tril_blk, gk[bi], 3))
    qe_all = eb(lambda bi: p_ref[bi, :, 0:128] * (GLA_DK ** -0.5) * jnp.exp(b_all[bi]))
    ke_all = eb(lambda bi: p_ref[bi, :, 128:256] * jnp.exp(-b_all[bi]))

    n = TB // C
    chains = [(bi, c) for c in range(n) for bi in range(bsz)]
    ew = lambda f: [f(bi, slice(c * C, (c + 1) * C)) for bi, c in chains]
    b_last = ew(lambda bi, s: b_all[bi][s.stop - 1:s.stop, :])
    v = ew(lambda bi, s: p_ref[bi, s, 256:512])
    q_e = ew(lambda bi, s: qe_all[bi][s])
    k_e = ew(lambda bi, s: ke_all[bi][s])
    kdec = ew(lambda bi, s: p_ref[bi, s, 128:256] * jnp.exp(b_all[bi][s.stop - 1:s.stop, :] - b_all[bi][s]))
    ei = lambda f: [f(i) for i in range(len(chains))]
    q_st = ei(lambda i: jnp.concatenate([jnp.where(qhead == h, q_e[i], 0.0) for h in range(GLA_H)], axis=0))
    att = ei(lambda i: jnp.where(tril4, _mm_nt(q_st[i], k_e[i]), 0.0))
    r = ei(lambda i: _mm(att[i], v[i]))
    upd = ei(lambda i: jnp.where(smask, _mm_tn(v[i], kdec[i]), 0.0))

    st = [st_ref[bi] for bi in range(bsz)]
    outs = [[] for _ in range(bsz)]
    for i, (bi, c) in enumerate(chains):
        o = _mm_nt(q_e[i], st[bi])
        for h in range(GLA_H):
            o = o + jnp.where(vhead == h, r[i][h * C:(h + 1) * C], 0.0)
        outs[bi].append(o)
        st[bi] = st[bi] * jnp.exp(b_last[i]) + upd[i]
    for bi in range(bsz):
        st_ref[bi] = st[bi]
        o = jnp.concatenate(outs[bi], axis=0)
        ms = _mm_sel_l(o * o, gmat, 2)
        o = o * lax.rsqrt(ms + LN_EPS) * ng_ref[...]
        o_ref[bi] = o * _silu(p_ref[bi, :, 512:768])


def _gla(p, gk_up, gk_bias, norm_g):
    bsz, T, _ = p.shape
    gkup = jnp.zeros((LANE, GLA_QK), F32).at[:GLA_GATE_RANK].set(gk_up)
    return pl.pallas_call(
        functools.partial(_gla_kernel, bsz=bsz),
        out_shape=jax.ShapeDtypeStruct((bsz, T, GLA_W), F32),
        grid=(T // GLA_TB,),
        in_specs=[pl.BlockSpec((bsz, GLA_TB, GLA_PAD), lambda t: (0, t, 0)),
                  _const_spec((LANE, GLA_QK)), _const_spec((1, GLA_QK)), _const_spec((1, GLA_W))],
        out_specs=pl.BlockSpec((bsz, GLA_TB, GLA_W), lambda t: (0, t, 0)),
        scratch_shapes=[pltpu.VMEM((bsz, GLA_W, GLA_QK), F32)],
        compiler_params=pltpu.CompilerParams(dimension_semantics=("arbitrary",),
                                             vmem_limit_bytes=VMEM_LIMIT),
        name="gla_mixer",
    )(p, gkup, gk_bias.reshape(1, -1), jnp.tile(norm_g, GLA_H).reshape(1, -1))


def _rwkv_kernel(*refs, has_vres, bsz):
    if has_vres:
        (p_ref, vf_ref, mu_ref, w0_ref, wup_ref, a0_ref, aup_ref, gup_ref, kk_ref, ka_ref, rk_ref,
         gng_ref, gnb_ref, v0_ref, vup_ref, zero_ref, o_ref,
         ht_ref, prev_ref, r_s, k_s, v_s, g_s, wr_s, kb_s, arb_s, u0_s, y0_s, dec_s) = refs
    else:
        (p_ref, mu_ref, w0_ref, wup_ref, a0_ref, aup_ref, gup_ref, kk_ref, ka_ref, rk_ref,
         gng_ref, gnb_ref, zero_ref, o_ref, vf_out_ref,
         ht_ref, prev_ref, r_s, k_s, v_s, g_s, wr_s, kb_s, arb_s, u0_s, y0_s, dec_s) = refs
    C, TB, W, H = RW_C, RW_TB, RW_W, RW_H
    carried = (ht_ref, prev_ref, r_s, k_s, v_s, g_s, wr_s, kb_s, arb_s, u0_s, y0_s, dec_s)

    @pl.when(pl.program_id(0) == 0)
    def _():
        for ref in carried:
            ref[...] = jnp.zeros_like(ref)

    head_eq = (_iota((W, W), 0) // RW_DH) == (_iota((W, W), 1) // RW_DH)
    gsum = head_eq.astype(BF16)
    gmean = (head_eq.astype(F32) / RW_DH).astype(BF16)
    tril_blk = (((_iota((TB, TB), 0) // C) == (_iota((TB, TB), 1) // C))
                & (_iota((TB, TB), 0) >= _iota((TB, TB), 1))).astype(BF16)
    ri = _iota((C, W), 0)
    li = _iota((C, W), 1) % C
    strict = ri > li
    incl = ri >= li
    eye = (ri == li).astype(F32)
    blk_eq = (_iota((H * C, W), 0) // C) == (_iota((H * C, W), 1) // RW_DH)

    def bdiag(m):
        return jnp.where(blk_eq, jnp.concatenate([m] * H, axis=0), 0.0).astype(BF16)

    chains = [(bi, c) for c in range(TB // C) for bi in range(bsz)]
    ew = lambda f: [f(i) for i in range(len(chains))]
    sl = [(bi, slice(c * C, (c + 1) * C)) for bi, c in chains]

    def finish_previous():
        ht = [ht_ref[bi] for bi in range(bsz)]
        ys = [[] for _ in range(bsz)]
        for i, (bi, c) in enumerate(chains):
            sh = _mm_nt(wr_s[i], ht[bi])
            u = sh[0:C] + u0_s[i]
            ys[bi].append(sh[C:2 * C] + y0_s[i] + _mm(arb_s[i], bdiag(u)))
            upd = _mm_tn(jnp.concatenate([v_s[sl[i]], u], axis=0), kb_s[i])
            ht[bi] = ht[bi] * dec_s[i] + jnp.where(head_eq, upd, 0.0)
            yield ht[bi][0:8, 0:LANE]
        for bi in range(bsz):
            ht_ref[bi] = ht[bi]
            y = jnp.concatenate(ys[bi], axis=0)
            mean = _mm_sel_l(y, gmean, 2)
            d = y - mean
            var = _mm_sel_l(d * d, gmean, 2)
            y = d * lax.rsqrt(var + RW_GN_EPS) * gng_ref[...] + gnb_ref[...]
            bonus = _mm_sel_l(r_s[bi] * k_s[bi] * rk_ref[...], gsum, 2) * v_s[bi]
            out = (y + bonus) * g_s[bi]
            o_ref[bi] = out
            tiles = [out[i:i + 8, j:j + LANE] for i in range(0, TB, 8) for j in range(0, W, LANE)]
            yield functools.reduce(lambda a, b: a + b, tiles)

    previous = finish_previous()

    tokens = []
    sites = [0]

    def tie(x, lag=2):
        token = next(previous, None)
        if token is not None:
            tokens.append(token)
        at_site = sites[0] - lag
        sites[0] += 1
        if not 0 <= at_site < len(tokens):
            return x
        top = x[0:8, 0:LANE] + _ordering_zero(tokens[at_site], zero_ref, LANE)
        if x.shape[1] > LANE:
            top = jnp.concatenate([top, x[0:8, LANE:]], axis=1)
        return jnp.concatenate([top, x[8:]], axis=0)

    rv, kv, vv, gv, av, bv, lwv, csv = [], [], [], [], [], [], [], []
    prev_new = []
    for bi in range(bsz):
        p = p_ref[bi]
        shifted = jnp.where(_iota(p.shape, 0) == 0, prev_ref[bi], pltpu.roll(p, 1, 0))
        prev_new.append(p[TB - 1:TB, :])
        xs = p + (shifted - p) * mu_ref[...]
        r = xs[:, 0:256]
        k = xs[:, 256:512]
        v = xs[:, 512:768]
        w_low = xs[:, 768:896]
        a_low = xs[:, 896:1024]
        g_low = xs[:, 1024:1152]
        if has_vres:
            v_low = xs[:, 1152:1280]
            v = v + (vf_ref[bi] - v) * jax.nn.sigmoid(v0_ref[...] + _mm(v_low, vup_ref[...]))
        else:
            vf_out_ref[bi] = v
        w = -_softplus(-(w0_ref[...] + _mm(jnp.tanh(w_low), wup_ref[...]))) - 0.5
        lw = tie(-jnp.exp(w))
        lwv.append(lw)
        csv.append(_mm_sel_r(tril_blk, lw, 3))
        a = jax.nn.sigmoid(a0_ref[...] + _mm(a_low, aup_ref[...]))
        gv.append(_mm(jax.nn.sigmoid(g_low), gup_ref[...]))
        kk = k * kk_ref[...]
        kk = tie(kk / jnp.maximum(jnp.sqrt(_mm_sel_l(kk * kk, gsum, 2)), 1e-12))
        rv.append(r)
        kv.append(tie(k * (1.0 + (a - 1.0) * ka_ref[...])))
        vv.append(v)
        av.append(-kk)
        bv.append(kk * a)

    cut = lambda vals: ew(lambda i: vals[sl[i][0]][sl[i][1]])
    cs = cut(csv)
    kc = cut(kv)
    bc = cut(bv)
    vc = cut(vv)
    ac = cut(av)
    lwc = cut(lwv)
    rc = cut(rv)
    e_neg = ew(lambda i: jnp.exp(-cs[i]))
    e_neg[0] = tie(e_neg[0])
    at = ew(lambda i: ac[i] * jnp.exp(cs[i] - lwc[i]))
    rt = ew(lambda i: (rc[i] * jnp.exp(cs[i])).astype(BF16))
    ar = ew(lambda i: jnp.concatenate([at[i].astype(BF16), rt[i]], axis=0))
    pk = ew(lambda i: _mm_nt(ar[i], bdiag(kc[i] * e_neg[i])))
    pb = ew(lambda i: _mm_nt(ar[i], bdiag(bc[i] * e_neg[i])))
    a_ak = ew(lambda i: jnp.where(strict, pk[i][0:C], 0.0))
    a_rk = ew(lambda i: jnp.where(incl, pk[i][C:2 * C], 0.0))
    a_ab = ew(lambda i: jnp.where(strict, pb[i][0:C], 0.0))
    a_ab[0] = tie(a_ab[0])
    a_rb = ew(lambda i: jnp.where(incl, pb[i][C:2 * C], 0.0).astype(BF16))
    t = ew(lambda i: eye + a_ab[i])
    pw = a_ab
    pw_bd = ew(lambda i: bdiag(pw[i]))
    for _ in range(5):
        pw = ew(lambda i: _mm(pw[i], pw_bd[i]))
        pw[0] = tie(pw[0])
        pw_bd = ew(lambda i: bdiag(pw[i]))
        t = ew(lambda i: t[i] + _mm(t[i], pw_bd[i]))
    for _ in previous:
        pass
    v_bd = ew(lambda i: bdiag(vc[i]))
    wt = ew(lambda i: _mm(t[i], bdiag(at[i])).astype(BF16))
    u0 = ew(lambda i: _mm(t[i], bdiag(_mm(a_ak[i], v_bd[i]))))
    y0 = ew(lambda i: _mm(a_rk[i], v_bd[i]))
    wr = ew(lambda i: jnp.concatenate([wt[i], rt[i]], axis=0))
    e_rem = ew(lambda i: jnp.exp(cs[i][C - 1:C, :] - cs[i]))
    kb = ew(lambda i: jnp.concatenate([kc[i] * e_rem[i], bc[i] * e_rem[i]], axis=0).astype(BF16))

    zrow = _ordering_zero(tokens[-1] + tokens[-2], zero_ref, W)[0:1, :]
    zrow_b = zrow.astype(BF16)
    for bi in range(bsz):
        prev_ref[bi] = prev_new[bi]
        r_s[bi] = rv[bi] + zrow
        k_s[bi] = kv[bi] + zrow
        v_s[bi] = vv[bi] + zrow
        g_s[bi] = gv[bi] + zrow
    for i in range(len(chains)):
        wr_s[i] = wr[i] + zrow_b
        kb_s[i] = kb[i] + zrow_b
        arb_s[i] = a_rb[i] + zrow_b
        u0_s[i] = u0[i] + zrow
        y0_s[i] = y0[i] + zrow
        dec_s[i] = jnp.exp(cs[i][C - 1:C, :]) + zrow


def _rwkv(p, v_first, mu, w0, w_up, a0, a_up, g_up, k_k, k_a, r_k, gn_g, gn_b, v0, v_up):
    bsz, T, width = p.shape
    has_vres = v_first is not None
    row = lambda x: x.reshape(1, -1)
    pad_rows = lambda m: jnp.zeros((LANE, m.shape[1]), F32).at[:m.shape[0]].set(m)
    nt = T // RW_TB
    nch = bsz * (RW_TB // RW_C)
    blk = lambda w: pl.BlockSpec((bsz, RW_TB, w), lambda t: (0, jnp.minimum(t, nt - 1), 0))
    done = lambda w: pl.BlockSpec((bsz, RW_TB, w), lambda t: (0, jnp.maximum(t - 1, 0), 0))
    vec = _const_spec((1, RW_W))
    args = [p]
    specs = [blk(width)]
    if has_vres:
        args.append(v_first)
        specs.append(blk(RW_W))
    args += [row(mu), row(w0), pad_rows(w_up), row(a0), pad_rows(a_up), g_up, row(k_k), row(k_a),
             row(r_k), row(gn_g), row(gn_b)]
    specs += [_const_spec((1, width)), vec, _const_spec((LANE, RW_W)), vec, _const_spec((LANE, RW_W)),
              _const_spec((LANE, RW_W)), vec, vec, vec, vec, vec]
    if has_vres:
        args += [row(v0), pad_rows(v_up)]
        specs += [vec, _const_spec((LANE, RW_W))]
    args.append(jnp.zeros((8, LANE), jnp.uint32))
    specs.append(_const_spec((8, LANE)))
    if has_vres:
        out_shape = jax.ShapeDtypeStruct((bsz, T, RW_W), F32)
        out_specs = done(RW_W)
    else:
        out_shape = [jax.ShapeDtypeStruct((bsz, T, RW_W), F32),
                     jax.ShapeDtypeStruct((bsz, T + RW_TB, RW_W), F32)]
        out_specs = [done(RW_W), pl.BlockSpec((bsz, RW_TB, RW_W), lambda t: (0, t, 0))]
    seq = pltpu.VMEM((bsz, RW_TB, RW_W), F32)
    per_chain = lambda rows, dtype: pltpu.VMEM((nch, rows, RW_W), dtype)
    return pl.pallas_call(
        functools.partial(_rwkv_kernel, has_vres=has_vres, bsz=bsz),
        out_shape=out_shape,
        grid=(nt + 1,),
        in_specs=specs,
        out_specs=out_specs,
        scratch_shapes=[pltpu.VMEM((bsz, RW_W, RW_W), F32), pltpu.VMEM((bsz, 1, width), F32)] + [seq] * 4
        + [per_chain(2 * RW_C, BF16), per_chain(2 * RW_C, BF16), per_chain(RW_C, BF16),
           per_chain(RW_C, F32), per_chain(RW_C, F32), per_chain(1, F32)],
        compiler_params=pltpu.CompilerParams(dimension_semantics=("arbitrary",),
                                             vmem_limit_bytes=VMEM_LIMIT),
        name="rwkv7_mixer",
    )(*args)


def _mamba_kernel(p_ref, cw_ref, cb_ref, dtb_ref, alog_ref, d_ref, ng_ref, o_ref, st_ref, xpad_ref, *, bsz):
    L, TB = MB_L, MB_TB

    @pl.when(pl.program_id(0) == 0)
    def _():
        st_ref[...] = jnp.zeros_like(st_ref)
        xpad_ref[:, 0:8, :] = jnp.zeros((bsz, 8, MB_CONV_DIM), F32)

    tril = _iota((L, L), 0) >= _iota((L, L), 1)
    tril_b = tril.astype(BF16)
    expand = (_iota((LANE, MB_DI), 0) == (_iota((LANE, MB_DI), 1) // MB_P)).astype(BF16)
    lane_lo = _iota((1, LANE), 1) < MB_P
    gw = MB_DI // MB_G
    a_neg = -jnp.exp(alog_ref[...])

    xs, bms, cms, dts = [], [], [], []
    for bi in range(bsz):
        xpad_ref[bi, 8:8 + TB, :] = p_ref[bi, :, 512:1536]
        conv = cb_ref[...] + xpad_ref[bi, 8:8 + TB, :] * cw_ref[MB_CONV - 1:MB_CONV, :]
        for s in range(1, MB_CONV):
            conv = conv + xpad_ref[bi, 8 - s:8 - s + TB, :] * cw_ref[MB_CONV - 1 - s:MB_CONV - s, :]
        xpad_ref[bi, 0:8, :] = xpad_ref[bi, TB:TB + 8, :]
        xbc = _silu(conv)
        xs.append(xbc[:, 0:512])
        bms.append(xbc[:, 512:768])
        cms.append(xbc[:, 768:1024])
        dts.append(_softplus(p_ref[bi, :, 1536:1664] + dtb_ref[...]))

    chains = [(bi, c) for c in range(TB // L) for bi in range(bsz)]
    pre = []
    for bi, c in chains:
        sl = slice(c * L, (c + 1) * L)
        dtc = dts[bi][sl]
        cs = _mm_sel_r(tril_b, dtc * a_neg, 3)
        cs_t = cs.T
        cs_x = _mm_sel_l(cs, expand, 3)
        dt_x = _mm_sel_l(dtc, expand, 2)
        cs_last = cs_x[L - 1:L, :]
        xc = xs[bi][sl]
        xdt = xc * dt_x
        xdec = xdt * jnp.exp(cs_last - cs_x)
        y_parts, upd, cgs = [], [], []
        for g in range(MB_G):
            bg = bms[bi][sl, g * MB_N:(g + 1) * MB_N]
            cg = cms[bi][sl, g * MB_N:(g + 1) * MB_N]
            cbm = _mm_nt(cg, bg)
            for hp in range(2):
                pair = g * 2 + hp
                xp = xdt[:, pair * LANE:(pair + 1) * LANE]
                yh = []
                for e in range(2):
                    h = pair * 2 + e
                    diff = cs[:, h:h + 1] - cs_t[h:h + 1, :]
                    lmat = jnp.exp(jnp.where(tril, diff, -1e30))
                    yh.append(_mm(cbm * lmat, xp))
                y_parts.append(jnp.where(lane_lo, yh[0], yh[1]))
            upd.append(_mm_tn(bg, xdec[:, g * gw:(g + 1) * gw]))
            cgs.append(cg.astype(BF16))
        pre.append(dict(y=jnp.concatenate(y_parts, axis=1) + d_ref[...] * xc,
                        upd=jnp.concatenate(upd, axis=1), cg=cgs, e_in=jnp.exp(cs_x),
                        dec=jnp.exp(cs_last)))

    st = [st_ref[bi] for bi in range(bsz)]
    ys = [[] for _ in range(bsz)]
    for (bi, c), q in zip(chains, pre):
        stb = st[bi].astype(BF16)
        y_off = jnp.concatenate([_mm(q["cg"][g], stb[:, g * gw:(g + 1) * gw]) for g in range(MB_G)], axis=1)
        ys[bi].append(q["y"] + y_off * q["e_in"])
        st[bi] = st[bi] * q["dec"] + q["upd"]
    for bi in range(bsz):
        st_ref[bi] = st[bi]
        y = jnp.concatenate(ys[bi], axis=0) * _silu(p_ref[bi, :, 0:512])
        outs = []
        for g in range(MB_G):
            yg = y[:, g * gw:(g + 1) * gw]
            outs.append(yg * lax.rsqrt(jnp.mean(yg * yg, axis=-1, keepdims=True) + LN_EPS))
        o_ref[bi] = jnp.concatenate(outs, axis=1) * ng_ref[...]


def _mamba(p, conv_w, conv_b, dt_bias, a_log, d, norm_g):
    bsz, T, _ = p.shape
    pad = lambda vec: jnp.zeros((1, LANE), F32).at[0, :MB_H].set(vec)
    return pl.pallas_call(
        functools.partial(_mamba_kernel, bsz=bsz),
        out_shape=jax.ShapeDtypeStruct((bsz, T, MB_DI), F32),
        grid=(T // MB_TB,),
        in_specs=[pl.BlockSpec((bsz, MB_TB, MB_PAD), lambda t: (0, t, 0)),
                  _const_spec((MB_CONV, MB_CONV_DIM)), _const_spec((1, MB_CONV_DIM)),
                  _const_spec((1, LANE)), _const_spec((1, LANE)), _const_spec((1, MB_DI)),
                  _const_spec((1, MB_DI))],
        out_specs=pl.BlockSpec((bsz, MB_TB, MB_DI), lambda t: (0, t, 0)),
        scratch_shapes=[pltpu.VMEM((bsz, MB_N, MB_DI), F32),
                        pltpu.VMEM((bsz, MB_TB + 8, MB_CONV_DIM), F32)],
        compiler_params=pltpu.CompilerParams(dimension_semantics=("arbitrary",),
                                             vmem_limit_bytes=VMEM_LIMIT),
        name="mamba2_mixer",
    )(p, conv_w, conv_b.reshape(1, -1), pad(dt_bias), pad(a_log),
      jnp.repeat(d, MB_P).reshape(1, -1), norm_g.reshape(1, -1))


def _pad_cols(w, width):
    return jnp.pad(w, ((0, 0), (0, width - w.shape[1])))


def _proj_weights(w_in_l, w_vres_l):
    g = w_in_l[:, :GLA_COLS]
    gla = jnp.concatenate([g[:, 0:512], g[:, 528:784], _pad_cols(g[:, 512:528], LANE)], axis=1)
    r = w_in_l[:, GLA_COLS:GLA_COLS + RW_COLS]
    parts = [r[:, 0:768], _pad_cols(r[:, 768:832], LANE), _pad_cols(r[:, 832:896], LANE), r[:, 896:1024]]
    if w_vres_l is not None:
        parts.append(_pad_cols(w_vres_l, LANE))
    rw = jnp.concatenate(parts, axis=1)
    mb = _pad_cols(w_in_l[:, GLA_COLS + RW_COLS:N_IN], MB_PAD)
    return gla.astype(BF16), rw.astype(BF16), mb.astype(BF16)


def _rw_mu(mu_l, mu_vres_l):
    pad = lambda vec: jnp.pad(vec, (0, LANE - vec.shape[0]))
    parts = [mu_l[0:768], pad(mu_l[768:832]), pad(mu_l[832:896]), mu_l[896:1024]]
    if mu_vres_l is not None:
        parts.append(pad(mu_vres_l))
    return jnp.concatenate(parts)


def kernel(x, ln_g, ln_b, ffn_w_in, ffn_w_down, w_in, w_in_vres, w_out, gla_gk_up, gla_gk_bias, gla_norm_g, rw_mu, rw_mu_vres, rw_w0, rw_w_up, rw_a0, rw_a_up, rw_g_up, rw_k_k, rw_k_a, rw_r_k, rw_gn_g, rw_gn_b, rw_v0, rw_v_up, mb_conv_w, mb_conv_b, mb_dt_bias, mb_A_log, mb_D, mb_norm_g):
    bsz, T, _ = x.shape
    n = bsz * T
    x2 = x.reshape(n, D_MODEL)
    v_first = None
    for l in range(DEPTH):
        x2 = _ffn_ln(x2, ffn_w_in[l, 0], ffn_w_down[l, 0], ln_g[l, 0], ln_b[l, 0])
        vres = l > 0
        wa, wb, wc = _proj_weights(w_in[l], w_in_vres[l - 1] if vres else None)
        gla_p, rw_p, mb_p = _in_proj(x2, wa, wb, wc)
        gla_o = _gla(gla_p.reshape(bsz, T, -1), gla_gk_up[l], gla_gk_bias[l], gla_norm_g[l])
        mu = _rw_mu(rw_mu[l], rw_mu_vres[l - 1] if vres else None)
        rw_args = (rw_p.reshape(bsz, T, -1), v_first, mu, rw_w0[l], rw_w_up[l], rw_a0[l], rw_a_up[l],
                   rw_g_up[l], rw_k_k[l], rw_k_a[l], rw_r_k[l].reshape(-1), rw_gn_g[l], rw_gn_b[l])
        if vres:
            rw_o = _rwkv(*rw_args, rw_v0[l - 1], rw_v_up[l - 1])
        else:
            rw_o, v_first = _rwkv(*rw_args, None, None)
        mb_o = _mamba(mb_p.reshape(bsz, T, -1), mb_conv_w[l], mb_conv_b[l], mb_dt_bias[l], mb_A_log[l],
                      mb_D[l], mb_norm_g[l])
        x2 = _out_proj_ln(x2, gla_o.reshape(n, -1), rw_o.reshape(n, -1), mb_o.reshape(n, -1),
                          w_out[l], ln_g[l, 1], ln_b[l, 1])
        x2 = _ffn_ln(x2, ffn_w_in[l, 1], ffn_w_down[l, 1], ln_g[l, 2], ln_b[l, 2])
    return x2.reshape(bsz, T, D_MODEL)
```

```python
import functools

import jax
import jax.numpy as jnp
from jax import lax
from jax.experimental import pallas as pl
from jax.experimental.pallas import tpu as pltpu

F32 = jnp.float32
BF16 = jnp.bfloat16

D_MODEL = 1024
DEPTH = 2
D_FF = 2816
LN_EPS = 1e-5
DN_ALPHA = (2 * DEPTH) ** 0.25

GLA_H, GLA_DK, GLA_DV = 4, 32, 64
GLA_GATE_RANK = 16
GLA_GATE_NORM = 16.0
GLA_QK = GLA_H * GLA_DK
GLA_W = GLA_H * GLA_DV
GLA_COLS = 784
GLA_PAD = 896

RW_H, RW_DH = 4, 64
RW_W = 256
RW_COLS = 1024
RW_V_LORA = 32
RW_GN_EPS = 64e-5
RW_PAD0 = 1152
RW_PAD1 = 1280

MB_H, MB_P, MB_DI, MB_G, MB_N = 8, 64, 512, 2, 128
MB_CONV = 4
MB_CONV_DIM = 1024
MB_COLS = 1544
MB_PAD = 1664
N_IN = GLA_COLS + RW_COLS + MB_COLS

LANE = 128
VMEM_LIMIT = 56 * 1024 * 1024

FFN_TM = 512
FFN_FC = 256
FFN_LN_ROWS = 32
PROJ_TM = 512
GLA_TB, GLA_C = 512, 64
RW_TB, RW_C = 256, 64
MB_TB, MB_L = 256, 128


def _mm(a, b):
    return jnp.dot(a.astype(BF16), b.astype(BF16), preferred_element_type=F32)


def _mm_nt(a, b):
    return lax.dot_general(a.astype(BF16), b.astype(BF16), (((1,), (1,)), ((), ())),
                           preferred_element_type=F32)


def _mm_tn(a, b):
    return lax.dot_general(a.astype(BF16), b.astype(BF16), (((0,), (0,)), ((), ())),
                           preferred_element_type=F32)


def _split(x, parts):
    out = []
    for _ in range(parts - 1):
        piece = x.astype(BF16)
        out.append(piece)
        x = x - piece.astype(F32)
    out.append(x.astype(BF16))
    return out


def _mm_sel_l(x, sel, parts):
    acc = None
    for piece in _split(x, parts):
        term = jnp.dot(piece, sel, preferred_element_type=F32)
        acc = term if acc is None else acc + term
    return acc


def _mm_sel_r(sel, x, parts):
    acc = None
    for piece in _split(x, parts):
        term = jnp.dot(sel, piece, preferred_element_type=F32)
        acc = term if acc is None else acc + term
    return acc


def _layer_norm(y, g, b):
    mu = jnp.mean(y, axis=-1, keepdims=True)
    d = y - mu
    var = jnp.mean(d * d, axis=-1, keepdims=True)
    return d * lax.rsqrt(var + LN_EPS) * g + b


def _silu(x):
    return x * jax.nn.sigmoid(x)


def _softplus(x):
    return jnp.maximum(x, 0.0) + jnp.log(1.0 + jnp.exp(-jnp.abs(x)))


def _ordering_zero(v, zero_ref, width):
    z = pltpu.bitcast(pltpu.bitcast(v, jnp.uint32) & zero_ref[...], F32)
    return jnp.concatenate([z] * (width // LANE), axis=1)


def _iota(shape, dim):
    return lax.broadcasted_iota(jnp.int32, shape, dim)


def _const_spec(shape):
    n = len(shape)
    return pl.BlockSpec(shape, lambda *_: (0,) * n, pipeline_mode=pl.Buffered(1))


def _ffn_ln_kernel(x_ref, xp_ref, win_ref, wd_ref, g_ref, b_ref, zero_ref, o_ref, acc_ref, accp_ref, *,
                   n_chunks, n_tiles):
    step = pl.program_id(0)

    @pl.when(step == 0)
    def _():
        accp_ref[...] = jnp.zeros_like(accp_ref)

    def finish_previous():
        token = None
        for r in range(0, FFN_TM, FFN_LN_ROWS):
            rows = pl.ds(r, FFN_LN_ROWS)
            out = _layer_norm(DN_ALPHA * xp_ref[rows, :] + accp_ref[rows, :], g_ref[...], b_ref[...])
            o_ref[rows, :] = out
            tiles = [out[i:i + 8, j:j + LANE] for i in range(0, FFN_LN_ROWS, 8) for j in range(0, D_MODEL, LANE)]
            if token is not None:
                tiles.append(token)
            token = functools.reduce(lambda a, b: a + b, tiles)
            yield token

    @pl.when(step < n_tiles)
    def _():
        previous = finish_previous()
        per_chunk = -(-(FFN_TM // FFN_LN_ROWS) // (n_chunks - 1))
        xb = x_ref[...].astype(BF16)
        pin = None
        for c in range(n_chunks):
            gate = jnp.dot(xb, win_ref[:, c * FFN_FC:(c + 1) * FFN_FC], preferred_element_type=F32)
            up = jnp.dot(xb, win_ref[:, D_FF + c * FFN_FC:D_FF + (c + 1) * FFN_FC], preferred_element_type=F32)
            act = _silu(gate) * up
            if pin is not None:
                act = jnp.concatenate([act[0:8, :] + _ordering_zero(pin, zero_ref, FFN_FC), act[8:, :]], axis=0)
            part = jnp.dot(act.astype(BF16), wd_ref[c], preferred_element_type=F32)
            if c == 0:
                acc_ref[...] = part
            elif c < n_chunks - 1:
                acc_ref[...] += part
            else:
                for pin in previous:
                    pass
                accp_ref[...] = acc_ref[...] + part + _ordering_zero(pin, zero_ref, D_MODEL)[0:1, :]
            for _ in range(per_chunk):
                pin = next(previous, pin)

    @pl.when(step == n_tiles)
    def _():
        for _ in finish_previous():
            pass


def _ffn_weights(ffn_w_in, ffn_w_down):
    nc = D_FF // FFN_FC
    wd = (0.5 * ffn_w_down).astype(BF16).reshape(DEPTH, 2, nc, FFN_FC, D_MODEL)
    return ffn_w_in.astype(BF16), wd


def _ffn_ln(x2, w_in_all, wd_all, layer, half, g, b):
    n = x2.shape[0]
    nc = D_FF // FFN_FC
    nt = n // FFN_TM
    pick = lambda shape: pl.BlockSpec((None, None) + shape, lambda i: (layer, half) + (0,) * len(shape),
                                      pipeline_mode=pl.Buffered(1))
    cur = pl.BlockSpec((FFN_TM, D_MODEL), lambda i: (jnp.minimum(i, nt - 1), 0))
    done = pl.BlockSpec((FFN_TM, D_MODEL), lambda i: (jnp.maximum(i - 1, 0), 0))
    return pl.pallas_call(
        functools.partial(_ffn_ln_kernel, n_chunks=nc, n_tiles=nt),
        out_shape=jax.ShapeDtypeStruct((n, D_MODEL), F32),
        grid=(nt + 1,),
        in_specs=[
            cur,
            done,
            pick((D_MODEL, 2 * D_FF)),
            pick((nc, FFN_FC, D_MODEL)),
            _const_spec((1, D_MODEL)),
            _const_spec((1, D_MODEL)),
            _const_spec((8, LANE)),
        ],
        out_specs=done,
        scratch_shapes=[pltpu.VMEM((FFN_TM, D_MODEL), F32), pltpu.VMEM((FFN_TM, D_MODEL), F32)],
        compiler_params=pltpu.CompilerParams(dimension_semantics=("arbitrary",),
                                             vmem_limit_bytes=VMEM_LIMIT),
        name="ffn_ln",
    )(x2, x2, w_in_all, wd_all, g.reshape(1, -1), b.reshape(1, -1), jnp.zeros((8, LANE), jnp.uint32))


def _in_proj_kernel(x_ref, wa_ref, wb_ref, wc_ref, oa_ref, ob_ref, oc_ref):
    xb = x_ref[...].astype(BF16)
    oa_ref[...] = jnp.dot(xb, wa_ref[...], preferred_element_type=F32)
    ob_ref[...] = jnp.dot(xb, wb_ref[...], preferred_element_type=F32)
    oc_ref[...] = jnp.dot(xb, wc_ref[...], preferred_element_type=F32)


def _in_proj(x2, wa, wb, wc):
    n = x2.shape[0]
    widths = (wa.shape[1], wb.shape[1], wc.shape[1])
    return pl.pallas_call(
        _in_proj_kernel,
        out_shape=[jax.ShapeDtypeStruct((n, w), F32) for w in widths],
        grid=(n // PROJ_TM,),
        in_specs=[pl.BlockSpec((PROJ_TM, D_MODEL), lambda i: (i, 0))]
        + [_const_spec((D_MODEL, w)) for w in widths],
        out_specs=[pl.BlockSpec((PROJ_TM, w), lambda i: (i, 0)) for w in widths],
        compiler_params=pltpu.CompilerParams(dimension_semantics=("parallel",),
                                             vmem_limit_bytes=VMEM_LIMIT),
        name="in_proj",
    )(x2, wa, wb, wc)


def _out_proj_ln_kernel(x_ref, ma_ref, mb_ref, mc_ref, wa_ref, wb_ref, wc_ref, g_ref, b_ref, o_ref):
    acc = jnp.dot(ma_ref[...], wa_ref[...], preferred_element_type=F32)
    acc += jnp.dot(mb_ref[...], wb_ref[...], preferred_element_type=F32)
    acc += jnp.dot(mc_ref[...], wc_ref[...], preferred_element_type=F32)
    o_ref[...] = _layer_norm(DN_ALPHA * x_ref[...] + acc, g_ref[...], b_ref[...])


def _out_proj_ln(x2, ma, mb, mc, w_out, g, b):
    n = x2.shape[0]
    wo = w_out.astype(BF16)
    wa, wb, wc = wo[:GLA_W], wo[GLA_W:GLA_W + RW_W], wo[GLA_W + RW_W:]
    tm = PROJ_TM
    row = lambda w: pl.BlockSpec((tm, w), lambda i: (i, 0))
    return pl.pallas_call(
        _out_proj_ln_kernel,
        out_shape=jax.ShapeDtypeStruct((n, D_MODEL), F32),
        grid=(n // tm,),
        in_specs=[row(D_MODEL), row(GLA_W), row(RW_W), row(MB_DI),
                  _const_spec((GLA_W, D_MODEL)), _const_spec((RW_W, D_MODEL)),
                  _const_spec((MB_DI, D_MODEL)), _const_spec((1, D_MODEL)), _const_spec((1, D_MODEL))],
        out_specs=row(D_MODEL),
        compiler_params=pltpu.CompilerParams(dimension_semantics=("parallel",),
                                             vmem_limit_bytes=VMEM_LIMIT),
        name="out_proj_ln",
    )(x2, ma, mb, mc, wa, wb, wc, g.reshape(1, -1), b.reshape(1, -1))


def _gla_kernel(p_ref, gkup_ref, gkb_ref, ng_ref, o_ref, st_ref, *, bsz):
    C = GLA_C

    @pl.when(pl.program_id(0) == 0)
    def _():
        st_ref[...] = jnp.zeros_like(st_ref)

    TB = GLA_TB
    tril_blk = (((_iota((TB, TB), 0) // C) == (_iota((TB, TB), 1) // C))
                & (_iota((TB, TB), 0) >= _iota((TB, TB), 1))).astype(BF16)
    tril4 = (_iota((GLA_H * C, C), 0) % C) >= _iota((GLA_H * C, C), 1)
    qhead = _iota((1, GLA_QK), 1) // GLA_DK
    vhead = _iota((1, GLA_W), 1) // GLA_DV
    smask = (_iota((GLA_W, GLA_QK), 0) // GLA_DV) == (_iota((GLA_W, GLA_QK), 1) // GLA_DK)
    gmat = ((_iota((GLA_W, GLA_W), 0) // GLA_DV) == (_iota((GLA_W, GLA_W), 1) // GLA_DV)).astype(BF16) / GLA_DV

    eb = lambda f: [f(bi) for bi in range(bsz)]
    z = eb(lambda bi: _mm(p_ref[bi, :, 768:896], gkup_ref[...]) + gkb_ref[...])
    gk = eb(lambda bi: -_softplus(-z[bi]) / GLA_GATE_NORM)
    b_all = eb(lambda bi: _mm_sel_r(tril_blk, gk[bi], 3))
    qe_all = eb(lambda bi: p_ref[bi, :, 0:128] * (GLA_DK ** -0.5) * jnp.exp(b_all[bi]))
    ke_all = eb(lambda bi: p_ref[bi, :, 128:256] * jnp.exp(-b_all[bi]))

    n = TB // C
    chains = [(bi, c) for c in range(n) for bi in range(bsz)]
    ew = lambda f: [f(bi, slice(c * C, (c + 1) * C)) for bi, c in chains]
    b_last = ew(lambda bi, s: b_all[bi][s.stop - 1:s.stop, :])
    v = ew(lambda bi, s: p_ref[bi, s, 256:512])
    q_e = ew(lambda bi, s: qe_all[bi][s])
    k_e = ew(lambda bi, s: ke_all[bi][s])
    kdec = ew(lambda bi, s: p_ref[bi, s, 128:256] * jnp.exp(b_all[bi][s.stop - 1:s.stop, :] - b_all[bi][s]))
    ei = lambda f: [f(i) for i in range(len(chains))]
    q_st = ei(lambda i: jnp.concatenate([jnp.where(qhead == h, q_e[i], 0.0) for h in range(GLA_H)], axis=0))
    att = ei(lambda i: jnp.where(tril4, _mm_nt(q_st[i], k_e[i]), 0.0))
    r = ei(lambda i: _mm(att[i], v[i]))
    upd = ei(lambda i: jnp.where(smask, _mm_tn(v[i], kdec[i]), 0.0))

    st = [st_ref[bi] for bi in range(bsz)]
    outs = [[] for _ in range(bsz)]
    for i, (bi, c) in enumerate(chains):
        o = _mm_nt(q_e[i], st[bi])
        for h in range(GLA_H):
            o = o + jnp.where(vhead == h, r[i][h * C:(h + 1) * C], 0.0)
        outs[bi].append(o)
        st[bi] = st[bi] * jnp.exp(b_last[i]) + upd[i]
    for bi in range(bsz):
        st_ref[bi] = st[bi]
        o = jnp.concatenate(outs[bi], axis=0)
        ms = _mm_sel_l(o * o, gmat, 2)
        o = o * lax.rsqrt(ms + LN_EPS) * ng_ref[...]
        o_ref[bi] = (o * _silu(p_ref[bi, :, 512:768])).astype(BF16)


def _gla(p, gk_up, gk_bias, norm_g):
    bsz, T, _ = p.shape
    gkup = jnp.zeros((LANE, GLA_QK), F32).at[:GLA_GATE_RANK].set(gk_up)
    return pl.pallas_call(
        functools.partial(_gla_kernel, bsz=bsz),
        out_shape=jax.ShapeDtypeStruct((bsz, T, GLA_W), BF16),
        grid=(T // GLA_TB,),
        in_specs=[pl.BlockSpec((bsz, GLA_TB, GLA_PAD), lambda t: (0, t, 0)),
                  _const_spec((LANE, GLA_QK)), _const_spec((1, GLA_QK)), _const_spec((1, GLA_W))],
        out_specs=pl.BlockSpec((bsz, GLA_TB, GLA_W), lambda t: (0, t, 0)),
        scratch_shapes=[pltpu.VMEM((bsz, GLA_W, GLA_QK), F32)],
        compiler_params=pltpu.CompilerParams(dimension_semantics=("arbitrary",),
                                             vmem_limit_bytes=VMEM_LIMIT),
        name="gla_mixer",
    )(p, gkup, gk_bias.reshape(1, -1), jnp.tile(norm_g, GLA_H).reshape(1, -1))


def _rwkv_kernel(*refs, has_vres, bsz):
    if has_vres:
        (p_ref, vf_ref, mu_ref, w0_ref, wup_ref, a0_ref, aup_ref, gup_ref, kk_ref, ka_ref, rk_ref,
         gng_ref, gnb_ref, v0_ref, vup_ref, zero_ref, o_ref,
         ht_ref, prev_ref, r_s, k_s, v_s, g_s, wr_s, kb_s, arb_s, u0_s, y0_s, dec_s) = refs
    else:
        (p_ref, mu_ref, w0_ref, wup_ref, a0_ref, aup_ref, gup_ref, kk_ref, ka_ref, rk_ref,
         gng_ref, gnb_ref, zero_ref, o_ref, vf_out_ref,
         ht_ref, prev_ref, r_s, k_s, v_s, g_s, wr_s, kb_s, arb_s, u0_s, y0_s, dec_s) = refs
    C, TB, W, H = RW_C, RW_TB, RW_W, RW_H
    carried = (ht_ref, prev_ref, r_s, k_s, v_s, g_s, wr_s, kb_s, arb_s, u0_s, y0_s, dec_s)

    @pl.when(pl.program_id(0) == 0)
    def _():
        for ref in carried:
            ref[...] = jnp.zeros_like(ref)

    head_eq = (_iota((W, W), 0) // RW_DH) == (_iota((W, W), 1) // RW_DH)
    gsum = head_eq.astype(BF16)
    gmean = (head_eq.astype(F32) / RW_DH).astype(BF16)
    tril_blk = (((_iota((TB, TB), 0) // C) == (_iota((TB, TB), 1) // C))
                & (_iota((TB, TB), 0) >= _iota((TB, TB), 1))).astype(BF16)
    ri = _iota((C, W), 0)
    li = _iota((C, W), 1) % C
    strict = ri > li
    incl = ri >= li
    eye = (ri == li).astype(F32)
    blk_eq = (_iota((H * C, W), 0) // C) == (_iota((H * C, W), 1) // RW_DH)

    def bdiag(m):
        return jnp.where(blk_eq, jnp.concatenate([m] * H, axis=0), 0.0).astype(BF16)

    chains = [(bi, c) for c in range(TB // C) for bi in range(bsz)]
    ew = lambda f: [f(i) for i in range(len(chains))]
    sl = [(bi, slice(c * C, (c + 1) * C)) for bi, c in chains]

    def finish_previous():
        ht = [ht_ref[bi] for bi in range(bsz)]
        ys = [[] for _ in range(bsz)]
        for i, (bi, c) in enumerate(chains):
            sh = _mm_nt(wr_s[i], ht[bi])
            u = sh[0:C] + u0_s[i]
            ys[bi].append(sh[C:2 * C] + y0_s[i] + _mm(arb_s[i], bdiag(u)))
            upd = _mm_tn(jnp.concatenate([v_s[sl[i]], u], axis=0), kb_s[i])
            ht[bi] = ht[bi] * dec_s[i] + jnp.where(head_eq, upd, 0.0)
            yield ht[bi][0:8, 0:LANE]
        for bi in range(bsz):
            ht_ref[bi] = ht[bi]
            y = jnp.concatenate(ys[bi], axis=0)
            mean = _mm_sel_l(y, gmean, 2)
            d = y - mean
            var = _mm_sel_l(d * d, gmean, 2)
            y = d * lax.rsqrt(var + RW_GN_EPS) * gng_ref[...] + gnb_ref[...]
            bonus = _mm_sel_l(r_s[bi] * k_s[bi] * rk_ref[...], gsum, 2) * v_s[bi]
            out = (y + bonus) * g_s[bi]
            o_ref[bi] = out.astype(BF16)
            tiles = [out[i:i + 8, j:j + LANE] for i in range(0, TB, 8) for j in range(0, W, LANE)]
            yield functools.reduce(lambda a, b: a + b, tiles)

    previous = finish_previous()

    tokens = []
    sites = [0]

    def tie(x, lag=2):
        token = next(previous, None)
        if token is not None:
            tokens.append(token)
        at_site = sites[0] - lag
        sites[0] += 1
        if not 0 <= at_site < len(tokens):
            return x
        top = x[0:8, 0:LANE] + _ordering_zero(tokens[at_site], zero_ref, LANE)
        if x.shape[1] > LANE:
            top = jnp.concatenate([top, x[0:8, LANE:]], axis=1)
        return jnp.concatenate([top, x[8:]], axis=0)

    rv, kv, vv, gv, av, bv, lwv, csv = [], [], [], [], [], [], [], []
    prev_new = []
    for bi in range(bsz):
        p = p_ref[bi]
        shifted = jnp.where(_iota(p.shape, 0) == 0, prev_ref[bi], pltpu.roll(p, 1, 0))
        prev_new.append(p[TB - 1:TB, :])
        xs = p + (shifted - p) * mu_ref[...]
        r = xs[:, 0:256]
        k = xs[:, 256:512]
        v = xs[:, 512:768]
        w_low = xs[:, 768:896]
        a_low = xs[:, 896:1024]
        g_low = xs[:, 1024:1152]
        if has_vres:
            v_low = xs[:, 1152:1280]
            v = v + (vf_ref[bi] - v) * jax.nn.sigmoid(v0_ref[...] + _mm(v_low, vup_ref[...]))
        else:
            vf_out_ref[bi] = v
        w = -_softplus(-(w0_ref[...] + _mm(jnp.tanh(w_low), wup_ref[...]))) - 0.5
        lw = tie(-jnp.exp(w))
        lwv.append(lw)
        csv.append(_mm_sel_r(tril_blk, lw, 3))
        a = jax.nn.sigmoid(a0_ref[...] + _mm(a_low, aup_ref[...]))
        gv.append(_mm(jax.nn.sigmoid(g_low), gup_ref[...]))
        kk = k * kk_ref[...]
        kk = tie(kk / jnp.maximum(jnp.sqrt(_mm_sel_l(kk * kk, gsum, 2)), 1e-12))
        rv.append(r)
        kv.append(tie(k * (1.0 + (a - 1.0) * ka_ref[...])))
        vv.append(v)
        av.append(-kk)
        bv.append(kk * a)

    cut = lambda vals: ew(lambda i: vals[sl[i][0]][sl[i][1]])
    cs = cut(csv)
    kc = cut(kv)
    bc = cut(bv)
    vc = cut(vv)
    ac = cut(av)
    lwc = cut(lwv)
    rc = cut(rv)
    e_neg = ew(lambda i: jnp.exp(-cs[i]))
    e_neg[0] = tie(e_neg[0])
    at = ew(lambda i: ac[i] * jnp.exp(cs[i] - lwc[i]))
    rt = ew(lambda i: (rc[i] * jnp.exp(cs[i])).astype(BF16))
    ar = ew(lambda i: jnp.concatenate([at[i].astype(BF16), rt[i]], axis=0))
    pk = ew(lambda i: _mm_nt(ar[i], bdiag(kc[i] * e_neg[i])))
    pb = ew(lambda i: _mm_nt(ar[i], bdiag(bc[i] * e_neg[i])))
    a_ak = ew(lambda i: jnp.where(strict, pk[i][0:C], 0.0))
    a_rk = ew(lambda i: jnp.where(incl, pk[i][C:2 * C], 0.0))
    a_ab = ew(lambda i: jnp.where(strict, pb[i][0:C], 0.0))
    a_ab[0] = tie(a_ab[0])
    a_rb = ew(lambda i: jnp.where(incl, pb[i][C:2 * C], 0.0).astype(BF16))
    t = ew(lambda i: eye + a_ab[i])
    pw = a_ab
    pw_bd = ew(lambda i: bdiag(pw[i]))
    for _ in range(5):
        pw = ew(lambda i: _mm(pw[i], pw_bd[i]))
        pw[0] = tie(pw[0])
        pw_bd = ew(lambda i: bdiag(pw[i]))
        t = ew(lambda i: t[i] + _mm(t[i], pw_bd[i]))
    for _ in previous:
        pass
    v_bd = ew(lambda i: bdiag(vc[i]))
    wt = ew(lambda i: _mm(t[i], bdiag(at[i])).astype(BF16))
    u0 = ew(lambda i: _mm(t[i], bdiag(_mm(a_ak[i], v_bd[i]))))
    y0 = ew(lambda i: _mm(a_rk[i], v_bd[i]))
    wr = ew(lambda i: jnp.concatenate([wt[i], rt[i]], axis=0))
    e_rem = ew(lambda i: jnp.exp(cs[i][C - 1:C, :] - cs[i]))
    kb = ew(lambda i: jnp.concatenate([kc[i] * e_rem[i], bc[i] * e_rem[i]], axis=0).astype(BF16))

    zrow = _ordering_zero(tokens[-1] + tokens[-2], zero_ref, W)[0:1, :]
    zrow_b = zrow.astype(BF16)
    for bi in range(bsz):
        prev_ref[bi] = prev_new[bi]
        r_s[bi] = rv[bi] + zrow
        k_s[bi] = kv[bi] + zrow
        v_s[bi] = vv[bi] + zrow
        g_s[bi] = gv[bi] + zrow
    for i in range(len(chains)):
        wr_s[i] = wr[i] + zrow_b
        kb_s[i] = kb[i] + zrow_b
        arb_s[i] = a_rb[i] + zrow_b
        u0_s[i] = u0[i] + zrow
        y0_s[i] = y0[i] + zrow
        dec_s[i] = jnp.exp(cs[i][C - 1:C, :]) + zrow


def _rwkv(p, v_first, mu, w0, w_up, a0, a_up, g_up, k_k, k_a, r_k, gn_g, gn_b, v0, v_up):
    bsz, T, width = p.shape
    has_vres = v_first is not None
    row = lambda x: x.reshape(1, -1)
    pad_rows = lambda m: jnp.zeros((LANE, m.shape[1]), F32).at[:m.shape[0]].set(m)
    nt = T // RW_TB
    nch = bsz * (RW_TB // RW_C)
    blk = lambda w: pl.BlockSpec((bsz, RW_TB, w), lambda t: (0, jnp.minimum(t, nt - 1), 0))
    done = lambda w: pl.BlockSpec((bsz, RW_TB, w), lambda t: (0, jnp.maximum(t - 1, 0), 0))
    vec = _const_spec((1, RW_W))
    args = [p]
    specs = [blk(width)]
    if has_vres:
        args.append(v_first)
        specs.append(blk(RW_W))
    args += [row(mu), row(w0), pad_rows(w_up), row(a0), pad_rows(a_up), g_up, row(k_k), row(k_a),
             row(r_k), row(gn_g), row(gn_b)]
    specs += [_const_spec((1, width)), vec, _const_spec((LANE, RW_W)), vec, _const_spec((LANE, RW_W)),
              _const_spec((LANE, RW_W)), vec, vec, vec, vec, vec]
    if has_vres:
        args += [row(v0), pad_rows(v_up)]
        specs += [vec, _const_spec((LANE, RW_W))]
    args.append(jnp.zeros((8, LANE), jnp.uint32))
    specs.append(_const_spec((8, LANE)))
    if has_vres:
        out_shape = jax.ShapeDtypeStruct((bsz, T, RW_W), BF16)
        out_specs = done(RW_W)
    else:
        out_shape = [jax.ShapeDtypeStruct((bsz, T, RW_W), BF16),
                     jax.ShapeDtypeStruct((bsz, T + RW_TB, RW_W), F32)]
        out_specs = [done(RW_W), pl.BlockSpec((bsz, RW_TB, RW_W), lambda t: (0, t, 0))]
    seq = pltpu.VMEM((bsz, RW_TB, RW_W), F32)
    per_chain = lambda rows, dtype: pltpu.VMEM((nch, rows, RW_W), dtype)
    return pl.pallas_call(
        functools.partial(_rwkv_kernel, has_vres=has_vres, bsz=bsz),
        out_shape=out_shape,
        grid=(nt + 1,),
        in_specs=specs,
        out_specs=out_specs,
        scratch_shapes=[pltpu.VMEM((bsz, RW_W, RW_W), F32), pltpu.VMEM((bsz, 1, width), F32)] + [seq] * 4
        + [per_chain(2 * RW_C, BF16), per_chain(2 * RW_C, BF16), per_chain(RW_C, BF16),
           per_chain(RW_C, F32), per_chain(RW_C, F32), per_chain(1, F32)],
        compiler_params=pltpu.CompilerParams(dimension_semantics=("arbitrary",),
                                             vmem_limit_bytes=VMEM_LIMIT),
        name="rwkv7_mixer",
    )(*args)


def _mamba_kernel(p_ref, cw_ref, cb_ref, dtb_ref, alog_ref, d_ref, ng_ref, o_ref, st_ref, xpad_ref, *, bsz):
    L, TB = MB_L, MB_TB

    @pl.when(pl.program_id(0) == 0)
    def _():
        st_ref[...] = jnp.zeros_like(st_ref)
        xpad_ref[:, 0:8, :] = jnp.zeros((bsz, 8, MB_CONV_DIM), F32)

    tril = _iota((L, L), 0) >= _iota((L, L), 1)
    tril_b = tril.astype(BF16)
    expand = (_iota((LANE, MB_DI), 0) == (_iota((LANE, MB_DI), 1) // MB_P)).astype(BF16)
    lane_lo = _iota((1, LANE), 1) < MB_P
    gw = MB_DI // MB_G
    a_neg = -jnp.exp(alog_ref[...])

    xs, bms, cms, dts = [], [], [], []
    for bi in range(bsz):
        xpad_ref[bi, 8:8 + TB, :] = p_ref[bi, :, 512:1536]
        conv = cb_ref[...] + xpad_ref[bi, 8:8 + TB, :] * cw_ref[MB_CONV - 1:MB_CONV, :]
        for s in range(1, MB_CONV):
            conv = conv + xpad_ref[bi, 8 - s:8 - s + TB, :] * cw_ref[MB_CONV - 1 - s:MB_CONV - s, :]
        xpad_ref[bi, 0:8, :] = xpad_ref[bi, TB:TB + 8, :]
        xbc = _silu(conv)
        xs.append(xbc[:, 0:512])
        bms.append(xbc[:, 512:768])
        cms.append(xbc[:, 768:1024])
        dts.append(_softplus(p_ref[bi, :, 1536:1664] + dtb_ref[...]))

    chains = [(bi, c) for c in range(TB // L) for bi in range(bsz)]
    pre = []
    for bi, c in chains:
        sl = slice(c * L, (c + 1) * L)
        dtc = dts[bi][sl]
        cs = _mm_sel_r(tril_b, dtc * a_neg, 3)
        cs_t = cs.T
        cs_x = _mm_sel_l(cs, expand, 3)
        dt_x = _mm_sel_l(dtc, expand, 2)
        cs_last = cs_x[L - 1:L, :]
        xc = xs[bi][sl]
        xdt = xc * dt_x
        xdec = xdt * jnp.exp(cs_last - cs_x)
        y_parts, upd, cgs = [], [], []
        for g in range(MB_G):
            bg = bms[bi][sl, g * MB_N:(g + 1) * MB_N]
            cg = cms[bi][sl, g * MB_N:(g + 1) * MB_N]
            cbm = _mm_nt(cg, bg)
            for hp in range(2):
                pair = g * 2 + hp
                xp = xdt[:, pair * LANE:(pair + 1) * LANE]
                yh = []
                for e in range(2):
                    h = pair * 2 + e
                    diff = cs[:, h:h + 1] - cs_t[h:h + 1, :]
                    lmat = jnp.exp(jnp.where(tril, diff, -1e30))
                    yh.append(_mm(cbm * lmat, xp))
                y_parts.append(jnp.where(lane_lo, yh[0], yh[1]))
            upd.append(_mm_tn(bg, xdec[:, g * gw:(g + 1) * gw]))
            cgs.append(cg.astype(BF16))
        pre.append(dict(y=jnp.concatenate(y_parts, axis=1) + d_ref[...] * xc,
                        upd=jnp.concatenate(upd, axis=1), cg=cgs, e_in=jnp.exp(cs_x),
                        dec=jnp.exp(cs_last)))

    st = [st_ref[bi] for bi in range(bsz)]
    ys = [[] for _ in range(bsz)]
    for (bi, c), q in zip(chains, pre):
        stb = st[bi].astype(BF16)
        y_off = jnp.concatenate([_mm(q["cg"][g], stb[:, g * gw:(g + 1) * gw]) for g in range(MB_G)], axis=1)
        ys[bi].append(q["y"] + y_off * q["e_in"])
        st[bi] = st[bi] * q["dec"] + q["upd"]
    for bi in range(bsz):
        st_ref[bi] = st[bi]
        y = jnp.concatenate(ys[bi], axis=0) * _silu(p_ref[bi, :, 0:512])
        outs = []
        for g in range(MB_G):
            yg = y[:, g * gw:(g + 1) * gw]
            outs.append(yg * lax.rsqrt(jnp.mean(yg * yg, axis=-1, keepdims=True) + LN_EPS))
        o_ref[bi] = (jnp.concatenate(outs, axis=1) * ng_ref[...]).astype(BF16)


def _mamba(p, conv_w, conv_b, dt_bias, a_log, d, norm_g):
    bsz, T, _ = p.shape
    pad = lambda vec: jnp.zeros((1, LANE), F32).at[0, :MB_H].set(vec)
    return pl.pallas_call(
        functools.partial(_mamba_kernel, bsz=bsz),
        out_shape=jax.ShapeDtypeStruct((bsz, T, MB_DI), BF16),
        grid=(T // MB_TB,),
        in_specs=[pl.BlockSpec((bsz, MB_TB, MB_PAD), lambda t: (0, t, 0)),
                  _const_spec((MB_CONV, MB_CONV_DIM)), _const_spec((1, MB_CONV_DIM)),
                  _const_spec((1, LANE)), _const_spec((1, LANE)), _const_spec((1, MB_DI)),
                  _const_spec((1, MB_DI))],
        out_specs=pl.BlockSpec((bsz, MB_TB, MB_DI), lambda t: (0, t, 0)),
        scratch_shapes=[pltpu.VMEM((bsz, MB_N, MB_DI), F32),
                        pltpu.VMEM((bsz, MB_TB + 8, MB_CONV_DIM), F32)],
        compiler_params=pltpu.CompilerParams(dimension_semantics=("arbitrary",),
                                             vmem_limit_bytes=VMEM_LIMIT),
        name="mamba2_mixer",
    )(p, conv_w, conv_b.reshape(1, -1), pad(dt_bias), pad(a_log),
      jnp.repeat(d, MB_P).reshape(1, -1), norm_g.reshape(1, -1))


def _pad_cols(w, width):
    return jnp.pad(w, ((0, 0), (0, width - w.shape[1])))


def _proj_weights(w_in_l, w_vres_l):
    g = w_in_l[:, :GLA_COLS]
    gla = jnp.concatenate([g[:, 0:512], g[:, 528:784], _pad_cols(g[:, 512:528], LANE)], axis=1)
    r = w_in_l[:, GLA_COLS:GLA_COLS + RW_COLS]
    parts = [r[:, 0:768], _pad_cols(r[:, 768:832], LANE), _pad_cols(r[:, 832:896], LANE), r[:, 896:1024]]
    if w_vres_l is not None:
        parts.append(_pad_cols(w_vres_l, LANE))
    rw = jnp.concatenate(parts, axis=1)
    mb = _pad_cols(w_in_l[:, GLA_COLS + RW_COLS:N_IN], MB_PAD)
    return gla.astype(BF16), rw.astype(BF16), mb.astype(BF16)


def _rw_mu(mu_l, mu_vres_l):
    pad = lambda vec: jnp.pad(vec, (0, LANE - vec.shape[0]))
    parts = [mu_l[0:768], pad(mu_l[768:832]), pad(mu_l[832:896]), mu_l[896:1024]]
    if mu_vres_l is not None:
        parts.append(pad(mu_vres_l))
    return jnp.concatenate(parts)


def kernel(x, ln_g, ln_b, ffn_w_in, ffn_w_down, w_in, w_in_vres, w_out, gla_gk_up, gla_gk_bias, gla_norm_g, rw_mu, rw_mu_vres, rw_w0, rw_w_up, rw_a0, rw_a_up, rw_g_up, rw_k_k, rw_k_a, rw_r_k, rw_gn_g, rw_gn_b, rw_v0, rw_v_up, mb_conv_w, mb_conv_b, mb_dt_bias, mb_A_log, mb_D, mb_norm_g):
    bsz, T, _ = x.shape
    n = bsz * T
    x2 = x.reshape(n, D_MODEL)
    v_first = None
    ffn_in, ffn_down = _ffn_weights(ffn_w_in, ffn_w_down)
    for l in range(DEPTH):
        x2 = _ffn_ln(x2, ffn_in, ffn_down, l, 0, ln_g[l, 0], ln_b[l, 0])
        vres = l > 0
        wa, wb, wc = _proj_weights(w_in[l], w_in_vres[l - 1] if vres else None)
        gla_p, rw_p, mb_p = _in_proj(x2, wa, wb, wc)
        gla_o = _gla(gla_p.reshape(bsz, T, -1), gla_gk_up[l], gla_gk_bias[l], gla_norm_g[l])
        mu = _rw_mu(rw_mu[l], rw_mu_vres[l - 1] if vres else None)
        rw_args = (rw_p.reshape(bsz, T, -1), v_first, mu, rw_w0[l], rw_w_up[l], rw_a0[l], rw_a_up[l],
                   rw_g_up[l], rw_k_k[l], rw_k_a[l], rw_r_k[l].reshape(-1), rw_gn_g[l], rw_gn_b[l])
        if vres:
            rw_o = _rwkv(*rw_args, rw_v0[l - 1], rw_v_up[l - 1])
        else:
            rw_o, v_first = _rwkv(*rw_args, None, None)
        mb_o = _mamba(mb_p.reshape(bsz, T, -1), mb_conv_w[l], mb_conv_b[l], mb_dt_bias[l], mb_A_log[l],
                      mb_D[l], mb_norm_g[l])
        x2 = _out_proj_ln(x2, gla_o.reshape(n, -1), rw_o.reshape(n, -1), mb_o.reshape(n, -1),
                          w_out[l], ln_g[l, 1], ln_b[l, 1])
        x2 = _ffn_ln(x2, ffn_in, ffn_down, l, 1, ln_g[l, 2], ln_b[l, 2])
    return x2.reshape(bsz, T, D_MODEL)
```

```python
import functools

import jax
import jax.numpy as jnp
from jax import lax
from jax.experimental import pallas as pl
from jax.experimental.pallas import tpu as pltpu

F32 = jnp.float32
BF16 = jnp.bfloat16

D_MODEL = 1024
DEPTH = 2
D_FF = 2816
LN_EPS = 1e-5
DN_ALPHA = (2 * DEPTH) ** 0.25

GLA_H, GLA_DK, GLA_DV = 4, 32, 64
GLA_GATE_RANK = 16
GLA_GATE_NORM = 16.0
GLA_QK = GLA_H * GLA_DK
GLA_W = GLA_H * GLA_DV
GLA_COLS = 784
GLA_PAD = 896

RW_H, RW_DH = 4, 64
RW_W = 256
RW_COLS = 1024
RW_V_LORA = 32
RW_GN_EPS = 64e-5
RW_PAD0 = 1152
RW_PAD1 = 1280

MB_H, MB_P, MB_DI, MB_G, MB_N = 8, 64, 512, 2, 128
MB_CONV = 4
MB_CONV_DIM = 1024
MB_COLS = 1544
MB_PAD = 1664
N_IN = GLA_COLS + RW_COLS + MB_COLS

LANE = 128
VMEM_LIMIT = 56 * 1024 * 1024

FFN_TM = 512
FFN_FC = 256
FFN_LN_ROWS = 32
PROJ_TM = 512
OUT_TM = 1024
GLA_TB, GLA_C = 512, 64
RW_TB, RW_C = 256, 64
MB_TB, MB_L = 256, 128


def _mm(a, b):
    return jnp.dot(a.astype(BF16), b.astype(BF16), preferred_element_type=F32)


def _mm_nt(a, b):
    return lax.dot_general(a.astype(BF16), b.astype(BF16), (((1,), (1,)), ((), ())),
                           preferred_element_type=F32)


def _mm_tn(a, b):
    return lax.dot_general(a.astype(BF16), b.astype(BF16), (((0,), (0,)), ((), ())),
                           preferred_element_type=F32)


def _split(x, parts):
    out = []
    for _ in range(parts - 1):
        piece = x.astype(BF16)
        out.append(piece)
        x = x - piece.astype(F32)
    out.append(x.astype(BF16))
    return out


def _mm_sel_l(x, sel, parts):
    acc = None
    for piece in _split(x, parts):
        term = jnp.dot(piece, sel, preferred_element_type=F32)
        acc = term if acc is None else acc + term
    return acc


def _mm_sel_r(sel, x, parts):
    acc = None
    for piece in _split(x, parts):
        term = jnp.dot(sel, piece, preferred_element_type=F32)
        acc = term if acc is None else acc + term
    return acc


def _layer_norm(y, g, b):
    mu = jnp.mean(y, axis=-1, keepdims=True)
    d = y - mu
    var = jnp.mean(d * d, axis=-1, keepdims=True)
    return d * lax.rsqrt(var + LN_EPS) * g + b


def _silu(x):
    return x * jax.nn.sigmoid(x)


def _softplus(x):
    return jnp.maximum(x, 0.0) + jnp.log(1.0 + jnp.exp(-jnp.abs(x)))


def _ordering_zero(v, zero_ref, width):
    z = pltpu.bitcast(pltpu.bitcast(v, jnp.uint32) & zero_ref[...], F32)
    return jnp.concatenate([z] * (width // LANE), axis=1)


def _iota(shape, dim):
    return lax.broadcasted_iota(jnp.int32, shape, dim)


def _const_spec(shape):
    n = len(shape)
    return pl.BlockSpec(shape, lambda *_: (0,) * n, pipeline_mode=pl.Buffered(1))


def _ffn_ln_kernel(x_ref, xp_ref, win_ref, wd_ref, g_ref, b_ref, zero_ref, o_ref, acc_ref, accp_ref, *,
                   n_chunks, n_tiles):
    step = pl.program_id(0)

    @pl.when(step == 0)
    def _():
        accp_ref[...] = jnp.zeros_like(accp_ref)

    def finish_previous():
        token = None
        for r in range(0, FFN_TM, FFN_LN_ROWS):
            rows = pl.ds(r, FFN_LN_ROWS)
            out = _layer_norm(DN_ALPHA * xp_ref[rows, :] + accp_ref[rows, :], g_ref[...], b_ref[...])
            o_ref[rows, :] = out
            tiles = [out[i:i + 8, j:j + LANE] for i in range(0, FFN_LN_ROWS, 8) for j in range(0, D_MODEL, LANE)]
            if token is not None:
                tiles.append(token)
            token = functools.reduce(lambda a, b: a + b, tiles)
            yield token

    @pl.when(step < n_tiles)
    def _():
        previous = finish_previous()
        per_chunk = -(-(FFN_TM // FFN_LN_ROWS) // (n_chunks - 1))
        xb = x_ref[...].astype(BF16)
        pin = None
        for c in range(n_chunks):
            gate = jnp.dot(xb, win_ref[:, c * FFN_FC:(c + 1) * FFN_FC], preferred_element_type=F32)
            up = jnp.dot(xb, win_ref[:, D_FF + c * FFN_FC:D_FF + (c + 1) * FFN_FC], preferred_element_type=F32)
            act = _silu(gate) * up
            if pin is not None:
                act = jnp.concatenate([act[0:8, :] + _ordering_zero(pin, zero_ref, FFN_FC), act[8:, :]], axis=0)
            part = jnp.dot(act.astype(BF16), wd_ref[c], preferred_element_type=F32)
            if c == 0:
                acc_ref[...] = part
            elif c < n_chunks - 1:
                acc_ref[...] += part
            else:
                for pin in previous:
                    pass
                accp_ref[...] = acc_ref[...] + part + _ordering_zero(pin, zero_ref, D_MODEL)[0:1, :]
            for _ in range(per_chunk):
                pin = next(previous, pin)

    @pl.when(step == n_tiles)
    def _():
        for _ in finish_previous():
            pass


def _ffn_weights(ffn_w_in, ffn_w_down):
    nc = D_FF // FFN_FC
    wd = (0.5 * ffn_w_down).astype(BF16).reshape(DEPTH, 2, nc, FFN_FC, D_MODEL)
    return ffn_w_in.astype(BF16), wd


def _ffn_ln(x2, w_in_all, wd_all, layer, half, g, b):
    n = x2.shape[0]
    nc = D_FF // FFN_FC
    nt = n // FFN_TM
    pick = lambda shape: pl.BlockSpec((None, None) + shape, lambda i: (layer, half) + (0,) * len(shape),
                                      pipeline_mode=pl.Buffered(1))
    cur = pl.BlockSpec((FFN_TM, D_MODEL), lambda i: (jnp.minimum(i, nt - 1), 0))
    done = pl.BlockSpec((FFN_TM, D_MODEL), lambda i: (jnp.maximum(i - 1, 0), 0))
    return pl.pallas_call(
        functools.partial(_ffn_ln_kernel, n_chunks=nc, n_tiles=nt),
        out_shape=jax.ShapeDtypeStruct((n, D_MODEL), F32),
        grid=(nt + 1,),
        in_specs=[
            cur,
            done,
            pick((D_MODEL, 2 * D_FF)),
            pick((nc, FFN_FC, D_MODEL)),
            _const_spec((1, D_MODEL)),
            _const_spec((1, D_MODEL)),
            _const_spec((8, LANE)),
        ],
        out_specs=done,
        scratch_shapes=[pltpu.VMEM((FFN_TM, D_MODEL), F32), pltpu.VMEM((FFN_TM, D_MODEL), F32)],
        compiler_params=pltpu.CompilerParams(dimension_semantics=("arbitrary",),
                                             vmem_limit_bytes=VMEM_LIMIT),
        name="ffn_ln",
    )(x2, x2, w_in_all, wd_all, g.reshape(1, -1), b.reshape(1, -1), jnp.zeros((8, LANE), jnp.uint32))


def _in_proj_kernel(x_ref, wa_ref, wb_ref, wc_ref, oa_ref, ob_ref, oc_ref):
    xb = x_ref[...].astype(BF16)
    oa_ref[...] = jnp.dot(xb, wa_ref[...], preferred_element_type=F32)
    ob_ref[...] = jnp.dot(xb, wb_ref[...], preferred_element_type=F32)
    oc_ref[...] = jnp.dot(xb, wc_ref[...], preferred_element_type=F32)


def _in_proj(x2, wa, wb, wc):
    n = x2.shape[0]
    widths = (wa.shape[1], wb.shape[1], wc.shape[1])
    return pl.pallas_call(
        _in_proj_kernel,
        out_shape=[jax.ShapeDtypeStruct((n, w), F32) for w in widths],
        grid=(n // PROJ_TM,),
        in_specs=[pl.BlockSpec((PROJ_TM, D_MODEL), lambda i: (i, 0))]
        + [_const_spec((D_MODEL, w)) for w in widths],
        out_specs=[pl.BlockSpec((PROJ_TM, w), lambda i: (i, 0)) for w in widths],
        compiler_params=pltpu.CompilerParams(dimension_semantics=("parallel",),
                                             vmem_limit_bytes=VMEM_LIMIT),
        name="in_proj",
    )(x2, wa, wb, wc)


def _out_proj_ln_kernel(x_ref, ma_ref, mb_ref, mc_ref, wa_ref, wb_ref, wc_ref, g_ref, b_ref, o_ref):
    acc = jnp.dot(ma_ref[...], wa_ref[...], preferred_element_type=F32)
    acc += jnp.dot(mb_ref[...], wb_ref[...], preferred_element_type=F32)
    acc += jnp.dot(mc_ref[...], wc_ref[...], preferred_element_type=F32)
    o_ref[...] = acc
    for r in range(0, OUT_TM, FFN_LN_ROWS):
        rows = pl.ds(r, FFN_LN_ROWS)
        o_ref[rows, :] = _layer_norm(DN_ALPHA * x_ref[rows, :] + o_ref[rows, :], g_ref[...], b_ref[...])


def _out_proj_ln(x2, ma, mb, mc, w_out, g, b):
    n = x2.shape[0]
    wo = w_out.astype(BF16)
    wa, wb, wc = wo[:GLA_W], wo[GLA_W:GLA_W + RW_W], wo[GLA_W + RW_W:]
    tm = OUT_TM
    row = lambda w: pl.BlockSpec((tm, w), lambda i: (i, 0))
    return pl.pallas_call(
        _out_proj_ln_kernel,
        out_shape=jax.ShapeDtypeStruct((n, D_MODEL), F32),
        grid=(n // tm,),
        in_specs=[row(D_MODEL), row(GLA_W), row(RW_W), row(MB_DI),
                  _const_spec((GLA_W, D_MODEL)), _const_spec((RW_W, D_MODEL)),
                  _const_spec((MB_DI, D_MODEL)), _const_spec((1, D_MODEL)), _const_spec((1, D_MODEL))],
        out_specs=row(D_MODEL),
        compiler_params=pltpu.CompilerParams(dimension_semantics=("parallel",),
                                             vmem_limit_bytes=VMEM_LIMIT),
        name="out_proj_ln",
    )(x2, ma, mb, mc, wa, wb, wc, g.reshape(1, -1), b.reshape(1, -1))


def _gla_kernel(p_ref, gkup_ref, gkb_ref, ng_ref, o_ref, st_ref, *, bsz):
    C = GLA_C

    @pl.when(pl.program_id(0) == 0)
    def _():
        st_ref[...] = jnp.zeros_like(st_ref)

    TB = GLA_TB
    tril_blk = (((_iota((TB, TB), 0) // C) == (_iota((TB, TB), 1) // C))
                & (_iota((TB, TB), 0) >= _iota((TB, TB), 1))).astype(BF16)
    tril4 = (_iota((GLA_H * C, C), 0) % C) >= _iota((GLA_H * C, C), 1)
    qhead = _iota((1, GLA_QK), 1) // GLA_DK
    vhead = _iota((1, GLA_W), 1) // GLA_DV
    smask = (_iota((GLA_W, GLA_QK), 0) // GLA_DV) == (_iota((GLA_W, GLA_QK), 1) // GLA_DK)
    gmat = ((_iota((GLA_W, GLA_W), 0) // GLA_DV) == (_iota((GLA_W, GLA_W), 1) // GLA_DV)).astype(BF16) / GLA_DV

    eb = lambda f: [f(bi) for bi in range(bsz)]
    z = eb(lambda bi: _mm(p_ref[bi, :, 768:896], gkup_ref[...]) + gkb_ref[...])
    gk = eb(lambda bi: -_softplus(-z[bi]) / GLA_GATE_NORM)
    b_all = eb(lambda bi: _mm_sel_r(tril_blk, gk[bi], 3))
    qe_all = eb(lambda bi: p_ref[bi, :, 0:128] * (GLA_DK ** -0.5) * jnp.exp(b_all[bi]))
    ke_all = eb(lambda bi: p_ref[bi, :, 128:256] * jnp.exp(-b_all[bi]))

    n = TB // C
    chains = [(bi, c) for c in range(n) for bi in range(bsz)]
    ew = lambda f: [f(bi, slice(c * C, (c + 1) * C)) for bi, c in chains]
    b_last = ew(lambda bi, s: b_all[bi][s.stop - 1:s.stop, :])
    v = ew(lambda bi, s: p_ref[bi, s, 256:512])
    q_e = ew(lambda bi, s: qe_all[bi][s])
    k_e = ew(lambda bi, s: ke_all[bi][s])
    kdec = ew(lambda bi, s: p_ref[bi, s, 128:256] * jnp.exp(b_all[bi][s.stop - 1:s.stop, :] - b_all[bi][s]))
    ei = lambda f: [f(i) for i in range(len(chains))]
    q_st = ei(lambda i: jnp.concatenate([jnp.where(qhead == h, q_e[i], 0.0) for h in range(GLA_H)], axis=0))
    att = ei(lambda i: jnp.where(tril4, _mm_nt(q_st[i], k_e[i]), 0.0))
    r = ei(lambda i: _mm(att[i], v[i]))
    upd = ei(lambda i: jnp.where(smask, _mm_tn(v[i], kdec[i]), 0.0))

    st = [st_ref[bi] for bi in range(bsz)]
    outs = [[] for _ in range(bsz)]
    for i, (bi, c) in enumerate(chains):
        o = _mm_nt(q_e[i], st[bi])
        for h in range(GLA_H):
            o = o + jnp.where(vhead == h, r[i][h * C:(h + 1) * C], 0.0)
        outs[bi].append(o)
        st[bi] = st[bi] * jnp.exp(b_last[i]) + upd[i]
    for bi in range(bsz):
        st_ref[bi] = st[bi]
        o = jnp.concatenate(outs[bi], axis=0)
        ms = _mm_sel_l(o * o, gmat, 2)
        o = o * lax.rsqrt(ms + LN_EPS) * ng_ref[...]
        o_ref[bi] = (o * _silu(p_ref[bi, :, 512:768])).astype(BF16)


def _gla(p, gk_up, gk_bias, norm_g):
    bsz, T, _ = p.shape
    gkup = jnp.zeros((LANE, GLA_QK), F32).at[:GLA_GATE_RANK].set(gk_up)
    return pl.pallas_call(
        functools.partial(_gla_kernel, bsz=bsz),
        out_shape=jax.ShapeDtypeStruct((bsz, T, GLA_W), BF16),
        grid=(T // GLA_TB,),
        in_specs=[pl.BlockSpec((bsz, GLA_TB, GLA_PAD), lambda t: (0, t, 0)),
                  _const_spec((LANE, GLA_QK)), _const_spec((1, GLA_QK)), _const_spec((1, GLA_W))],
        out_specs=pl.BlockSpec((bsz, GLA_TB, GLA_W), lambda t: (0, t, 0)),
        scratch_shapes=[pltpu.VMEM((bsz, GLA_W, GLA_QK), F32)],
        compiler_params=pltpu.CompilerParams(dimension_semantics=("arbitrary",),
                                             vmem_limit_bytes=VMEM_LIMIT),
        name="gla_mixer",
    )(p, gkup, gk_bias.reshape(1, -1), jnp.tile(norm_g, GLA_H).reshape(1, -1))


def _rwkv_kernel(*refs, has_vres, bsz):
    if has_vres:
        (p_ref, vf_ref, mu_ref, w0_ref, wup_ref, a0_ref, aup_ref, gup_ref, kk_ref, ka_ref, rk_ref,
         gng_ref, gnb_ref, v0_ref, vup_ref, zero_ref, o_ref,
         ht_ref, prev_ref, r_s, k_s, v_s, g_s, wr_s, kb_s, arb_s, u0_s, y0_s, dec_s) = refs
    else:
        (p_ref, mu_ref, w0_ref, wup_ref, a0_ref, aup_ref, gup_ref, kk_ref, ka_ref, rk_ref,
         gng_ref, gnb_ref, zero_ref, o_ref, vf_out_ref,
         ht_ref, prev_ref, r_s, k_s, v_s, g_s, wr_s, kb_s, arb_s, u0_s, y0_s, dec_s) = refs
    C, TB, W, H = RW_C, RW_TB, RW_W, RW_H
    carried = (ht_ref, prev_ref, r_s, k_s, v_s, g_s, wr_s, kb_s, arb_s, u0_s, y0_s, dec_s)

    @pl.when(pl.program_id(0) == 0)
    def _():
        for ref in carried:
            ref[...] = jnp.zeros_like(ref)

    head_eq = (_iota((W, W), 0) // RW_DH) == (_iota((W, W), 1) // RW_DH)
    gsum = head_eq.astype(BF16)
    gmean = (head_eq.astype(F32) / RW_DH).astype(BF16)
    tril_blk = (((_iota((TB, TB), 0) // C) == (_iota((TB, TB), 1) // C))
                & (_iota((TB, TB), 0) >= _iota((TB, TB), 1))).astype(BF16)
    ri = _iota((C, W), 0)
    li = _iota((C, W), 1) % C
    strict = ri > li
    incl = ri >= li
    eye = (ri == li).astype(F32)
    blk_eq = (_iota((H * C, W), 0) // C) == (_iota((H * C, W), 1) // RW_DH)

    def bdiag(m):
        return jnp.where(blk_eq, jnp.concatenate([m] * H, axis=0), 0.0).astype(BF16)

    chains = [(bi, c) for c in range(TB // C) for bi in range(bsz)]
    ew = lambda f: [f(i) for i in range(len(chains))]
    sl = [(bi, slice(c * C, (c + 1) * C)) for bi, c in chains]

    def finish_previous():
        ht = [ht_ref[bi] for bi in range(bsz)]
        ys = [[] for _ in range(bsz)]
        for i, (bi, c) in enumerate(chains):
            sh = _mm_nt(wr_s[i], ht[bi])
            u = sh[0:C] + u0_s[i]
            ys[bi].append(sh[C:2 * C] + y0_s[i] + _mm(arb_s[i], bdiag(u)))
            upd = _mm_tn(jnp.concatenate([v_s[sl[i]], u], axis=0), kb_s[i])
            ht[bi] = ht[bi] * dec_s[i] + jnp.where(head_eq, upd, 0.0)
            yield ht[bi][0:8, 0:LANE]
        for bi in range(bsz):
            ht_ref[bi] = ht[bi]
            y = jnp.concatenate(ys[bi], axis=0)
            mean = _mm_sel_l(y, gmean, 2)
            d = y - mean
            var = _mm_sel_l(d * d, gmean, 2)
            y = d * lax.rsqrt(var + RW_GN_EPS) * gng_ref[...] + gnb_ref[...]
            bonus = _mm_sel_l(r_s[bi] * k_s[bi] * rk_ref[...], gsum, 2) * v_s[bi]
            out = (y + bonus) * g_s[bi]
            o_ref[bi] = out.astype(BF16)
            tiles = [out[i:i + 8, j:j + LANE] for i in range(0, TB, 8) for j in range(0, W, LANE)]
            yield functools.reduce(lambda a, b: a + b, tiles)

    previous = finish_previous()

    tokens = []
    sites = [0]

    def tie(x, lag=2):
        token = next(previous, None)
        if token is not None:
            tokens.append(token)
        at_site = sites[0] - lag
        sites[0] += 1
        if not 0 <= at_site < len(tokens):
            return x
        top = x[0:8, 0:LANE] + _ordering_zero(tokens[at_site], zero_ref, LANE)
        if x.shape[1] > LANE:
            top = jnp.concatenate([top, x[0:8, LANE:]], axis=1)
        return jnp.concatenate([top, x[8:]], axis=0)

    rv, kv, vv, gv, av, bv, lwv, csv = [], [], [], [], [], [], [], []
    prev_new = []
    for bi in range(bsz):
        p = p_ref[bi]
        shifted = jnp.where(_iota(p.shape, 0) == 0, prev_ref[bi], pltpu.roll(p, 1, 0))
        prev_new.append(p[TB - 1:TB, :])
        xs = p + (shifted - p) * mu_ref[...]
        r = xs[:, 0:256]
        k = xs[:, 256:512]
        v = xs[:, 512:768]
        w_low = xs[:, 768:896]
        a_low = xs[:, 896:1024]
        g_low = xs[:, 1024:1152]
        if has_vres:
            v_low = xs[:, 1152:1280]
            v = v + (vf_ref[bi] - v) * jax.nn.sigmoid(v0_ref[...] + _mm(v_low, vup_ref[...]))
        else:
            vf_out_ref[bi] = v
        w = -_softplus(-(w0_ref[...] + _mm(jnp.tanh(w_low), wup_ref[...]))) - 0.5
        lw = tie(-jnp.exp(w))
        lwv.append(lw)
        csv.append(_mm_sel_r(tril_blk, lw, 3))
        a = jax.nn.sigmoid(a0_ref[...] + _mm(a_low, aup_ref[...]))
        gv.append(_mm(jax.nn.sigmoid(g_low), gup_ref[...]))
        kk = k * kk_ref[...]
        kk = tie(kk / jnp.maximum(jnp.sqrt(_mm_sel_l(kk * kk, gsum, 2)), 1e-12))
        rv.append(r)
        kv.append(tie(k * (1.0 + (a - 1.0) * ka_ref[...])))
        vv.append(v)
        av.append(-kk)
        bv.append(kk * a)

    cut = lambda vals: ew(lambda i: vals[sl[i][0]][sl[i][1]])
    cs = cut(csv)
    kc = cut(kv)
    bc = cut(bv)
    vc = cut(vv)
    ac = cut(av)
    lwc = cut(lwv)
    rc = cut(rv)
    e_neg = ew(lambda i: jnp.exp(-cs[i]))
    e_neg[0] = tie(e_neg[0])
    at = ew(lambda i: ac[i] * jnp.exp(cs[i] - lwc[i]))
    rt = ew(lambda i: (rc[i] * jnp.exp(cs[i])).astype(BF16))
    ar = ew(lambda i: jnp.concatenate([at[i].astype(BF16), rt[i]], axis=0))
    pk = ew(lambda i: _mm_nt(ar[i], bdiag(kc[i] * e_neg[i])))
    pb = ew(lambda i: _mm_nt(ar[i], bdiag(bc[i] * e_neg[i])))
    a_ak = ew(lambda i: jnp.where(strict, pk[i][0:C], 0.0))
    a_rk = ew(lambda i: jnp.where(incl, pk[i][C:2 * C], 0.0))
    a_ab = ew(lambda i: jnp.where(strict, pb[i][0:C], 0.0))
    a_ab[0] = tie(a_ab[0])
    a_rb = ew(lambda i: jnp.where(incl, pb[i][C:2 * C], 0.0).astype(BF16))
    t = ew(lambda i: eye + a_ab[i])
    pw = ew(lambda i: _mm(a_ab[i], bdiag(a_ab[i])))
    for level in range(5):
        pw[0] = tie(pw[0])
        pw_bd = ew(lambda i: bdiag(pw[i]))
        if level < 4:
            both = ew(lambda i: _mm(jnp.concatenate([t[i], pw[i]], axis=0), pw_bd[i]))
            t = ew(lambda i: t[i] + both[i][0:C])
            pw = ew(lambda i: both[i][C:2 * C])
        else:
            t = ew(lambda i: t[i] + _mm(t[i], pw_bd[i]))
    for _ in previous:
        pass
    v_bd = ew(lambda i: bdiag(vc[i]))
    av_both = ew(lambda i: _mm(jnp.concatenate([a_ak[i], a_rk[i]], axis=0), v_bd[i]))
    wt = ew(lambda i: _mm(t[i], bdiag(at[i])).astype(BF16))
    u0 = ew(lambda i: _mm(t[i], bdiag(av_both[i][0:C])))
    y0 = ew(lambda i: av_both[i][C:2 * C])
    wr = ew(lambda i: jnp.concatenate([wt[i], rt[i]], axis=0))
    e_rem = ew(lambda i: jnp.exp(cs[i][C - 1:C, :] - cs[i]))
    kb = ew(lambda i: jnp.concatenate([kc[i] * e_rem[i], bc[i] * e_rem[i]], axis=0).astype(BF16))

    zrow = _ordering_zero(tokens[-1] + tokens[-2], zero_ref, W)[0:1, :]
    zrow_b = zrow.astype(BF16)
    for bi in range(bsz):
        prev_ref[bi] = prev_new[bi]
        r_s[bi] = rv[bi] + zrow
        k_s[bi] = kv[bi] + zrow
        v_s[bi] = vv[bi] + zrow
        g_s[bi] = gv[bi] + zrow
    for i in range(len(chains)):
        wr_s[i] = wr[i] + zrow_b
        kb_s[i] = kb[i] + zrow_b
        arb_s[i] = a_rb[i] + zrow_b
        u0_s[i] = u0[i] + zrow
        y0_s[i] = y0[i] + zrow
        dec_s[i] = jnp.exp(cs[i][C - 1:C, :]) + zrow


def _rwkv(p, v_first, mu, w0, w_up, a0, a_up, g_up, k_k, k_a, r_k, gn_g, gn_b, v0, v_up):
    bsz, T, width = p.shape
    has_vres = v_first is not None
    row = lambda x: x.reshape(1, -1)
    pad_rows = lambda m: jnp.zeros((LANE, m.shape[1]), F32).at[:m.shape[0]].set(m)
    nt = T // RW_TB
    nch = bsz * (RW_TB // RW_C)
    blk = lambda w: pl.BlockSpec((bsz, RW_TB, w), lambda t: (0, jnp.minimum(t, nt - 1), 0))
    done = lambda w: pl.BlockSpec((bsz, RW_TB, w), lambda t: (0, jnp.maximum(t - 1, 0), 0))
    vec = _const_spec((1, RW_W))
    args = [p]
    specs = [blk(width)]
    if has_vres:
        args.append(v_first)
        specs.append(blk(RW_W))
    args += [row(mu), row(w0), pad_rows(w_up), row(a0), pad_rows(a_up), g_up, row(k_k), row(k_a),
             row(r_k), row(gn_g), row(gn_b)]
    specs += [_const_spec((1, width)), vec, _const_spec((LANE, RW_W)), vec, _const_spec((LANE, RW_W)),
              _const_spec((LANE, RW_W)), vec, vec, vec, vec, vec]
    if has_vres:
        args += [row(v0), pad_rows(v_up)]
        specs += [vec, _const_spec((LANE, RW_W))]
    args.append(jnp.zeros((8, LANE), jnp.uint32))
    specs.append(_const_spec((8, LANE)))
    if has_vres:
        out_shape = jax.ShapeDtypeStruct((bsz, T, RW_W), BF16)
        out_specs = done(RW_W)
    else:
        out_shape = [jax.ShapeDtypeStruct((bsz, T, RW_W), BF16),
                     jax.ShapeDtypeStruct((bsz, T + RW_TB, RW_W), F32)]
        out_specs = [done(RW_W), pl.BlockSpec((bsz, RW_TB, RW_W), lambda t: (0, t, 0))]
    seq = pltpu.VMEM((bsz, RW_TB, RW_W), F32)
    per_chain = lambda rows, dtype: pltpu.VMEM((nch, rows, RW_W), dtype)
    return pl.pallas_call(
        functools.partial(_rwkv_kernel, has_vres=has_vres, bsz=bsz),
        out_shape=out_shape,
        grid=(nt + 1,),
        in_specs=specs,
        out_specs=out_specs,
        scratch_shapes=[pltpu.VMEM((bsz, RW_W, RW_W), F32), pltpu.VMEM((bsz, 1, width), F32)] + [seq] * 4
        + [per_chain(2 * RW_C, BF16), per_chain(2 * RW_C, BF16), per_chain(RW_C, BF16),
           per_chain(RW_C, F32), per_chain(RW_C, F32), per_chain(1, F32)],
        compiler_params=pltpu.CompilerParams(dimension_semantics=("arbitrary",),
                                             vmem_limit_bytes=VMEM_LIMIT),
        name="rwkv7_mixer",
    )(*args)


def _mamba_kernel(p_ref, cw_ref, cb_ref, dtb_ref, alog_ref, d_ref, ng_ref, o_ref, st_ref, xpad_ref, *, bsz):
    L, TB = MB_L, MB_TB

    @pl.when(pl.program_id(0) == 0)
    def _():
        st_ref[...] = jnp.zeros_like(st_ref)
        xpad_ref[:, 0:8, :] = jnp.zeros((bsz, 8, MB_CONV_DIM), F32)

    tril = _iota((L, L), 0) >= _iota((L, L), 1)
    tril_b = tril.astype(BF16)
    expand = (_iota((LANE, MB_DI), 0) == (_iota((LANE, MB_DI), 1) // MB_P)).astype(BF16)
    lane_lo = _iota((1, LANE), 1) < MB_P
    gw = MB_DI // MB_G
    a_neg = -jnp.exp(alog_ref[...])

    xs, bms, cms, dts = [], [], [], []
    for bi in range(bsz):
        xpad_ref[bi, 8:8 + TB, :] = p_ref[bi, :, 512:1536]
        conv = cb_ref[...] + xpad_ref[bi, 8:8 + TB, :] * cw_ref[MB_CONV - 1:MB_CONV, :]
        for s in range(1, MB_CONV):
            conv = conv + xpad_ref[bi, 8 - s:8 - s + TB, :] * cw_ref[MB_CONV - 1 - s:MB_CONV - s, :]
        xpad_ref[bi, 0:8, :] = xpad_ref[bi, TB:TB + 8, :]
        xbc = _silu(conv)
        xs.append(xbc[:, 0:512])
        bms.append(xbc[:, 512:768])
        cms.append(xbc[:, 768:1024])
        dts.append(_softplus(p_ref[bi, :, 1536:1664] + dtb_ref[...]))

    chains = [(bi, c) for c in range(TB // L) for bi in range(bsz)]
    pre = []
    for bi, c in chains:
        sl = slice(c * L, (c + 1) * L)
        dtc = dts[bi][sl]
        cs = _mm_sel_r(tril_b, dtc * a_neg, 3)
        cs_t = cs.T
        cs_x = _mm_sel_l(cs, expand, 3)
        dt_x = _mm_sel_l(dtc, expand, 2)
        cs_last = cs_x[L - 1:L, :]
        xc = xs[bi][sl]
        xdt = xc * dt_x
        xdec = xdt * jnp.exp(cs_last - cs_x)
        y_parts, upd, cgs = [], [], []
        for g in range(MB_G):
            bg = bms[bi][sl, g * MB_N:(g + 1) * MB_N]
            cg = cms[bi][sl, g * MB_N:(g + 1) * MB_N]
            cbm = _mm_nt(cg, bg)
            for hp in range(2):
                pair = g * 2 + hp
                xp = xdt[:, pair * LANE:(pair + 1) * LANE]
                yh = []
                for e in range(2):
                    h = pair * 2 + e
                    diff = cs[:, h:h + 1] - cs_t[h:h + 1, :]
                    lmat = jnp.exp(jnp.where(tril, diff, -1e30))
                    yh.append(_mm(cbm * lmat, xp))
                y_parts.append(jnp.where(lane_lo, yh[0], yh[1]))
            upd.append(_mm_tn(bg, xdec[:, g * gw:(g + 1) * gw]))
            cgs.append(cg.astype(BF16))
        pre.append(dict(y=jnp.concatenate(y_parts, axis=1) + d_ref[...] * xc,
                        upd=jnp.concatenate(upd, axis=1), cg=cgs, e_in=jnp.exp(cs_x),
                        dec=jnp.exp(cs_last)))

    st = [st_ref[bi] for bi in range(bsz)]
    ys = [[] for _ in range(bsz)]
    for (bi, c), q in zip(chains, pre):
        stb = st[bi].astype(BF16)
        y_off = jnp.concatenate([_mm(q["cg"][g], stb[:, g * gw:(g + 1) * gw]) for g in range(MB_G)], axis=1)
        ys[bi].append(q["y"] + y_off * q["e_in"])
        st[bi] = st[bi] * q["dec"] + q["upd"]
    for bi in range(bsz):
        st_ref[bi] = st[bi]
        y = jnp.concatenate(ys[bi], axis=0) * _silu(p_ref[bi, :, 0:512])
        outs = []
        for g in range(MB_G):
            yg = y[:, g * gw:(g + 1) * gw]
            outs.append(yg * lax.rsqrt(jnp.mean(yg * yg, axis=-1, keepdims=True) + LN_EPS))
        o_ref[bi] = (jnp.concatenate(outs, axis=1) * ng_ref[...]).astype(BF16)


def _mamba(p, conv_w, conv_b, dt_bias, a_log, d, norm_g):
    bsz, T, _ = p.shape
    pad = lambda vec: jnp.zeros((1, LANE), F32).at[0, :MB_H].set(vec)
    return pl.pallas_call(
        functools.partial(_mamba_kernel, bsz=bsz),
        out_shape=jax.ShapeDtypeStruct((bsz, T, MB_DI), BF16),
        grid=(T // MB_TB,),
        in_specs=[pl.BlockSpec((bsz, MB_TB, MB_PAD), lambda t: (0, t, 0)),
                  _const_spec((MB_CONV, MB_CONV_DIM)), _const_spec((1, MB_CONV_DIM)),
                  _const_spec((1, LANE)), _const_spec((1, LANE)), _const_spec((1, MB_DI)),
                  _const_spec((1, MB_DI))],
        out_specs=pl.BlockSpec((bsz, MB_TB, MB_DI), lambda t: (0, t, 0)),
        scratch_shapes=[pltpu.VMEM((bsz, MB_N, MB_DI), F32),
                        pltpu.VMEM((bsz, MB_TB + 8, MB_CONV_DIM), F32)],
        compiler_params=pltpu.CompilerParams(dimension_semantics=("arbitrary",),
                                             vmem_limit_bytes=VMEM_LIMIT),
        name="mamba2_mixer",
    )(p, conv_w, conv_b.reshape(1, -1), pad(dt_bias), pad(a_log),
      jnp.repeat(d, MB_P).reshape(1, -1), norm_g.reshape(1, -1))


def _pad_cols(w, width):
    return jnp.pad(w, ((0, 0), (0, width - w.shape[1])))


def _proj_weights(w_in_l, w_vres_l):
    g = w_in_l[:, :GLA_COLS]
    gla = jnp.concatenate([g[:, 0:512], g[:, 528:784], _pad_cols(g[:, 512:528], LANE)], axis=1)
    r = w_in_l[:, GLA_COLS:GLA_COLS + RW_COLS]
    parts = [r[:, 0:768], _pad_cols(r[:, 768:832], LANE), _pad_cols(r[:, 832:896], LANE), r[:, 896:1024]]
    if w_vres_l is not None:
        parts.append(_pad_cols(w_vres_l, LANE))
    rw = jnp.concatenate(parts, axis=1)
    mb = _pad_cols(w_in_l[:, GLA_COLS + RW_COLS:N_IN], MB_PAD)
    return gla.astype(BF16), rw.astype(BF16), mb.astype(BF16)


def _rw_mu(mu_l, mu_vres_l):
    pad = lambda vec: jnp.pad(vec, (0, LANE - vec.shape[0]))
    parts = [mu_l[0:768], pad(mu_l[768:832]), pad(mu_l[832:896]), mu_l[896:1024]]
    if mu_vres_l is not None:
        parts.append(pad(mu_vres_l))
    return jnp.concatenate(parts)


def kernel(x, ln_g, ln_b, ffn_w_in, ffn_w_down, w_in, w_in_vres, w_out, gla_gk_up, gla_gk_bias, gla_norm_g, rw_mu, rw_mu_vres, rw_w0, rw_w_up, rw_a0, rw_a_up, rw_g_up, rw_k_k, rw_k_a, rw_r_k, rw_gn_g, rw_gn_b, rw_v0, rw_v_up, mb_conv_w, mb_conv_b, mb_dt_bias, mb_A_log, mb_D, mb_norm_g):
    bsz, T, _ = x.shape
    n = bsz * T
    x2 = x.reshape(n, D_MODEL)
    v_first = None
    ffn_in, ffn_down = _ffn_weights(ffn_w_in, ffn_w_down)
    for l in range(DEPTH):
        x2 = _ffn_ln(x2, ffn_in, ffn_down, l, 0, ln_g[l, 0], ln_b[l, 0])
        vres = l > 0
        wa, wb, wc = _proj_weights(w_in[l], w_in_vres[l - 1] if vres else None)
        gla_p, rw_p, mb_p = _in_proj(x2, wa, wb, wc)
        gla_o = _gla(gla_p.reshape(bsz, T, -1), gla_gk_up[l], gla_gk_bias[l], gla_norm_g[l])
        mu = _rw_mu(rw_mu[l], rw_mu_vres[l - 1] if vres else None)
        rw_args = (rw_p.reshape(bsz, T, -1), v_first, mu, rw_w0[l], rw_w_up[l], rw_a0[l], rw_a_up[l],
                   rw_g_up[l], rw_k_k[l], rw_k_a[l], rw_r_k[l].reshape(-1), rw_gn_g[l], rw_gn_b[l])
        if vres:
            rw_o = _rwkv(*rw_args, rw_v0[l - 1], rw_v_up[l - 1])
        else:
            rw_o, v_first = _rwkv(*rw_args, None, None)
        mb_o = _mamba(mb_p.reshape(bsz, T, -1), mb_conv_w[l], mb_conv_b[l], mb_dt_bias[l], mb_A_log[l],
                      mb_D[l], mb_norm_g[l])
        x2 = _out_proj_ln(x2, gla_o.reshape(n, -1), rw_o.reshape(n, -1), mb_o.reshape(n, -1),
                          w_out[l], ln_g[l, 1], ln_b[l, 1])
        x2 = _ffn_ln(x2, ffn_in, ffn_down, l, 1, ln_g[l, 2], ln_b[l, 2])
    return x2.reshape(bsz, T, D_MODEL)
```

```python
import functools

import jax
import jax.numpy as jnp
from jax import lax
from jax.experimental import pallas as pl
from jax.experimental.pallas import tpu as pltpu

F32 = jnp.float32
BF16 = jnp.bfloat16

D_MODEL = 1024
DEPTH = 2
D_FF = 2816
LN_EPS = 1e-5
DN_ALPHA = (2 * DEPTH) ** 0.25

GLA_H, GLA_DK, GLA_DV = 4, 32, 64
GLA_GATE_RANK = 16
GLA_GATE_NORM = 16.0
GLA_QK = GLA_H * GLA_DK
GLA_W = GLA_H * GLA_DV
GLA_COLS = 784
GLA_PAD = 896

RW_H, RW_DH = 4, 64
RW_W = 256
RW_COLS = 1024
RW_V_LORA = 32
RW_GN_EPS = 64e-5
RW_PAD0 = 1152
RW_PAD1 = 1280

MB_H, MB_P, MB_DI, MB_G, MB_N = 8, 64, 512, 2, 128
MB_CONV = 4
MB_CONV_DIM = 1024
MB_COLS = 1544
MB_PAD = 1664
N_IN = GLA_COLS + RW_COLS + MB_COLS

LANE = 128
VMEM_LIMIT = 56 * 1024 * 1024

FFN_TM = 1024
FFN_VMEM_LIMIT = 62 * 1024 * 1024
FFN_FC = 256
FFN_LN_ROWS = 32
PROJ_TM = 1024
OUT_TM = 1024
GLA_TB, GLA_C = 512, 64
RW_TB, RW_C = 256, 64
MB_TB, MB_L = 256, 128


def _mm(a, b):
    return jnp.dot(a.astype(BF16), b.astype(BF16), preferred_element_type=F32)


def _mm_nt(a, b):
    return lax.dot_general(a.astype(BF16), b.astype(BF16), (((1,), (1,)), ((), ())),
                           preferred_element_type=F32)


def _mm_tn(a, b):
    return lax.dot_general(a.astype(BF16), b.astype(BF16), (((0,), (0,)), ((), ())),
                           preferred_element_type=F32)


def _split(x, parts):
    out = []
    for _ in range(parts - 1):
        piece = x.astype(BF16)
        out.append(piece)
        x = x - piece.astype(F32)
    out.append(x.astype(BF16))
    return out


def _mm_sel_l(x, sel, parts):
    acc = None
    for piece in _split(x, parts):
        term = jnp.dot(piece, sel, preferred_element_type=F32)
        acc = term if acc is None else acc + term
    return acc


def _mm_sel_r(sel, x, parts):
    acc = None
    for piece in _split(x, parts):
        term = jnp.dot(sel, piece, preferred_element_type=F32)
        acc = term if acc is None else acc + term
    return acc


def _layer_norm(y, g, b):
    mu = jnp.mean(y, axis=-1, keepdims=True)
    d = y - mu
    var = jnp.mean(d * d, axis=-1, keepdims=True)
    return d * lax.rsqrt(var + LN_EPS) * g + b


def _silu(x):
    return x * jax.nn.sigmoid(x)


def _softplus(x):
    return jnp.maximum(x, 0.0) + jnp.log(1.0 + jnp.exp(-jnp.abs(x)))


def _ordering_zero(v, zero_ref, width):
    z = pltpu.bitcast(pltpu.bitcast(v, jnp.uint32) & zero_ref[...], F32)
    return jnp.concatenate([z] * (width // LANE), axis=1)


def _iota(shape, dim):
    return lax.broadcasted_iota(jnp.int32, shape, dim)


def _const_spec(shape):
    n = len(shape)
    return pl.BlockSpec(shape, lambda *_: (0,) * n, pipeline_mode=pl.Buffered(1))


def _ffn_ln_kernel(x_ref, xp_ref, win_ref, wd_ref, g_ref, b_ref, zero_ref, o_ref, acc_ref, accp_ref, *,
                   n_chunks, n_tiles):
    step = pl.program_id(0)

    @pl.when(step == 0)
    def _():
        accp_ref[...] = jnp.zeros_like(accp_ref)

    def finish_previous():
        token = None
        for r in range(0, FFN_TM, FFN_LN_ROWS):
            rows = pl.ds(r, FFN_LN_ROWS)
            out = _layer_norm(DN_ALPHA * xp_ref[rows, :] + accp_ref[rows, :], g_ref[...], b_ref[...])
            o_ref[rows, :] = out
            tiles = [out[i:i + 8, j:j + LANE] for i in range(0, FFN_LN_ROWS, 8) for j in range(0, D_MODEL, LANE)]
            if token is not None:
                tiles.append(token)
            token = functools.reduce(lambda a, b: a + b, tiles)
            yield token

    @pl.when(step < n_tiles)
    def _():
        previous = finish_previous()
        per_chunk = -(-(FFN_TM // FFN_LN_ROWS) // (n_chunks - 1))
        xb = x_ref[...].astype(BF16)
        pin = None
        for c in range(n_chunks):
            gate = jnp.dot(xb, win_ref[:, c * FFN_FC:(c + 1) * FFN_FC], preferred_element_type=F32)
            up = jnp.dot(xb, win_ref[:, D_FF + c * FFN_FC:D_FF + (c + 1) * FFN_FC], preferred_element_type=F32)
            act = _silu(gate) * up
            if pin is not None:
                act = jnp.concatenate([act[0:8, :] + _ordering_zero(pin, zero_ref, FFN_FC), act[8:, :]], axis=0)
            part = jnp.dot(act.astype(BF16), wd_ref[c], preferred_element_type=F32)
            if c == 0:
                acc_ref[...] = part
            elif c < n_chunks - 1:
                acc_ref[...] += part
            else:
                for pin in previous:
                    pass
                accp_ref[...] = acc_ref[...] + part + _ordering_zero(pin, zero_ref, D_MODEL)[0:1, :]
            for _ in range(per_chunk):
                pin = next(previous, pin)

    @pl.when(step == n_tiles)
    def _():
        for _ in finish_previous():
            pass


def _ffn_weights(ffn_w_in, ffn_w_down):
    nc = D_FF // FFN_FC
    wd = (0.5 * ffn_w_down).astype(BF16).reshape(DEPTH, 2, nc, FFN_FC, D_MODEL)
    return ffn_w_in.astype(BF16), wd


def _ffn_ln(x2, w_in_all, wd_all, layer, half, g, b):
    n = x2.shape[0]
    nc = D_FF // FFN_FC
    nt = n // FFN_TM
    pick = lambda shape: pl.BlockSpec((None, None) + shape, lambda i: (layer, half) + (0,) * len(shape),
                                      pipeline_mode=pl.Buffered(1))
    cur = pl.BlockSpec((FFN_TM, D_MODEL), lambda i: (jnp.minimum(i, nt - 1), 0))
    done = pl.BlockSpec((FFN_TM, D_MODEL), lambda i: (jnp.maximum(i - 1, 0), 0))
    return pl.pallas_call(
        functools.partial(_ffn_ln_kernel, n_chunks=nc, n_tiles=nt),
        out_shape=jax.ShapeDtypeStruct((n, D_MODEL), F32),
        grid=(nt + 1,),
        in_specs=[
            cur,
            done,
            pick((D_MODEL, 2 * D_FF)),
            pick((nc, FFN_FC, D_MODEL)),
            _const_spec((1, D_MODEL)),
            _const_spec((1, D_MODEL)),
            _const_spec((8, LANE)),
        ],
        out_specs=done,
        scratch_shapes=[pltpu.VMEM((FFN_TM, D_MODEL), F32), pltpu.VMEM((FFN_TM, D_MODEL), F32)],
        compiler_params=pltpu.CompilerParams(dimension_semantics=("arbitrary",),
                                             vmem_limit_bytes=FFN_VMEM_LIMIT),
        name="ffn_ln",
    )(x2, x2, w_in_all, wd_all, g.reshape(1, -1), b.reshape(1, -1), jnp.zeros((8, LANE), jnp.uint32))


def _in_proj_kernel(x_ref, wa_ref, wb_ref, wc_ref, oa_ref, ob_ref, oc_ref):
    xb = x_ref[...].astype(BF16)
    oa_ref[...] = jnp.dot(xb, wa_ref[...], preferred_element_type=F32)
    ob_ref[...] = jnp.dot(xb, wb_ref[...], preferred_element_type=F32)
    oc_ref[...] = jnp.dot(xb, wc_ref[...], preferred_element_type=F32)


def _in_proj(x2, wa, wb, wc):
    n = x2.shape[0]
    widths = (wa.shape[1], wb.shape[1], wc.shape[1])
    return pl.pallas_call(
        _in_proj_kernel,
        out_shape=[jax.ShapeDtypeStruct((n, w), F32) for w in widths],
        grid=(n // PROJ_TM,),
        in_specs=[pl.BlockSpec((PROJ_TM, D_MODEL), lambda i: (i, 0))]
        + [_const_spec((D_MODEL, w)) for w in widths],
        out_specs=[pl.BlockSpec((PROJ_TM, w), lambda i: (i, 0)) for w in widths],
        compiler_params=pltpu.CompilerParams(dimension_semantics=("parallel",),
                                             vmem_limit_bytes=VMEM_LIMIT),
        name="in_proj",
    )(x2, wa, wb, wc)


def _out_proj_ln_kernel(x_ref, ma_ref, mb_ref, mc_ref, wa_ref, wb_ref, wc_ref, g_ref, b_ref, o_ref):
    acc = jnp.dot(ma_ref[...], wa_ref[...], preferred_element_type=F32)
    acc += jnp.dot(mb_ref[...], wb_ref[...], preferred_element_type=F32)
    acc += jnp.dot(mc_ref[...], wc_ref[...], preferred_element_type=F32)
    o_ref[...] = acc
    for r in range(0, OUT_TM, FFN_LN_ROWS):
        rows = pl.ds(r, FFN_LN_ROWS)
        o_ref[rows, :] = _layer_norm(DN_ALPHA * x_ref[rows, :] + o_ref[rows, :], g_ref[...], b_ref[...])


def _out_proj_ln(x2, ma, mb, mc, w_out, g, b):
    n = x2.shape[0]
    wo = w_out.astype(BF16)
    wa, wb, wc = wo[:GLA_W], wo[GLA_W:GLA_W + RW_W], wo[GLA_W + RW_W:]
    tm = OUT_TM
    row = lambda w: pl.BlockSpec((tm, w), lambda i: (i, 0))
    return pl.pallas_call(
        _out_proj_ln_kernel,
        out_shape=jax.ShapeDtypeStruct((n, D_MODEL), F32),
        grid=(n // tm,),
        in_specs=[row(D_MODEL), row(GLA_W), row(RW_W), row(MB_DI),
                  _const_spec((GLA_W, D_MODEL)), _const_spec((RW_W, D_MODEL)),
                  _const_spec((MB_DI, D_MODEL)), _const_spec((1, D_MODEL)), _const_spec((1, D_MODEL))],
        out_specs=row(D_MODEL),
        compiler_params=pltpu.CompilerParams(dimension_semantics=("parallel",),
                                             vmem_limit_bytes=VMEM_LIMIT),
        name="out_proj_ln",
    )(x2, ma, mb, mc, wa, wb, wc, g.reshape(1, -1), b.reshape(1, -1))


def _gla_kernel(p_ref, gkup_ref, gkb_ref, ng_ref, o_ref, st_ref, *, bsz):
    C = GLA_C

    @pl.when(pl.program_id(0) == 0)
    def _():
        st_ref[...] = jnp.zeros_like(st_ref)

    TB = GLA_TB
    tril_blk = (((_iota((TB, TB), 0) // C) == (_iota((TB, TB), 1) // C))
                & (_iota((TB, TB), 0) >= _iota((TB, TB), 1))).astype(BF16)
    tril4 = (_iota((GLA_H * C, C), 0) % C) >= _iota((GLA_H * C, C), 1)
    qhead = _iota((1, GLA_QK), 1) // GLA_DK
    vhead = _iota((1, GLA_W), 1) // GLA_DV
    smask = (_iota((GLA_W, GLA_QK), 0) // GLA_DV) == (_iota((GLA_W, GLA_QK), 1) // GLA_DK)
    gmat = ((_iota((GLA_W, GLA_W), 0) // GLA_DV) == (_iota((GLA_W, GLA_W), 1) // GLA_DV)).astype(BF16) / GLA_DV

    eb = lambda f: [f(bi) for bi in range(bsz)]
    z = eb(lambda bi: _mm(p_ref[bi, :, 768:896], gkup_ref[...]) + gkb_ref[...])
    gk = eb(lambda bi: -_softplus(-z[bi]) / GLA_GATE_NORM)
    b_all = eb(lambda bi: _mm_sel_r(tril_blk, gk[bi], 3))
    qe_all = eb(lambda bi: p_ref[bi, :, 0:128] * (GLA_DK ** -0.5) * jnp.exp(b_all[bi]))
    ke_all = eb(lambda bi: p_ref[bi, :, 128:256] * jnp.exp(-b_all[bi]))

    n = TB // C
    chains = [(bi, c) for c in range(n) for bi in range(bsz)]
    ew = lambda f: [f(bi, slice(c * C, (c + 1) * C)) for bi, c in chains]
    b_last = ew(lambda bi, s: b_all[bi][s.stop - 1:s.stop, :])
    v = ew(lambda bi, s: p_ref[bi, s, 256:512])
    q_e = ew(lambda bi, s: qe_all[bi][s])
    k_e = ew(lambda bi, s: ke_all[bi][s])
    kdec = ew(lambda bi, s: p_ref[bi, s, 128:256] * jnp.exp(b_all[bi][s.stop - 1:s.stop, :] - b_all[bi][s]))
    ei = lambda f: [f(i) for i in range(len(chains))]
    q_st = ei(lambda i: jnp.concatenate([jnp.where(qhead == h, q_e[i], 0.0) for h in range(GLA_H)], axis=0))
    att = ei(lambda i: jnp.where(tril4, _mm_nt(q_st[i], k_e[i]), 0.0))
    r = ei(lambda i: _mm(att[i], v[i]))
    upd = ei(lambda i: jnp.where(smask, _mm_tn(v[i], kdec[i]), 0.0))

    st = [st_ref[bi] for bi in range(bsz)]
    outs = [[] for _ in range(bsz)]
    for i, (bi, c) in enumerate(chains):
        o = _mm_nt(q_e[i], st[bi])
        for h in range(GLA_H):
            o = o + jnp.where(vhead == h, r[i][h * C:(h + 1) * C], 0.0)
        outs[bi].append(o)
        st[bi] = st[bi] * jnp.exp(b_last[i]) + upd[i]
    for bi in range(bsz):
        st_ref[bi] = st[bi]
        o = jnp.concatenate(outs[bi], axis=0)
        ms = _mm_sel_l(o * o, gmat, 2)
        o = o * lax.rsqrt(ms + LN_EPS) * ng_ref[...]
        o_ref[bi] = (o * _silu(p_ref[bi, :, 512:768])).astype(BF16)


def _gla(p, gk_up, gk_bias, norm_g):
    bsz, T, _ = p.shape
    gkup = jnp.zeros((LANE, GLA_QK), F32).at[:GLA_GATE_RANK].set(gk_up)
    return pl.pallas_call(
        functools.partial(_gla_kernel, bsz=bsz),
        out_shape=jax.ShapeDtypeStruct((bsz, T, GLA_W), BF16),
        grid=(T // GLA_TB,),
        in_specs=[pl.BlockSpec((bsz, GLA_TB, GLA_PAD), lambda t: (0, t, 0)),
                  _const_spec((LANE, GLA_QK)), _const_spec((1, GLA_QK)), _const_spec((1, GLA_W))],
        out_specs=pl.BlockSpec((bsz, GLA_TB, GLA_W), lambda t: (0, t, 0)),
        scratch_shapes=[pltpu.VMEM((bsz, GLA_W, GLA_QK), F32)],
        compiler_params=pltpu.CompilerParams(dimension_semantics=("arbitrary",),
                                             vmem_limit_bytes=VMEM_LIMIT),
        name="gla_mixer",
    )(p, gkup, gk_bias.reshape(1, -1), jnp.tile(norm_g, GLA_H).reshape(1, -1))


def _rwkv_kernel(*refs, has_vres, bsz):
    if has_vres:
        (p_ref, vf_ref, mu_ref, w0_ref, wup_ref, a0_ref, aup_ref, gup_ref, kk_ref, ka_ref, rk_ref,
         gng_ref, gnb_ref, v0_ref, vup_ref, zero_ref, o_ref,
         ht_ref, prev_ref, r_s, k_s, v_s, g_s, wr_s, kb_s, arb_s, u0_s, y0_s, dec_s) = refs
    else:
        (p_ref, mu_ref, w0_ref, wup_ref, a0_ref, aup_ref, gup_ref, kk_ref, ka_ref, rk_ref,
         gng_ref, gnb_ref, zero_ref, o_ref, vf_out_ref,
         ht_ref, prev_ref, r_s, k_s, v_s, g_s, wr_s, kb_s, arb_s, u0_s, y0_s, dec_s) = refs
    C, TB, W, H = RW_C, RW_TB, RW_W, RW_H
    carried = (ht_ref, prev_ref, r_s, k_s, v_s, g_s, wr_s, kb_s, arb_s, u0_s, y0_s, dec_s)

    @pl.when(pl.program_id(0) == 0)
    def _():
        for ref in carried:
            ref[...] = jnp.zeros_like(ref)

    head_eq = (_iota((W, W), 0) // RW_DH) == (_iota((W, W), 1) // RW_DH)
    gsum = head_eq.astype(BF16)
    gmean = (head_eq.astype(F32) / RW_DH).astype(BF16)
    tril_blk = (((_iota((TB, TB), 0) // C) == (_iota((TB, TB), 1) // C))
                & (_iota((TB, TB), 0) >= _iota((TB, TB), 1))).astype(BF16)
    ri = _iota((C, W), 0)
    li = _iota((C, W), 1) % C
    strict = ri > li
    incl = ri >= li
    eye = (ri == li).astype(F32)
    blk_eq = (_iota((H * C, W), 0) // C) == (_iota((H * C, W), 1) // RW_DH)

    def bdiag(m):
        return jnp.where(blk_eq, jnp.concatenate([m] * H, axis=0), 0.0).astype(BF16)

    chains = [(bi, c) for c in range(TB // C) for bi in range(bsz)]
    ew = lambda f: [f(i) for i in range(len(chains))]
    sl = [(bi, slice(c * C, (c + 1) * C)) for bi, c in chains]

    def finish_previous():
        ht = [ht_ref[bi] for bi in range(bsz)]
        ys = [[] for _ in range(bsz)]
        for i, (bi, c) in enumerate(chains):
            sh = _mm_nt(wr_s[i], ht[bi])
            u = sh[0:C] + u0_s[i]
            ys[bi].append(sh[C:2 * C] + y0_s[i] + _mm(arb_s[i], bdiag(u)))
            upd = _mm_tn(jnp.concatenate([v_s[sl[i]], u], axis=0), kb_s[i])
            ht[bi] = ht[bi] * dec_s[i] + jnp.where(head_eq, upd, 0.0)
            yield ht[bi][0:8, 0:LANE]
        for bi in range(bsz):
            ht_ref[bi] = ht[bi]
            y = jnp.concatenate(ys[bi], axis=0)
            mean = _mm_sel_l(y, gmean, 2)
            d = y - mean
            var = _mm_sel_l(d * d, gmean, 2)
            y = d * lax.rsqrt(var + RW_GN_EPS) * gng_ref[...] + gnb_ref[...]
            bonus = _mm_sel_l(r_s[bi] * k_s[bi] * rk_ref[...], gsum, 2) * v_s[bi]
            out = (y + bonus) * g_s[bi]
            o_ref[bi] = out.astype(BF16)
            tiles = [out[i:i + 8, j:j + LANE] for i in range(0, TB, 8) for j in range(0, W, LANE)]
            yield functools.reduce(lambda a, b: a + b, tiles)

    previous = finish_previous()

    tokens = []
    sites = [0]

    def tie(x, lag=2):
        token = next(previous, None)
        if token is not None:
            tokens.append(token)
        at_site = sites[0] - lag
        sites[0] += 1
        if not 0 <= at_site < len(tokens):
            return x
        top = x[0:8, 0:LANE] + _ordering_zero(tokens[at_site], zero_ref, LANE)
        if x.shape[1] > LANE:
            top = jnp.concatenate([top, x[0:8, LANE:]], axis=1)
        return jnp.concatenate([top, x[8:]], axis=0)

    rv, kv, vv, gv, av, bv, lwv, csv = [], [], [], [], [], [], [], []
    prev_new = []
    for bi in range(bsz):
        p = p_ref[bi]
        shifted = jnp.where(_iota(p.shape, 0) == 0, prev_ref[bi], pltpu.roll(p, 1, 0))
        prev_new.append(p[TB - 1:TB, :])
        xs = p + (shifted - p) * mu_ref[...]
        r = xs[:, 0:256]
        k = xs[:, 256:512]
        v = xs[:, 512:768]
        w_low = xs[:, 768:896]
        a_low = xs[:, 896:1024]
        g_low = xs[:, 1024:1152]
        if has_vres:
            v_low = xs[:, 1152:1280]
            v = v + (vf_ref[bi] - v) * jax.nn.sigmoid(v0_ref[...] + _mm(v_low, vup_ref[...]))
        else:
            vf_out_ref[bi] = v
        w = -_softplus(-(w0_ref[...] + _mm(jnp.tanh(w_low), wup_ref[...]))) - 0.5
        lw = tie(-jnp.exp(w))
        lwv.append(lw)
        csv.append(_mm_sel_r(tril_blk, lw, 3))
        a = jax.nn.sigmoid(a0_ref[...] + _mm(a_low, aup_ref[...]))
        gv.append(_mm(jax.nn.sigmoid(g_low), gup_ref[...]))
        kk = k * kk_ref[...]
        kk = tie(kk / jnp.maximum(jnp.sqrt(_mm_sel_l(kk * kk, gsum, 2)), 1e-12))
        rv.append(r)
        kv.append(tie(k * (1.0 + (a - 1.0) * ka_ref[...])))
        vv.append(v)
        av.append(-kk)
        bv.append(kk * a)

    cut = lambda vals: ew(lambda i: vals[sl[i][0]][sl[i][1]])
    cs = cut(csv)
    kc = cut(kv)
    bc = cut(bv)
    vc = cut(vv)
    ac = cut(av)
    lwc = cut(lwv)
    rc = cut(rv)
    e_neg = ew(lambda i: jnp.exp(-cs[i]))
    e_neg[0] = tie(e_neg[0])
    at = ew(lambda i: ac[i] * jnp.exp(cs[i] - lwc[i]))
    rt = ew(lambda i: (rc[i] * jnp.exp(cs[i])).astype(BF16))
    ar = ew(lambda i: jnp.concatenate([at[i].astype(BF16), rt[i]], axis=0))
    pk = ew(lambda i: _mm_nt(ar[i], bdiag(kc[i] * e_neg[i])))
    pb = ew(lambda i: _mm_nt(ar[i], bdiag(bc[i] * e_neg[i])))
    a_ak = ew(lambda i: jnp.where(strict, pk[i][0:C], 0.0))
    a_rk = ew(lambda i: jnp.where(incl, pk[i][C:2 * C], 0.0))
    a_ab = ew(lambda i: jnp.where(strict, pb[i][0:C], 0.0))
    a_ab[0] = tie(a_ab[0])
    a_rb = ew(lambda i: jnp.where(incl, pb[i][C:2 * C], 0.0).astype(BF16))
    t = ew(lambda i: eye + a_ab[i])
    pw = ew(lambda i: _mm(a_ab[i], bdiag(a_ab[i])))
    for level in range(5):
        pw[0] = tie(pw[0])
        pw_bd = ew(lambda i: bdiag(pw[i]))
        if level < 4:
            both = ew(lambda i: _mm(jnp.concatenate([t[i], pw[i]], axis=0), pw_bd[i]))
            t = ew(lambda i: t[i] + both[i][0:C])
            pw = ew(lambda i: both[i][C:2 * C])
        else:
            t = ew(lambda i: t[i] + _mm(t[i], pw_bd[i]))
    for _ in previous:
        pass
    v_bd = ew(lambda i: bdiag(vc[i]))
    av_both = ew(lambda i: _mm(jnp.concatenate([a_ak[i], a_rk[i]], axis=0), v_bd[i]))
    wt = ew(lambda i: _mm(t[i], bdiag(at[i])).astype(BF16))
    u0 = ew(lambda i: _mm(t[i], bdiag(av_both[i][0:C])))
    y0 = ew(lambda i: av_both[i][C:2 * C])
    wr = ew(lambda i: jnp.concatenate([wt[i], rt[i]], axis=0))
    e_rem = ew(lambda i: jnp.exp(cs[i][C - 1:C, :] - cs[i]))
    kb = ew(lambda i: jnp.concatenate([kc[i] * e_rem[i], bc[i] * e_rem[i]], axis=0).astype(BF16))

    zrow = _ordering_zero(tokens[-1] + tokens[-2], zero_ref, W)[0:1, :]
    zrow_b = zrow.astype(BF16)
    for bi in range(bsz):
        prev_ref[bi] = prev_new[bi]
        r_s[bi] = rv[bi] + zrow
        k_s[bi] = kv[bi] + zrow
        v_s[bi] = vv[bi] + zrow
        g_s[bi] = gv[bi] + zrow
    for i in range(len(chains)):
        wr_s[i] = wr[i] + zrow_b
        kb_s[i] = kb[i] + zrow_b
        arb_s[i] = a_rb[i] + zrow_b
        u0_s[i] = u0[i] + zrow
        y0_s[i] = y0[i] + zrow
        dec_s[i] = jnp.exp(cs[i][C - 1:C, :]) + zrow


def _rwkv(p, v_first, mu, w0, w_up, a0, a_up, g_up, k_k, k_a, r_k, gn_g, gn_b, v0, v_up):
    bsz, T, width = p.shape
    has_vres = v_first is not None
    row = lambda x: x.reshape(1, -1)
    pad_rows = lambda m: jnp.zeros((LANE, m.shape[1]), F32).at[:m.shape[0]].set(m)
    nt = T // RW_TB
    nch = bsz * (RW_TB // RW_C)
    blk = lambda w: pl.BlockSpec((bsz, RW_TB, w), lambda t: (0, jnp.minimum(t, nt - 1), 0))
    done = lambda w: pl.BlockSpec((bsz, RW_TB, w), lambda t: (0, jnp.maximum(t - 1, 0), 0))
    vec = _const_spec((1, RW_W))
    args = [p]
    specs = [blk(width)]
    if has_vres:
        args.append(v_first)
        specs.append(blk(RW_W))
    args += [row(mu), row(w0), pad_rows(w_up), row(a0), pad_rows(a_up), g_up, row(k_k), row(k_a),
             row(r_k), row(gn_g), row(gn_b)]
    specs += [_const_spec((1, width)), vec, _const_spec((LANE, RW_W)), vec, _const_spec((LANE, RW_W)),
              _const_spec((LANE, RW_W)), vec, vec, vec, vec, vec]
    if has_vres:
        args += [row(v0), pad_rows(v_up)]
        specs += [vec, _const_spec((LANE, RW_W))]
    args.append(jnp.zeros((8, LANE), jnp.uint32))
    specs.append(_const_spec((8, LANE)))
    if has_vres:
        out_shape = jax.ShapeDtypeStruct((bsz, T, RW_W), BF16)
        out_specs = done(RW_W)
    else:
        out_shape = [jax.ShapeDtypeStruct((bsz, T, RW_W), BF16),
                     jax.ShapeDtypeStruct((bsz, T + RW_TB, RW_W), F32)]
        out_specs = [done(RW_W), pl.BlockSpec((bsz, RW_TB, RW_W), lambda t: (0, t, 0))]
    seq = pltpu.VMEM((bsz, RW_TB, RW_W), F32)
    per_chain = lambda rows, dtype: pltpu.VMEM((nch, rows, RW_W), dtype)
    return pl.pallas_call(
        functools.partial(_rwkv_kernel, has_vres=has_vres, bsz=bsz),
        out_shape=out_shape,
        grid=(nt + 1,),
        in_specs=specs,
        out_specs=out_specs,
        scratch_shapes=[pltpu.VMEM((bsz, RW_W, RW_W), F32), pltpu.VMEM((bsz, 1, width), F32)] + [seq] * 4
        + [per_chain(2 * RW_C, BF16), per_chain(2 * RW_C, BF16), per_chain(RW_C, BF16),
           per_chain(RW_C, F32), per_chain(RW_C, F32), per_chain(1, F32)],
        compiler_params=pltpu.CompilerParams(dimension_semantics=("arbitrary",),
                                             vmem_limit_bytes=VMEM_LIMIT),
        name="rwkv7_mixer",
    )(*args)


def _mamba_kernel(p_ref, cw_ref, cb_ref, dtb_ref, alog_ref, d_ref, ng_ref, o_ref, st_ref, xpad_ref, *, bsz):
    L, TB = MB_L, MB_TB

    @pl.when(pl.program_id(0) == 0)
    def _():
        st_ref[...] = jnp.zeros_like(st_ref)
        xpad_ref[:, 0:8, :] = jnp.zeros((bsz, 8, MB_CONV_DIM), F32)

    tril = _iota((L, L), 0) >= _iota((L, L), 1)
    tril_b = tril.astype(BF16)
    expand = (_iota((LANE, MB_DI), 0) == (_iota((LANE, MB_DI), 1) // MB_P)).astype(BF16)
    lane_lo = _iota((1, LANE), 1) < MB_P
    gw = MB_DI // MB_G
    a_neg = -jnp.exp(alog_ref[...])

    xs, bms, cms, dts = [], [], [], []
    for bi in range(bsz):
        xpad_ref[bi, 8:8 + TB, :] = p_ref[bi, :, 512:1536]
        conv = cb_ref[...] + xpad_ref[bi, 8:8 + TB, :] * cw_ref[MB_CONV - 1:MB_CONV, :]
        for s in range(1, MB_CONV):
            conv = conv + xpad_ref[bi, 8 - s:8 - s + TB, :] * cw_ref[MB_CONV - 1 - s:MB_CONV - s, :]
        xpad_ref[bi, 0:8, :] = xpad_ref[bi, TB:TB + 8, :]
        xbc = _silu(conv)
        xs.append(xbc[:, 0:512])
        bms.append(xbc[:, 512:768])
        cms.append(xbc[:, 768:1024])
        dts.append(_softplus(p_ref[bi, :, 1536:1664] + dtb_ref[...]))

    chains = [(bi, c) for c in range(TB // L) for bi in range(bsz)]
    pre = []
    for bi, c in chains:
        sl = slice(c * L, (c + 1) * L)
        dtc = dts[bi][sl]
        cs = _mm_sel_r(tril_b, dtc * a_neg, 3)
        cs_t = cs.T
        cs_x = _mm_sel_l(cs, expand, 3)
        dt_x = _mm_sel_l(dtc, expand, 2)
        cs_last = cs_x[L - 1:L, :]
        xc = xs[bi][sl]
        xdt = xc * dt_x
        xdec = xdt * jnp.exp(cs_last - cs_x)
        y_parts, upd, cgs = [], [], []
        for g in range(MB_G):
            bg = bms[bi][sl, g * MB_N:(g + 1) * MB_N]
            cg = cms[bi][sl, g * MB_N:(g + 1) * MB_N]
            cbm = _mm_nt(cg, bg)
            for hp in range(2):
                pair = g * 2 + hp
                xp = xdt[:, pair * LANE:(pair + 1) * LANE]
                lhs = []
                for e in range(2):
                    h = pair * 2 + e
                    diff = cs[:, h:h + 1] - cs_t[h:h + 1, :]
                    lhs.append((cbm * jnp.exp(jnp.where(tril, diff, -1e30))).astype(BF16))
                yh = _mm(jnp.concatenate(lhs, axis=0), xp)
                y_parts.append(jnp.where(lane_lo, yh[0:L], yh[L:2 * L]))
            upd.append(_mm_tn(bg, xdec[:, g * gw:(g + 1) * gw]))
            cgs.append(cg.astype(BF16))
        pre.append(dict(y=jnp.concatenate(y_parts, axis=1) + d_ref[...] * xc,
                        upd=jnp.concatenate(upd, axis=1), cg=cgs, e_in=jnp.exp(cs_x),
                        dec=jnp.exp(cs_last)))

    st = [st_ref[bi] for bi in range(bsz)]
    ys = [[] for _ in range(bsz)]
    for (bi, c), q in zip(chains, pre):
        stb = st[bi].astype(BF16)
        y_off = jnp.concatenate([_mm(q["cg"][g], stb[:, g * gw:(g + 1) * gw]) for g in range(MB_G)], axis=1)
        ys[bi].append(q["y"] + y_off * q["e_in"])
        st[bi] = st[bi] * q["dec"] + q["upd"]
    for bi in range(bsz):
        st_ref[bi] = st[bi]
        y = jnp.concatenate(ys[bi], axis=0) * _silu(p_ref[bi, :, 0:512])
        outs = []
        for g in range(MB_G):
            yg = y[:, g * gw:(g + 1) * gw]
            outs.append(yg * lax.rsqrt(jnp.mean(yg * yg, axis=-1, keepdims=True) + LN_EPS))
        o_ref[bi] = (jnp.concatenate(outs, axis=1) * ng_ref[...]).astype(BF16)


def _mamba(p, conv_w, conv_b, dt_bias, a_log, d, norm_g):
    bsz, T, _ = p.shape
    pad = lambda vec: jnp.zeros((1, LANE), F32).at[0, :MB_H].set(vec)
    return pl.pallas_call(
        functools.partial(_mamba_kernel, bsz=bsz),
        out_shape=jax.ShapeDtypeStruct((bsz, T, MB_DI), BF16),
        grid=(T // MB_TB,),
        in_specs=[pl.BlockSpec((bsz, MB_TB, MB_PAD), lambda t: (0, t, 0)),
                  _const_spec((MB_CONV, MB_CONV_DIM)), _const_spec((1, MB_CONV_DIM)),
                  _const_spec((1, LANE)), _const_spec((1, LANE)), _const_spec((1, MB_DI)),
                  _const_spec((1, MB_DI))],
        out_specs=pl.BlockSpec((bsz, MB_TB, MB_DI), lambda t: (0, t, 0)),
        scratch_shapes=[pltpu.VMEM((bsz, MB_N, MB_DI), F32),
                        pltpu.VMEM((bsz, MB_TB + 8, MB_CONV_DIM), F32)],
        compiler_params=pltpu.CompilerParams(dimension_semantics=("arbitrary",),
                                             vmem_limit_bytes=VMEM_LIMIT),
        name="mamba2_mixer",
    )(p, conv_w, conv_b.reshape(1, -1), pad(dt_bias), pad(a_log),
      jnp.repeat(d, MB_P).reshape(1, -1), norm_g.reshape(1, -1))


def _pad_cols(w, width):
    return jnp.pad(w, ((0, 0), (0, width - w.shape[1])))


def _proj_weights(w_in_l, w_vres_l):
    g = w_in_l[:, :GLA_COLS]
    gla = jnp.concatenate([g[:, 0:512], g[:, 528:784], _pad_cols(g[:, 512:528], LANE)], axis=1)
    r = w_in_l[:, GLA_COLS:GLA_COLS + RW_COLS]
    parts = [r[:, 0:768], _pad_cols(r[:, 768:832], LANE), _pad_cols(r[:, 832:896], LANE), r[:, 896:1024]]
    if w_vres_l is not None:
        parts.append(_pad_cols(w_vres_l, LANE))
    rw = jnp.concatenate(parts, axis=1)
    mb = _pad_cols(w_in_l[:, GLA_COLS + RW_COLS:N_IN], MB_PAD)
    return gla.astype(BF16), rw.astype(BF16), mb.astype(BF16)


def _rw_mu(mu_l, mu_vres_l):
    pad = lambda vec: jnp.pad(vec, (0, LANE - vec.shape[0]))
    parts = [mu_l[0:768], pad(mu_l[768:832]), pad(mu_l[832:896]), mu_l[896:1024]]
    if mu_vres_l is not None:
        parts.append(pad(mu_vres_l))
    return jnp.concatenate(parts)


def kernel(x, ln_g, ln_b, ffn_w_in, ffn_w_down, w_in, w_in_vres, w_out, gla_gk_up, gla_gk_bias, gla_norm_g, rw_mu, rw_mu_vres, rw_w0, rw_w_up, rw_a0, rw_a_up, rw_g_up, rw_k_k, rw_k_a, rw_r_k, rw_gn_g, rw_gn_b, rw_v0, rw_v_up, mb_conv_w, mb_conv_b, mb_dt_bias, mb_A_log, mb_D, mb_norm_g):
    bsz, T, _ = x.shape
    n = bsz * T
    x2 = x.reshape(n, D_MODEL)
    v_first = None
    ffn_in, ffn_down = _ffn_weights(ffn_w_in, ffn_w_down)
    for l in range(DEPTH):
        x2 = _ffn_ln(x2, ffn_in, ffn_down, l, 0, ln_g[l, 0], ln_b[l, 0])
        vres = l > 0
        wa, wb, wc = _proj_weights(w_in[l], w_in_vres[l - 1] if vres else None)
        gla_p, rw_p, mb_p = _in_proj(x2, wa, wb, wc)
        gla_o = _gla(gla_p.reshape(bsz, T, -1), gla_gk_up[l], gla_gk_bias[l], gla_norm_g[l])
        mu = _rw_mu(rw_mu[l], rw_mu_vres[l - 1] if vres else None)
        rw_args = (rw_p.reshape(bsz, T, -1), v_first, mu, rw_w0[l], rw_w_up[l], rw_a0[l], rw_a_up[l],
                   rw_g_up[l], rw_k_k[l], rw_k_a[l], rw_r_k[l].reshape(-1), rw_gn_g[l], rw_gn_b[l])
        if vres:
            rw_o = _rwkv(*rw_args, rw_v0[l - 1], rw_v_up[l - 1])
        else:
            rw_o, v_first = _rwkv(*rw_args, None, None)
        mb_o = _mamba(mb_p.reshape(bsz, T, -1), mb_conv_w[l], mb_conv_b[l], mb_dt_bias[l], mb_A_log[l],
                      mb_D[l], mb_norm_g[l])
        x2 = _out_proj_ln(x2, gla_o.reshape(n, -1), rw_o.reshape(n, -1), mb_o.reshape(n, -1),
                          w_out[l], ln_g[l, 1], ln_b[l, 1])
        x2 = _ffn_ln(x2, ffn_in, ffn_down, l, 1, ln_g[l, 2], ln_b[l, 2])
    return x2.reshape(bsz, T, D_MODEL)
```

```python
import functools

import jax
import jax.numpy as jnp
from jax import lax
from jax.experimental import pallas as pl
from jax.experimental.pallas import tpu as pltpu

F32 = jnp.float32
BF16 = jnp.bfloat16

D_MODEL = 1024
DEPTH = 2
D_FF = 2816
LN_EPS = 1e-5
DN_ALPHA = (2 * DEPTH) ** 0.25

GLA_H, GLA_DK, GLA_DV = 4, 32, 64
GLA_GATE_RANK = 16
GLA_GATE_NORM = 16.0
GLA_QK = GLA_H * GLA_DK
GLA_W = GLA_H * GLA_DV
GLA_COLS = 784
GLA_PAD = 768

RW_H, RW_DH = 4, 64
RW_W = 256
RW_COLS = 1024
RW_V_LORA = 32
RW_GN_EPS = 64e-5
RW_LOW = 64

MB_H, MB_P, MB_DI, MB_G, MB_N = 8, 64, 512, 2, 128
MB_CONV = 4
MB_CONV_DIM = 1024
MB_COLS = 1544
MB_PAD = 1536
N_IN = GLA_COLS + RW_COLS + MB_COLS
MISC_GL = 0
MISC_DT = MISC_GL + GLA_GATE_RANK
MISC_VRES = MISC_DT + MB_H

LANE = 128
VMEM_LIMIT = 56 * 1024 * 1024

FFN_TM = 512
FFN_FC = 256
FFN_LN_ROWS = 32
PROJ_TM = 1024
OUT_TM = 1024
GLA_TB, GLA_C = 512, 64
RW_TB, RW_C = 256, 64
MB_TB, MB_L = 256, 128


def _mm(a, b):
    return jnp.dot(a.astype(BF16), b.astype(BF16), preferred_element_type=F32)


def _mm_nt(a, b):
    return lax.dot_general(a.astype(BF16), b.astype(BF16), (((1,), (1,)), ((), ())),
                           preferred_element_type=F32)


def _mm_tn(a, b):
    return lax.dot_general(a.astype(BF16), b.astype(BF16), (((0,), (0,)), ((), ())),
                           preferred_element_type=F32)


def _split(x, parts):
    out = []
    for _ in range(parts - 1):
        piece = x.astype(BF16)
        out.append(piece)
        x = x - piece.astype(F32)
    out.append(x.astype(BF16))
    return out


def _mm_sel_l(x, sel, parts):
    rows = x.shape[0]
    prod = jnp.dot(jnp.concatenate(_split(x, parts), axis=0), sel, preferred_element_type=F32)
    acc = prod[0:rows]
    for p in range(1, parts):
        acc = acc + prod[p * rows:(p + 1) * rows]
    return acc


def _mm_sel_r(sel, x, parts):
    acc = None
    for piece in _split(x, parts):
        term = jnp.dot(sel, piece, preferred_element_type=F32)
        acc = term if acc is None else acc + term
    return acc


def _layer_norm(y, g, b):
    mu = jnp.mean(y, axis=-1, keepdims=True)
    d = y - mu
    var = jnp.mean(d * d, axis=-1, keepdims=True)
    return d * lax.rsqrt(var + LN_EPS) * g + b


def _silu(x):
    return x * jax.nn.sigmoid(x)


def _softplus(x):
    return jnp.maximum(x, 0.0) + jnp.log(1.0 + jnp.exp(-jnp.abs(x)))


def _ordering_zero(v, zero_ref, width):
    z = pltpu.bitcast(pltpu.bitcast(v, jnp.uint32) & zero_ref[...], F32)
    return jnp.concatenate([z] * (width // LANE), axis=1)


def _iota(shape, dim):
    return lax.broadcasted_iota(jnp.int32, shape, dim)


def _const_spec(shape):
    n = len(shape)
    return pl.BlockSpec(shape, lambda *_: (0,) * n, pipeline_mode=pl.Buffered(1))


def _ffn_ln_kernel(x_ref, xp_ref, win_ref, wd_ref, g_ref, b_ref, zero_ref, o_ref, acc_ref, accp_ref, *,
                   n_chunks, n_tiles):
    step = pl.program_id(0)

    @pl.when(step == 0)
    def _():
        accp_ref[...] = jnp.zeros_like(accp_ref)

    def finish_previous():
        token = None
        for r in range(0, FFN_TM, FFN_LN_ROWS):
            rows = pl.ds(r, FFN_LN_ROWS)
            out = _layer_norm(DN_ALPHA * xp_ref[rows, :] + accp_ref[rows, :], g_ref[...], b_ref[...])
            o_ref[rows, :] = out
            tiles = [out[i:i + 8, j:j + LANE] for i in range(0, FFN_LN_ROWS, 8) for j in range(0, D_MODEL, LANE)]
            if token is not None:
                tiles.append(token)
            token = functools.reduce(lambda a, b: a + b, tiles)
            yield token

    @pl.when(step < n_tiles)
    def _():
        previous = finish_previous()
        per_chunk = -(-(FFN_TM // FFN_LN_ROWS) // (n_chunks - 1))
        xb = x_ref[...].astype(BF16)
        pin = None
        for c in range(n_chunks):
            gate = jnp.dot(xb, win_ref[:, c * FFN_FC:(c + 1) * FFN_FC], preferred_element_type=F32)
            up = jnp.dot(xb, win_ref[:, D_FF + c * FFN_FC:D_FF + (c + 1) * FFN_FC], preferred_element_type=F32)
            act = _silu(gate) * up
            if pin is not None:
                act = jnp.concatenate([act[0:8, :] + _ordering_zero(pin, zero_ref, FFN_FC), act[8:, :]], axis=0)
            part = jnp.dot(act.astype(BF16), wd_ref[c], preferred_element_type=F32)
            if c == 0:
                acc_ref[...] = part
            elif c < n_chunks - 1:
                acc_ref[...] += part
            else:
                for pin in previous:
                    pass
                accp_ref[...] = acc_ref[...] + part + _ordering_zero(pin, zero_ref, D_MODEL)[0:1, :]
            for _ in range(per_chunk):
                pin = next(previous, pin)

    @pl.when(step == n_tiles)
    def _():
        for _ in finish_previous():
            pass


def _ffn_weights(ffn_w_in, ffn_w_down):
    nc = D_FF // FFN_FC
    wd = (0.5 * ffn_w_down).astype(BF16).reshape(DEPTH, 2, nc, FFN_FC, D_MODEL)
    return ffn_w_in.astype(BF16), wd


def _ffn_ln(x2, w_in_all, wd_all, layer, half, g, b):
    n = x2.shape[0]
    nc = D_FF // FFN_FC
    nt = n // FFN_TM
    pick = lambda shape: pl.BlockSpec((None, None) + shape, lambda i: (layer, half) + (0,) * len(shape),
                                      pipeline_mode=pl.Buffered(1))
    cur = pl.BlockSpec((FFN_TM, D_MODEL), lambda i: (jnp.minimum(i, nt - 1), 0))
    done = pl.BlockSpec((FFN_TM, D_MODEL), lambda i: (jnp.maximum(i - 1, 0), 0))
    return pl.pallas_call(
        functools.partial(_ffn_ln_kernel, n_chunks=nc, n_tiles=nt),
        out_shape=jax.ShapeDtypeStruct((n, D_MODEL), F32),
        grid=(nt + 1,),
        in_specs=[
            cur,
            done,
            pick((D_MODEL, 2 * D_FF)),
            pick((nc, FFN_FC, D_MODEL)),
            _const_spec((1, D_MODEL)),
            _const_spec((1, D_MODEL)),
            _const_spec((8, LANE)),
        ],
        out_specs=done,
        scratch_shapes=[pltpu.VMEM((FFN_TM, D_MODEL), F32), pltpu.VMEM((FFN_TM, D_MODEL), F32)],
        compiler_params=pltpu.CompilerParams(dimension_semantics=("arbitrary",),
                                             vmem_limit_bytes=VMEM_LIMIT),
        name="ffn_ln",
    )(x2, x2, w_in_all, wd_all, g.reshape(1, -1), b.reshape(1, -1), jnp.zeros((8, LANE), jnp.uint32))


def _in_proj_kernel(x_ref, *refs):
    xb = x_ref[...].astype(BF16)
    n_groups = len(refs) // 2
    for w_ref, o_ref in zip(refs[:n_groups], refs[n_groups:]):
        o_ref[...] = jnp.dot(xb, w_ref[...], preferred_element_type=F32)


def _in_proj(x2, weights):
    n = x2.shape[0]
    widths = [w.shape[1] for w in weights]
    return pl.pallas_call(
        _in_proj_kernel,
        out_shape=[jax.ShapeDtypeStruct((n, w), F32) for w in widths],
        grid=(n // PROJ_TM,),
        in_specs=[pl.BlockSpec((PROJ_TM, D_MODEL), lambda i: (i, 0))]
        + [_const_spec((D_MODEL, w)) for w in widths],
        out_specs=[pl.BlockSpec((PROJ_TM, w), lambda i: (i, 0)) for w in widths],
        compiler_params=pltpu.CompilerParams(dimension_semantics=("parallel",),
                                             vmem_limit_bytes=VMEM_LIMIT),
        name="in_proj",
    )(x2, *weights)


def _out_proj_ln_kernel(x_ref, ma_ref, mb_ref, mc_ref, wa_ref, wb_ref, wc_ref, g_ref, b_ref, o_ref):
    acc = jnp.dot(ma_ref[...], wa_ref[...], preferred_element_type=F32)
    acc += jnp.dot(mb_ref[...], wb_ref[...], preferred_element_type=F32)
    acc += jnp.dot(mc_ref[...], wc_ref[...], preferred_element_type=F32)
    o_ref[...] = acc
    for r in range(0, OUT_TM, FFN_LN_ROWS):
        rows = pl.ds(r, FFN_LN_ROWS)
        o_ref[rows, :] = _layer_norm(DN_ALPHA * x_ref[rows, :] + o_ref[rows, :], g_ref[...], b_ref[...])


def _out_proj_ln(x2, ma, mb, mc, w_out, g, b):
    n = x2.shape[0]
    wo = w_out.astype(BF16)
    wa, wb, wc = wo[:GLA_W], wo[GLA_W:GLA_W + RW_W], wo[GLA_W + RW_W:]
    tm = OUT_TM
    row = lambda w: pl.BlockSpec((tm, w), lambda i: (i, 0))
    return pl.pallas_call(
        _out_proj_ln_kernel,
        out_shape=jax.ShapeDtypeStruct((n, D_MODEL), F32),
        grid=(n // tm,),
        in_specs=[row(D_MODEL), row(GLA_W), row(RW_W), row(MB_DI),
                  _const_spec((GLA_W, D_MODEL)), _const_spec((RW_W, D_MODEL)),
                  _const_spec((MB_DI, D_MODEL)), _const_spec((1, D_MODEL)), _const_spec((1, D_MODEL))],
        out_specs=row(D_MODEL),
        compiler_params=pltpu.CompilerParams(dimension_semantics=("parallel",),
                                             vmem_limit_bytes=VMEM_LIMIT),
        name="out_proj_ln",
    )(x2, ma, mb, mc, wa, wb, wc, g.reshape(1, -1), b.reshape(1, -1))


def _gla_kernel(p_ref, m_ref, gkup_ref, gkb_ref, ng_ref, o_ref, st_ref, *, bsz):
    C = GLA_C

    @pl.when(pl.program_id(0) == 0)
    def _():
        st_ref[...] = jnp.zeros_like(st_ref)

    TB = GLA_TB
    tril_blk = (((_iota((TB, TB), 0) // C) == (_iota((TB, TB), 1) // C))
                & (_iota((TB, TB), 0) >= _iota((TB, TB), 1))).astype(BF16)
    tril4 = (_iota((GLA_H * C, C), 0) % C) >= _iota((GLA_H * C, C), 1)
    qhead = _iota((1, GLA_QK), 1) // GLA_DK
    vhead = _iota((1, GLA_W), 1) // GLA_DV
    smask = (_iota((GLA_W, GLA_QK), 0) // GLA_DV) == (_iota((GLA_W, GLA_QK), 1) // GLA_DK)
    gmat = ((_iota((GLA_W, GLA_W), 0) // GLA_DV) == (_iota((GLA_W, GLA_W), 1) // GLA_DV)).astype(BF16) / GLA_DV

    eb = lambda f: [f(bi) for bi in range(bsz)]
    z = eb(lambda bi: _mm(m_ref[bi], gkup_ref[...]) + gkb_ref[...])
    gk = eb(lambda bi: -_softplus(-z[bi]) / GLA_GATE_NORM)
    b_all = eb(lambda bi: _mm_sel_r(tril_blk, gk[bi], 3))
    qe_all = eb(lambda bi: p_ref[bi, :, 0:128] * (GLA_DK ** -0.5) * jnp.exp(b_all[bi]))
    ke_all = eb(lambda bi: p_ref[bi, :, 128:256] * jnp.exp(-b_all[bi]))

    n = TB // C
    chains = [(bi, c) for c in range(n) for bi in range(bsz)]
    ew = lambda f: [f(bi, slice(c * C, (c + 1) * C)) for bi, c in chains]
    b_last = ew(lambda bi, s: b_all[bi][s.stop - 1:s.stop, :])
    v = ew(lambda bi, s: p_ref[bi, s, 256:512])
    q_e = ew(lambda bi, s: qe_all[bi][s])
    k_e = ew(lambda bi, s: ke_all[bi][s])
    kdec = ew(lambda bi, s: p_ref[bi, s, 128:256] * jnp.exp(b_all[bi][s.stop - 1:s.stop, :] - b_all[bi][s]))
    ei = lambda f: [f(i) for i in range(len(chains))]
    q_st = ei(lambda i: jnp.concatenate([jnp.where(qhead == h, q_e[i], 0.0) for h in range(GLA_H)], axis=0))
    att = ei(lambda i: jnp.where(tril4, _mm_nt(q_st[i], k_e[i]), 0.0))
    r = ei(lambda i: _mm(att[i], v[i]))
    upd = ei(lambda i: jnp.where(smask, _mm_tn(v[i], kdec[i]), 0.0))

    st = [st_ref[bi] for bi in range(bsz)]
    outs = [[] for _ in range(bsz)]
    for i, (bi, c) in enumerate(chains):
        o = _mm_nt(q_e[i], st[bi])
        for h in range(GLA_H):
            o = o + jnp.where(vhead == h, r[i][h * C:(h + 1) * C], 0.0)
        outs[bi].append(o)
        st[bi] = st[bi] * jnp.exp(b_last[i]) + upd[i]
    for bi in range(bsz):
        st_ref[bi] = st[bi]
        o = jnp.concatenate(outs[bi], axis=0)
        ms = _mm_sel_l(o * o, gmat, 2)
        o = o * lax.rsqrt(ms + LN_EPS) * ng_ref[...]
        o_ref[bi] = (o * _silu(p_ref[bi, :, 512:768])).astype(BF16)


def _gla(p, misc, gk_up, gk_bias, norm_g):
    bsz, T, _ = p.shape
    gkup = _at_rows(gk_up, MISC_GL)
    return pl.pallas_call(
        functools.partial(_gla_kernel, bsz=bsz),
        out_shape=jax.ShapeDtypeStruct((bsz, T, GLA_W), BF16),
        grid=(T // GLA_TB,),
        in_specs=[pl.BlockSpec((bsz, GLA_TB, GLA_PAD), lambda t: (0, t, 0)),
                  pl.BlockSpec((bsz, GLA_TB, LANE), lambda t: (0, t, 0)),
                  _const_spec((LANE, GLA_QK)), _const_spec((1, GLA_QK)), _const_spec((1, GLA_W))],
        out_specs=pl.BlockSpec((bsz, GLA_TB, GLA_W), lambda t: (0, t, 0)),
        scratch_shapes=[pltpu.VMEM((bsz, GLA_W, GLA_QK), F32)],
        compiler_params=pltpu.CompilerParams(dimension_semantics=("arbitrary",),
                                             vmem_limit_bytes=VMEM_LIMIT),
        name="gla_mixer",
    )(p, misc, gkup, gk_bias.reshape(1, -1), jnp.tile(norm_g, GLA_H).reshape(1, -1))


def _rwkv_kernel(*refs, has_vres, bsz):
    if has_vres:
        (p_ref, m_ref, vf_ref, mu_ref, mum_ref, w0_ref, wup_ref, a0_ref, aup_ref, gup_ref, kk_ref, ka_ref, rk_ref,
         gng_ref, gnb_ref, v0_ref, vup_ref, zero_ref, o_ref,
         ht_ref, prev_ref, prevm_ref, r_s, k_s, v_s, g_s, wr_s, kb_s, arb_s, u0_s, y0_s, dec_s) = refs
        carried = (prevm_ref,)
    else:
        (p_ref, mu_ref, w0_ref, wup_ref, a0_ref, aup_ref, gup_ref, kk_ref, ka_ref, rk_ref,
         gng_ref, gnb_ref, zero_ref, o_ref, vf_out_ref,
         ht_ref, prev_ref, r_s, k_s, v_s, g_s, wr_s, kb_s, arb_s, u0_s, y0_s, dec_s) = refs
        carried = ()
    C, TB, W, H = RW_C, RW_TB, RW_W, RW_H
    carried += (ht_ref, prev_ref, r_s, k_s, v_s, g_s, wr_s, kb_s, arb_s, u0_s, y0_s, dec_s)

    @pl.when(pl.program_id(0) == 0)
    def _():
        for ref in carried:
            ref[...] = jnp.zeros_like(ref)

    head_eq = (_iota((W, W), 0) // RW_DH) == (_iota((W, W), 1) // RW_DH)
    gsum = head_eq.astype(BF16)
    gmean = (head_eq.astype(F32) / RW_DH).astype(BF16)
    tril_blk = (((_iota((TB, TB), 0) // C) == (_iota((TB, TB), 1) // C))
                & (_iota((TB, TB), 0) >= _iota((TB, TB), 1))).astype(BF16)
    ri = _iota((C, W), 0)
    li = _iota((C, W), 1) % C
    strict = ri > li
    incl = ri >= li
    eye = (ri == li).astype(F32)
    blk_eq = (_iota((H * C, W), 0) // C) == (_iota((H * C, W), 1) // RW_DH)

    def bdiag(m):
        return jnp.where(blk_eq, jnp.concatenate([m] * H, axis=0), 0.0).astype(BF16)

    chains = [(bi, c) for c in range(TB // C) for bi in range(bsz)]
    ew = lambda f: [f(i) for i in range(len(chains))]
    sl = [(bi, slice(c * C, (c + 1) * C)) for bi, c in chains]

    def finish_previous():
        ht = [ht_ref[bi] for bi in range(bsz)]
        ys = [[] for _ in range(bsz)]
        for i, (bi, c) in enumerate(chains):
            sh = _mm_nt(wr_s[i], ht[bi])
            u = sh[0:C] + u0_s[i]
            ys[bi].append(sh[C:2 * C] + y0_s[i] + _mm(arb_s[i], bdiag(u)))
            upd = _mm_tn(jnp.concatenate([v_s[sl[i]], u], axis=0), kb_s[i])
            ht[bi] = ht[bi] * dec_s[i] + jnp.where(head_eq, upd, 0.0)
            yield ht[bi][0:8, 0:LANE]
        for bi in range(bsz):
            ht_ref[bi] = ht[bi]
            y = jnp.concatenate(ys[bi], axis=0)
            mean = _mm_sel_l(y, gmean, 2)
            d = y - mean
            var = _mm_sel_l(d * d, gmean, 2)
            y = d * lax.rsqrt(var + RW_GN_EPS) * gng_ref[...] + gnb_ref[...]
            bonus = _mm_sel_l(r_s[bi] * k_s[bi] * rk_ref[...], gsum, 2) * v_s[bi]
            out = (y + bonus) * g_s[bi]
            o_ref[bi] = out.astype(BF16)
            tiles = [out[i:i + 8, j:j + LANE] for i in range(0, TB, 8) for j in range(0, W, LANE)]
            yield functools.reduce(lambda a, b: a + b, tiles)

    previous = finish_previous()

    tokens = []
    sites = [0]

    def tie(x, lag=2):
        token = next(previous, None)
        if token is not None:
            tokens.append(token)
        at_site = sites[0] - lag
        sites[0] += 1
        if not 0 <= at_site < len(tokens):
            return x
        top = x[0:8, 0:LANE] + _ordering_zero(tokens[at_site], zero_ref, LANE)
        if x.shape[1] > LANE:
            top = jnp.concatenate([top, x[0:8, LANE:]], axis=1)
        return jnp.concatenate([top, x[8:]], axis=0)

    rv, kv, vv, gv, av, bv, lwv, csv = [], [], [], [], [], [], [], []
    prev_new, prevm_new = [], []
    for bi in range(bsz):
        p = p_ref[bi]
        shifted = jnp.where(_iota(p.shape, 0) == 0, prev_ref[bi], pltpu.roll(p, 1, 0))
        prev_new.append(p[TB - 1:TB, :])
        xs = p + (shifted - p) * mu_ref[...]
        r = xs[:, 0:256]
        k = xs[:, 256:512]
        v = xs[:, 512:768]
        wa_low = xs[:, 768:896]
        g_low = xs[:, 896:1024]
        if has_vres:
            m = m_ref[bi]
            shifted_m = jnp.where(_iota(m.shape, 0) == 0, prevm_ref[bi], pltpu.roll(m, 1, 0))
            prevm_new.append(m[TB - 1:TB, :])
            v_low = m + (shifted_m - m) * mum_ref[...]
            v = v + (vf_ref[bi] - v) * jax.nn.sigmoid(v0_ref[...] + _mm(v_low, vup_ref[...]))
        else:
            vf_out_ref[bi] = v
        w_low = a_low = wa_low
        w = -_softplus(-(w0_ref[...] + _mm(jnp.tanh(w_low), wup_ref[...]))) - 0.5
        lw = tie(-jnp.exp(w))
        lwv.append(lw)
        csv.append(_mm_sel_r(tril_blk, lw, 3))
        a = jax.nn.sigmoid(a0_ref[...] + _mm(a_low, aup_ref[...]))
        gv.append(_mm(jax.nn.sigmoid(g_low), gup_ref[...]))
        kk = k * kk_ref[...]
        kk = tie(kk / jnp.maximum(jnp.sqrt(_mm_sel_l(kk * kk, gsum, 2)), 1e-12))
        rv.append(r)
        kv.append(tie(k * (1.0 + (a - 1.0) * ka_ref[...])))
        vv.append(v)
        av.append(-kk)
        bv.append(kk * a)

    cut = lambda vals: ew(lambda i: vals[sl[i][0]][sl[i][1]])
    cs = cut(csv)
    kc = cut(kv)
    bc = cut(bv)
    vc = cut(vv)
    ac = cut(av)
    lwc = cut(lwv)
    rc = cut(rv)
    e_neg = ew(lambda i: jnp.exp(-cs[i]))
    e_neg[0] = tie(e_neg[0])
    at = ew(lambda i: ac[i] * jnp.exp(cs[i] - lwc[i]))
    rt = ew(lambda i: (rc[i] * jnp.exp(cs[i])).astype(BF16))
    ar = ew(lambda i: jnp.concatenate([at[i].astype(BF16), rt[i]], axis=0))
    pk = ew(lambda i: _mm_nt(ar[i], bdiag(kc[i] * e_neg[i])))
    pb = ew(lambda i: _mm_nt(ar[i], bdiag(bc[i] * e_neg[i])))
    a_ak = ew(lambda i: jnp.where(strict, pk[i][0:C], 0.0))
    a_rk = ew(lambda i: jnp.where(incl, pk[i][C:2 * C], 0.0))
    a_ab = ew(lambda i: jnp.where(strict, pb[i][0:C], 0.0))
    a_ab[0] = tie(a_ab[0])
    a_rb = ew(lambda i: jnp.where(incl, pb[i][C:2 * C], 0.0).astype(BF16))
    t = ew(lambda i: eye + a_ab[i])
    pw = ew(lambda i: _mm(a_ab[i], bdiag(a_ab[i])))
    for level in range(5):
        pw[0] = tie(pw[0])
        pw_bd = ew(lambda i: bdiag(pw[i]))
        if level < 4:
            both = ew(lambda i: _mm(jnp.concatenate([t[i], pw[i]], axis=0), pw_bd[i]))
            t = ew(lambda i: t[i] + both[i][0:C])
            pw = ew(lambda i: both[i][C:2 * C])
        else:
            t = ew(lambda i: t[i] + _mm(t[i], pw_bd[i]))
    for _ in previous:
        pass
    v_bd = ew(lambda i: bdiag(vc[i]))
    av_both = ew(lambda i: _mm(jnp.concatenate([a_ak[i], a_rk[i]], axis=0), v_bd[i]))
    wt = ew(lambda i: _mm(t[i], bdiag(at[i])).astype(BF16))
    u0 = ew(lambda i: _mm(t[i], bdiag(av_both[i][0:C])))
    y0 = ew(lambda i: av_both[i][C:2 * C])
    wr = ew(lambda i: jnp.concatenate([wt[i], rt[i]], axis=0))
    e_rem = ew(lambda i: jnp.exp(cs[i][C - 1:C, :] - cs[i]))
    kb = ew(lambda i: jnp.concatenate([kc[i] * e_rem[i], bc[i] * e_rem[i]], axis=0).astype(BF16))

    zwide = _ordering_zero(tokens[-1] + tokens[-2], zero_ref, p_ref.shape[-1])[0:1, :]
    zrow = zwide[:, 0:W]
    zrow_b = zrow.astype(BF16)
    for bi in range(bsz):
        prev_ref[bi] = prev_new[bi] + zwide
        if has_vres:
            prevm_ref[bi] = prevm_new[bi] + zwide[:, 0:LANE]
        r_s[bi] = rv[bi] + zrow
        k_s[bi] = kv[bi] + zrow
        v_s[bi] = vv[bi] + zrow
        g_s[bi] = gv[bi] + zrow
    for i in range(len(chains)):
        wr_s[i] = wr[i] + zrow_b
        kb_s[i] = kb[i] + zrow_b
        arb_s[i] = a_rb[i] + zrow_b
        u0_s[i] = u0[i] + zrow
        y0_s[i] = y0[i] + zrow
        dec_s[i] = jnp.exp(cs[i][C - 1:C, :]) + zrow


def _rwkv(p, misc, v_first, mu, mu_vres, w0, w_up, a0, a_up, g_up, k_k, k_a, r_k, gn_g, gn_b, v0, v_up):
    bsz, T, width = p.shape
    has_vres = v_first is not None
    row = lambda x: x.reshape(1, -1)
    nt = T // RW_TB
    nch = bsz * (RW_TB // RW_C)
    blk = lambda w: pl.BlockSpec((bsz, RW_TB, w), lambda t: (0, jnp.minimum(t, nt - 1), 0))
    done = lambda w: pl.BlockSpec((bsz, RW_TB, w), lambda t: (0, jnp.maximum(t - 1, 0), 0))
    vec = _const_spec((1, RW_W))
    args = [p]
    specs = [blk(width)]
    if has_vres:
        args += [misc, v_first]
        specs += [blk(LANE), blk(RW_W)]
    args.append(row(mu))
    specs.append(_const_spec((1, width)))
    if has_vres:
        args.append(_at_lanes(mu_vres, MISC_VRES))
        specs.append(_const_spec((1, LANE)))
    args += [row(w0), _at_rows(w_up, 0), row(a0), _at_rows(a_up, RW_LOW), g_up, row(k_k), row(k_a),
             row(r_k), row(gn_g), row(gn_b)]
    specs += [vec, _const_spec((LANE, RW_W)), vec, _const_spec((LANE, RW_W)),
              _const_spec((LANE, RW_W)), vec, vec, vec, vec, vec]
    if has_vres:
        args += [row(v0), _at_rows(v_up, MISC_VRES)]
        specs += [vec, _const_spec((LANE, RW_W))]
    args.append(jnp.zeros((8, LANE), jnp.uint32))
    specs.append(_const_spec((8, LANE)))
    if has_vres:
        out_shape = jax.ShapeDtypeStruct((bsz, T, RW_W), BF16)
        out_specs = done(RW_W)
    else:
        out_shape = [jax.ShapeDtypeStruct((bsz, T, RW_W), BF16),
                     jax.ShapeDtypeStruct((bsz, T + RW_TB, RW_W), F32)]
        out_specs = [done(RW_W), pl.BlockSpec((bsz, RW_TB, RW_W), lambda t: (0, t, 0))]
    seq = pltpu.VMEM((bsz, RW_TB, RW_W), F32)
    per_chain = lambda rows, dtype: pltpu.VMEM((nch, rows, RW_W), dtype)
    return pl.pallas_call(
        functools.partial(_rwkv_kernel, has_vres=has_vres, bsz=bsz),
        out_shape=out_shape,
        grid=(nt + 1,),
        in_specs=specs,
        out_specs=out_specs,
        scratch_shapes=[pltpu.VMEM((bsz, RW_W, RW_W), F32), pltpu.VMEM((bsz, 1, width), F32)]
        + ([pltpu.VMEM((bsz, 1, LANE), F32)] if has_vres else []) + [seq] * 4
        + [per_chain(2 * RW_C, BF16), per_chain(2 * RW_C, BF16), per_chain(RW_C, BF16),
           per_chain(RW_C, F32), per_chain(RW_C, F32), per_chain(1, F32)],
        compiler_params=pltpu.CompilerParams(dimension_semantics=("arbitrary",),
                                             vmem_limit_bytes=VMEM_LIMIT),
        name="rwkv7_mixer",
    )(*args)


def _mamba_kernel(p_ref, m_ref, cw_ref, cb_ref, dtb_ref, alog_ref, d_ref, ng_ref, o_ref, st_ref, xpad_ref, *, bsz):
    L, TB = MB_L, MB_TB

    @pl.when(pl.program_id(0) == 0)
    def _():
        st_ref[...] = jnp.zeros_like(st_ref)
        xpad_ref[:, 0:8, :] = jnp.zeros((bsz, 8, MB_CONV_DIM), F32)

    tril = _iota((L, L), 0) >= _iota((L, L), 1)
    tril_b = tril.astype(BF16)
    expand = (_iota((LANE, MB_DI), 0) == MISC_DT + (_iota((LANE, MB_DI), 1) // MB_P)).astype(BF16)
    lane_lo = _iota((1, LANE), 1) < MB_P
    gw = MB_DI // MB_G
    a_neg = -jnp.exp(alog_ref[...])

    xs, bms, cms, dts = [], [], [], []
    for bi in range(bsz):
        xpad_ref[bi, 8:8 + TB, :] = p_ref[bi, :, 512:1536]
        conv = cb_ref[...] + xpad_ref[bi, 8:8 + TB, :] * cw_ref[MB_CONV - 1:MB_CONV, :]
        for s in range(1, MB_CONV):
            conv = conv + xpad_ref[bi, 8 - s:8 - s + TB, :] * cw_ref[MB_CONV - 1 - s:MB_CONV - s, :]
        xpad_ref[bi, 0:8, :] = xpad_ref[bi, TB:TB + 8, :]
        xbc = _silu(conv)
        xs.append(xbc[:, 0:512])
        bms.append(xbc[:, 512:768])
        cms.append(xbc[:, 768:1024])
        dts.append(_softplus(m_ref[bi] + dtb_ref[...]))

    chains = [(bi, c) for c in range(TB // L) for bi in range(bsz)]
    pre = []
    for bi, c in chains:
        sl = slice(c * L, (c + 1) * L)
        dtc = dts[bi][sl]
        cs = _mm_sel_r(tril_b, dtc * a_neg, 3)
        cs_t = cs.T
        cs_x = _mm_sel_l(cs, expand, 3)
        dt_x = _mm_sel_l(dtc, expand, 2)
        cs_last = cs_x[L - 1:L, :]
        xc = xs[bi][sl]
        xdt = xc * dt_x
        xdec = xdt * jnp.exp(cs_last - cs_x)
        y_parts, upd, cgs = [], [], []
        for g in range(MB_G):
            bg = bms[bi][sl, g * MB_N:(g + 1) * MB_N]
            cg = cms[bi][sl, g * MB_N:(g + 1) * MB_N]
            cbm = _mm_nt(cg, bg)
            for hp in range(2):
                pair = g * 2 + hp
                xp = xdt[:, pair * LANE:(pair + 1) * LANE]
                lhs = []
                for e in range(2):
                    h = pair * 2 + e
                    lane = MISC_DT + h
                    diff = cs[:, lane:lane + 1] - cs_t[lane:lane + 1, :]
                    lhs.append((cbm * jnp.exp(jnp.where(tril, diff, -1e30))).astype(BF16))
                yh = _mm(jnp.concatenate(lhs, axis=0), xp)
                y_parts.append(jnp.where(lane_lo, yh[0:L], yh[L:2 * L]))
            upd.append(_mm_tn(bg, xdec[:, g * gw:(g + 1) * gw]))
            cgs.append(cg.astype(BF16))
        pre.append(dict(y=jnp.concatenate(y_parts, axis=1) + d_ref[...] * xc,
                        upd=jnp.concatenate(upd, axis=1), cg=cgs, e_in=jnp.exp(cs_x),
                        dec=jnp.exp(cs_last)))

    st = [st_ref[bi] for bi in range(bsz)]
    ys = [[] for _ in range(bsz)]
    for (bi, c), q in zip(chains, pre):
        stb = st[bi].astype(BF16)
        y_off = jnp.concatenate([_mm(q["cg"][g], stb[:, g * gw:(g + 1) * gw]) for g in range(MB_G)], axis=1)
        ys[bi].append(q["y"] + y_off * q["e_in"])
        st[bi] = st[bi] * q["dec"] + q["upd"]
    for bi in range(bsz):
        st_ref[bi] = st[bi]
        y = jnp.concatenate(ys[bi], axis=0) * _silu(p_ref[bi, :, 0:512])
        outs = []
        for g in range(MB_G):
            yg = y[:, g * gw:(g + 1) * gw]
            outs.append(yg * lax.rsqrt(jnp.mean(yg * yg, axis=-1, keepdims=True) + LN_EPS))
        o_ref[bi] = (jnp.concatenate(outs, axis=1) * ng_ref[...]).astype(BF16)


def _mamba(p, misc, conv_w, conv_b, dt_bias, a_log, d, norm_g):
    bsz, T, _ = p.shape
    pad = lambda vec: _at_lanes(vec, MISC_DT)
    return pl.pallas_call(
        functools.partial(_mamba_kernel, bsz=bsz),
        out_shape=jax.ShapeDtypeStruct((bsz, T, MB_DI), BF16),
        grid=(T // MB_TB,),
        in_specs=[pl.BlockSpec((bsz, MB_TB, MB_PAD), lambda t: (0, t, 0)),
                  pl.BlockSpec((bsz, MB_TB, LANE), lambda t: (0, t, 0)),
                  _const_spec((MB_CONV, MB_CONV_DIM)), _const_spec((1, MB_CONV_DIM)),
                  _const_spec((1, LANE)), _const_spec((1, LANE)), _const_spec((1, MB_DI)),
                  _const_spec((1, MB_DI))],
        out_specs=pl.BlockSpec((bsz, MB_TB, MB_DI), lambda t: (0, t, 0)),
        scratch_shapes=[pltpu.VMEM((bsz, MB_N, MB_DI), F32),
                        pltpu.VMEM((bsz, MB_TB + 8, MB_CONV_DIM), F32)],
        compiler_params=pltpu.CompilerParams(dimension_semantics=("arbitrary",),
                                             vmem_limit_bytes=VMEM_LIMIT),
        name="mamba2_mixer",
    )(p, misc, conv_w, conv_b.reshape(1, -1), pad(dt_bias), pad(a_log),
      jnp.repeat(d, MB_P).reshape(1, -1), norm_g.reshape(1, -1))


def _pad_cols(w, width):
    return jnp.pad(w, ((0, 0), (0, width - w.shape[1])))


def _at_lanes(vec, start, width=LANE):
    return jnp.zeros((1, width), F32).at[0, start:start + vec.shape[0]].set(vec)


def _at_rows(m, start):
    return jnp.zeros((LANE, m.shape[1]), F32).at[start:start + m.shape[0]].set(m)


def _proj_weights(w_in_l, w_vres_l):
    g = w_in_l[:, :GLA_COLS]
    gla = jnp.concatenate([g[:, 0:512], g[:, 528:784]], axis=1)
    rw = w_in_l[:, GLA_COLS:GLA_COLS + RW_COLS]
    m = w_in_l[:, GLA_COLS + RW_COLS:N_IN]
    misc = [g[:, 512:528], m[:, MB_PAD:MB_COLS]]
    if w_vres_l is not None:
        misc.append(w_vres_l)
    misc = _pad_cols(jnp.concatenate(misc, axis=1), LANE)
    return gla.astype(BF16), rw.astype(BF16), m[:, :MB_PAD].astype(BF16), misc.astype(BF16)


def kernel(x, ln_g, ln_b, ffn_w_in, ffn_w_down, w_in, w_in_vres, w_out, gla_gk_up, gla_gk_bias, gla_norm_g, rw_mu, rw_mu_vres, rw_w0, rw_w_up, rw_a0, rw_a_up, rw_g_up, rw_k_k, rw_k_a, rw_r_k, rw_gn_g, rw_gn_b, rw_v0, rw_v_up, mb_conv_w, mb_conv_b, mb_dt_bias, mb_A_log, mb_D, mb_norm_g):
    bsz, T, _ = x.shape
    n = bsz * T
    x2 = x.reshape(n, D_MODEL)
    v_first = None
    ffn_in, ffn_down = _ffn_weights(ffn_w_in, ffn_w_down)
    for l in range(DEPTH):
        x2 = _ffn_ln(x2, ffn_in, ffn_down, l, 0, ln_g[l, 0], ln_b[l, 0])
        vres = l > 0
        weights = _proj_weights(w_in[l], w_in_vres[l - 1] if vres else None)
        gla_p, rw_p, mb_p, misc = (t.reshape(bsz, T, -1) for t in _in_proj(x2, weights))
        gla_o = _gla(gla_p, misc, gla_gk_up[l], gla_gk_bias[l], gla_norm_g[l])
        rw_args = (rw_p, misc, v_first, rw_mu[l], rw_mu_vres[l - 1] if vres else None, rw_w0[l], rw_w_up[l],
                   rw_a0[l], rw_a_up[l], rw_g_up[l], rw_k_k[l], rw_k_a[l], rw_r_k[l].reshape(-1),
                   rw_gn_g[l], rw_gn_b[l])
        if vres:
            rw_o = _rwkv(*rw_args, rw_v0[l - 1], rw_v_up[l - 1])
        else:
            rw_o, v_first = _rwkv(*rw_args, None, None)
        mb_o = _mamba(mb_p, misc, mb_conv_w[l], mb_conv_b[l], mb_dt_bias[l], mb_A_log[l],
                      mb_D[l], mb_norm_g[l])
        x2 = _out_proj_ln(x2, gla_o.reshape(n, -1), rw_o.reshape(n, -1), mb_o.reshape(n, -1),
                          w_out[l], ln_g[l, 1], ln_b[l, 1])
        x2 = _ffn_ln(x2, ffn_in, ffn_down, l, 1, ln_g[l, 2], ln_b[l, 2])
    return x2.reshape(bsz, T, D_MODEL)
```

```python
import functools

import jax
import jax.numpy as jnp
from jax import lax
from jax.experimental import pallas as pl
from jax.experimental.pallas import tpu as pltpu

F32 = jnp.float32
BF16 = jnp.bfloat16

D_MODEL = 1024
DEPTH = 2
D_FF = 2816
LN_EPS = 1e-5
DN_ALPHA = (2 * DEPTH) ** 0.25

GLA_H, GLA_DK, GLA_DV = 4, 32, 64
GLA_GATE_RANK = 16
GLA_GATE_NORM = 16.0
GLA_QK = GLA_H * GLA_DK
GLA_W = GLA_H * GLA_DV
GLA_COLS = 784
GLA_PAD = 768

RW_H, RW_DH = 4, 64
RW_W = 256
RW_COLS = 1024
RW_V_LORA = 32
RW_GN_EPS = 64e-5
RW_LOW = 64

MB_H, MB_P, MB_DI, MB_G, MB_N = 8, 64, 512, 2, 128
MB_CONV = 4
MB_CONV_DIM = 1024
MB_COLS = 1544
MB_PAD = 1536
N_IN = GLA_COLS + RW_COLS + MB_COLS
MISC_GL = 0
MISC_DT = MISC_GL + GLA_GATE_RANK
MISC_VRES = MISC_DT + MB_H

LANE = 128
VMEM_LIMIT = 56 * 1024 * 1024

FFN_TM = 512
FFN_FC = 256
FFN_LN_ROWS = 32
PROJ_TM = 1024
OUT_TM = 1024
GLA_TB, GLA_C = 128, 64
RW_TB, RW_C = 256, 64
MB_TB, MB_L = 512, 128


def _mm(a, b):
    return jnp.dot(a.astype(BF16), b.astype(BF16), preferred_element_type=F32)


def _mm_nt(a, b):
    return lax.dot_general(a.astype(BF16), b.astype(BF16), (((1,), (1,)), ((), ())),
                           preferred_element_type=F32)


def _mm_tn(a, b):
    return lax.dot_general(a.astype(BF16), b.astype(BF16), (((0,), (0,)), ((), ())),
                           preferred_element_type=F32)


def _split(x, parts):
    out = []
    for _ in range(parts - 1):
        piece = x.astype(BF16)
        out.append(piece)
        x = x - piece.astype(F32)
    out.append(x.astype(BF16))
    return out


def _mm_sel_l(x, sel, parts, stack=False):
    rows = x.shape[0]
    if not stack:
        acc = None
        for piece in _split(x, parts):
            term = jnp.dot(piece, sel, preferred_element_type=F32)
            acc = term if acc is None else acc + term
        return acc
    prod = jnp.dot(jnp.concatenate(_split(x, parts), axis=0), sel, preferred_element_type=F32)
    acc = prod[0:rows]
    for p in range(1, parts):
        acc = acc + prod[p * rows:(p + 1) * rows]
    return acc


def _mm_sel_r(sel, x, parts):
    acc = None
    for piece in _split(x, parts):
        term = jnp.dot(sel, piece, preferred_element_type=F32)
        acc = term if acc is None else acc + term
    return acc


def _layer_norm(y, g, b):
    mu = jnp.mean(y, axis=-1, keepdims=True)
    d = y - mu
    var = jnp.mean(d * d, axis=-1, keepdims=True)
    return d * lax.rsqrt(var + LN_EPS) * g + b


def _silu(x):
    return x * jax.nn.sigmoid(x)


def _softplus(x):
    return jnp.maximum(x, 0.0) + jnp.log(1.0 + jnp.exp(-jnp.abs(x)))


def _ordering_zero(v, zero_ref, width):
    z = pltpu.bitcast(pltpu.bitcast(v, jnp.uint32) & zero_ref[...], F32)
    return jnp.concatenate([z] * (width // LANE), axis=1)


def _iota(shape, dim):
    return lax.broadcasted_iota(jnp.int32, shape, dim)


def _const_spec(shape):
    n = len(shape)
    return pl.BlockSpec(shape, lambda *_: (0,) * n, pipeline_mode=pl.Buffered(1))


def _ffn_ln_kernel(x_ref, xp_ref, win_ref, wd_ref, g_ref, b_ref, zero_ref, o_ref, acc_ref, accp_ref, *,
                   n_chunks, n_tiles):
    step = pl.program_id(0)

    @pl.when(step == 0)
    def _():
        accp_ref[...] = jnp.zeros_like(accp_ref)

    def finish_previous():
        token = None
        for r in range(0, FFN_TM, FFN_LN_ROWS):
            rows = pl.ds(r, FFN_LN_ROWS)
            out = _layer_norm(DN_ALPHA * xp_ref[rows, :] + accp_ref[rows, :], g_ref[...], b_ref[...])
            o_ref[rows, :] = out
            tiles = [out[i:i + 8, j:j + LANE] for i in range(0, FFN_LN_ROWS, 8) for j in range(0, D_MODEL, LANE)]
            if token is not None:
                tiles.append(token)
            token = functools.reduce(lambda a, b: a + b, tiles)
            yield token

    @pl.when(step < n_tiles)
    def _():
        previous = finish_previous()
        per_chunk = -(-(FFN_TM // FFN_LN_ROWS) // (n_chunks - 1))
        xb = x_ref[...].astype(BF16)
        pin = None
        for c in range(n_chunks):
            gate = jnp.dot(xb, win_ref[:, c * FFN_FC:(c + 1) * FFN_FC], preferred_element_type=F32)
            up = jnp.dot(xb, win_ref[:, D_FF + c * FFN_FC:D_FF + (c + 1) * FFN_FC], preferred_element_type=F32)
            act = _silu(gate) * up
            if pin is not None:
                act = jnp.concatenate([act[0:8, :] + _ordering_zero(pin, zero_ref, FFN_FC), act[8:, :]], axis=0)
            part = jnp.dot(act.astype(BF16), wd_ref[c], preferred_element_type=F32)
            if c == 0:
                acc_ref[...] = part
            elif c < n_chunks - 1:
                acc_ref[...] += part
            else:
                for pin in previous:
                    pass
                accp_ref[...] = acc_ref[...] + part + _ordering_zero(pin, zero_ref, D_MODEL)[0:1, :]
            for _ in range(per_chunk):
                pin = next(previous, pin)

    @pl.when(step == n_tiles)
    def _():
        for _ in finish_previous():
            pass


def _ffn_weights(ffn_w_in, ffn_w_down):
    nc = D_FF // FFN_FC
    wd = (0.5 * ffn_w_down).astype(BF16).reshape(DEPTH, 2, nc, FFN_FC, D_MODEL)
    return ffn_w_in.astype(BF16), wd


def _ffn_ln(x2, w_in_all, wd_all, layer, half, g, b):
    n = x2.shape[0]
    nc = D_FF // FFN_FC
    nt = n // FFN_TM
    pick = lambda shape: pl.BlockSpec((None, None) + shape, lambda i: (layer, half) + (0,) * len(shape),
                                      pipeline_mode=pl.Buffered(1))
    cur = pl.BlockSpec((FFN_TM, D_MODEL), lambda i: (jnp.minimum(i, nt - 1), 0))
    done = pl.BlockSpec((FFN_TM, D_MODEL), lambda i: (jnp.maximum(i - 1, 0), 0))
    return pl.pallas_call(
        functools.partial(_ffn_ln_kernel, n_chunks=nc, n_tiles=nt),
        out_shape=jax.ShapeDtypeStruct((n, D_MODEL), F32),
        grid=(nt + 1,),
        in_specs=[
            cur,
            done,
            pick((D_MODEL, 2 * D_FF)),
            pick((nc, FFN_FC, D_MODEL)),
            _const_spec((1, D_MODEL)),
            _const_spec((1, D_MODEL)),
            _const_spec((8, LANE)),
        ],
        out_specs=done,
        scratch_shapes=[pltpu.VMEM((FFN_TM, D_MODEL), F32), pltpu.VMEM((FFN_TM, D_MODEL), F32)],
        compiler_params=pltpu.CompilerParams(dimension_semantics=("arbitrary",),
                                             vmem_limit_bytes=VMEM_LIMIT),
        name="ffn_ln",
    )(x2, x2, w_in_all, wd_all, g.reshape(1, -1), b.reshape(1, -1), jnp.zeros((8, LANE), jnp.uint32))


def _in_proj_kernel(x_ref, *refs):
    xb = x_ref[...].astype(BF16)
    n_groups = len(refs) // 2
    for w_ref, o_ref in zip(refs[:n_groups], refs[n_groups:]):
        o_ref[...] = jnp.dot(xb, w_ref[...], preferred_element_type=F32)


def _in_proj(x2, weights):
    n = x2.shape[0]
    widths = [w.shape[1] for w in weights]
    return pl.pallas_call(
        _in_proj_kernel,
        out_shape=[jax.ShapeDtypeStruct((n, w), F32) for w in widths],
        grid=(n // PROJ_TM,),
        in_specs=[pl.BlockSpec((PROJ_TM, D_MODEL), lambda i: (i, 0))]
        + [_const_spec((D_MODEL, w)) for w in widths],
        out_specs=[pl.BlockSpec((PROJ_TM, w), lambda i: (i, 0)) for w in widths],
        compiler_params=pltpu.CompilerParams(dimension_semantics=("parallel",),
                                             vmem_limit_bytes=VMEM_LIMIT),
        name="in_proj",
    )(x2, *weights)


def _out_proj_ln_kernel(x_ref, ma_ref, mb_ref, mc_ref, wa_ref, wb_ref, wc_ref, g_ref, b_ref, o_ref):
    acc = jnp.dot(ma_ref[...], wa_ref[...], preferred_element_type=F32)
    acc += jnp.dot(mb_ref[...], wb_ref[...], preferred_element_type=F32)
    acc += jnp.dot(mc_ref[...], wc_ref[...], preferred_element_type=F32)
    o_ref[...] = acc
    for r in range(0, OUT_TM, FFN_LN_ROWS):
        rows = pl.ds(r, FFN_LN_ROWS)
        o_ref[rows, :] = _layer_norm(DN_ALPHA * x_ref[rows, :] + o_ref[rows, :], g_ref[...], b_ref[...])


def _out_proj_ln(x2, ma, mb, mc, w_out, g, b):
    n = x2.shape[0]
    wo = w_out.astype(BF16)
    wa, wb, wc = wo[:GLA_W], wo[GLA_W:GLA_W + RW_W], wo[GLA_W + RW_W:]
    tm = OUT_TM
    row = lambda w: pl.BlockSpec((tm, w), lambda i: (i, 0))
    return pl.pallas_call(
        _out_proj_ln_kernel,
        out_shape=jax.ShapeDtypeStruct((n, D_MODEL), F32),
        grid=(n // tm,),
        in_specs=[row(D_MODEL), row(GLA_W), row(RW_W), row(MB_DI),
                  _const_spec((GLA_W, D_MODEL)), _const_spec((RW_W, D_MODEL)),
                  _const_spec((MB_DI, D_MODEL)), _const_spec((1, D_MODEL)), _const_spec((1, D_MODEL))],
        out_specs=row(D_MODEL),
        compiler_params=pltpu.CompilerParams(dimension_semantics=("parallel",),
                                             vmem_limit_bytes=VMEM_LIMIT),
        name="out_proj_ln",
    )(x2, ma, mb, mc, wa, wb, wc, g.reshape(1, -1), b.reshape(1, -1))


def _gla_kernel(p_ref, m_ref, gkup_ref, gkb_ref, ng_ref, o_ref, st_ref, *, bsz):
    C = GLA_C

    @pl.when(pl.program_id(0) == 0)
    def _():
        st_ref[...] = jnp.zeros_like(st_ref)

    TB = GLA_TB
    tril_blk = (((_iota((TB, TB), 0) // C) == (_iota((TB, TB), 1) // C))
                & (_iota((TB, TB), 0) >= _iota((TB, TB), 1))).astype(BF16)
    tril4 = (_iota((GLA_H * C, C), 0) % C) >= _iota((GLA_H * C, C), 1)
    qhead = _iota((1, GLA_QK), 1) // GLA_DK
    vhead = _iota((1, GLA_W), 1) // GLA_DV
    smask = (_iota((GLA_W, GLA_QK), 0) // GLA_DV) == (_iota((GLA_W, GLA_QK), 1) // GLA_DK)
    gmat = ((_iota((GLA_W, GLA_W), 0) // GLA_DV) == (_iota((GLA_W, GLA_W), 1) // GLA_DV)).astype(BF16) / GLA_DV

    eb = lambda f: [f(bi) for bi in range(bsz)]
    z = eb(lambda bi: _mm(m_ref[bi], gkup_ref[...]) + gkb_ref[...])
    gk = eb(lambda bi: -_softplus(-z[bi]) / GLA_GATE_NORM)
    b_all = eb(lambda bi: _mm_sel_r(tril_blk, gk[bi], 3))
    qe_all = eb(lambda bi: p_ref[bi, :, 0:128] * (GLA_DK ** -0.5) * jnp.exp(b_all[bi]))
    ke_all = eb(lambda bi: p_ref[bi, :, 128:256] * jnp.exp(-b_all[bi]))

    n = TB // C
    chains = [(bi, c) for c in range(n) for bi in range(bsz)]
    ew = lambda f: [f(bi, slice(c * C, (c + 1) * C)) for bi, c in chains]
    b_last = ew(lambda bi, s: b_all[bi][s.stop - 1:s.stop, :])
    v = ew(lambda bi, s: p_ref[bi, s, 256:512])
    q_e = ew(lambda bi, s: qe_all[bi][s])
    k_e = ew(lambda bi, s: ke_all[bi][s])
    kdec = ew(lambda bi, s: p_ref[bi, s, 128:256] * jnp.exp(b_all[bi][s.stop - 1:s.stop, :] - b_all[bi][s]))
    ei = lambda f: [f(i) for i in range(len(chains))]
    q_st = ei(lambda i: jnp.concatenate([jnp.where(qhead == h, q_e[i], 0.0) for h in range(GLA_H)], axis=0))
    att = ei(lambda i: jnp.where(tril4, _mm_nt(q_st[i], k_e[i]), 0.0))
    r = ei(lambda i: _mm(att[i], v[i]))
    upd = ei(lambda i: jnp.where(smask, _mm_tn(v[i], kdec[i]), 0.0))

    st = [st_ref[bi] for bi in range(bsz)]
    outs = [[] for _ in range(bsz)]
    for i, (bi, c) in enumerate(chains):
        o = _mm_nt(q_e[i], st[bi])
        for h in range(GLA_H):
            o = o + jnp.where(vhead == h, r[i][h * C:(h + 1) * C], 0.0)
        outs[bi].append(o)
        st[bi] = st[bi] * jnp.exp(b_last[i]) + upd[i]
    for bi in range(bsz):
        st_ref[bi] = st[bi]
        o = jnp.concatenate(outs[bi], axis=0)
        ms = _mm_sel_l(o * o, gmat, 2)
        o = o * lax.rsqrt(ms + LN_EPS) * ng_ref[...]
        o_ref[bi] = (o * _silu(p_ref[bi, :, 512:768])).astype(BF16)


def _gla(p, misc, gk_up, gk_bias, norm_g):
    bsz, T, _ = p.shape
    gkup = _at_rows(gk_up, MISC_GL)
    return pl.pallas_call(
        functools.partial(_gla_kernel, bsz=bsz),
        out_shape=jax.ShapeDtypeStruct((bsz, T, GLA_W), BF16),
        grid=(T // GLA_TB,),
        in_specs=[pl.BlockSpec((bsz, GLA_TB, GLA_PAD), lambda t: (0, t, 0)),
                  pl.BlockSpec((bsz, GLA_TB, LANE), lambda t: (0, t, 0)),
                  _const_spec((LANE, GLA_QK)), _const_spec((1, GLA_QK)), _const_spec((1, GLA_W))],
        out_specs=pl.BlockSpec((bsz, GLA_TB, GLA_W), lambda t: (0, t, 0)),
        scratch_shapes=[pltpu.VMEM((bsz, GLA_W, GLA_QK), F32)],
        compiler_params=pltpu.CompilerParams(dimension_semantics=("arbitrary",),
                                             vmem_limit_bytes=VMEM_LIMIT),
        name="gla_mixer",
    )(p, misc, gkup, gk_bias.reshape(1, -1), jnp.tile(norm_g, GLA_H).reshape(1, -1))


def _rwkv_kernel(*refs, has_vres, bsz):
    if has_vres:
        (p_ref, m_ref, vf_ref, mu_ref, mum_ref, w0_ref, wup_ref, a0_ref, aup_ref, gup_ref, kk_ref, ka_ref, rk_ref,
         gng_ref, gnb_ref, v0_ref, vup_ref, zero_ref, o_ref,
         ht_ref, prev_ref, prevm_ref, r_s, k_s, v_s, g_s, wr_s, kb_s, arb_s, u0_s, y0_s, dec_s) = refs
        carried = (prevm_ref,)
    else:
        (p_ref, mu_ref, w0_ref, wup_ref, a0_ref, aup_ref, gup_ref, kk_ref, ka_ref, rk_ref,
         gng_ref, gnb_ref, zero_ref, o_ref, vf_out_ref,
         ht_ref, prev_ref, r_s, k_s, v_s, g_s, wr_s, kb_s, arb_s, u0_s, y0_s, dec_s) = refs
        carried = ()
    C, TB, W, H = RW_C, RW_TB, RW_W, RW_H
    carried += (ht_ref, prev_ref, r_s, k_s, v_s, g_s, wr_s, kb_s, arb_s, u0_s, y0_s, dec_s)

    @pl.when(pl.program_id(0) == 0)
    def _():
        for ref in carried:
            ref[...] = jnp.zeros_like(ref)

    head_eq = (_iota((W, W), 0) // RW_DH) == (_iota((W, W), 1) // RW_DH)
    gsum = head_eq.astype(BF16)
    gmean = (head_eq.astype(F32) / RW_DH).astype(BF16)
    tril_blk = (((_iota((TB, TB), 0) // C) == (_iota((TB, TB), 1) // C))
                & (_iota((TB, TB), 0) >= _iota((TB, TB), 1))).astype(BF16)
    ri = _iota((C, W), 0)
    li = _iota((C, W), 1) % C
    strict = ri > li
    incl = ri >= li
    eye = (ri == li).astype(F32)
    blk_eq = (_iota((H * C, W), 0) // C) == (_iota((H * C, W), 1) // RW_DH)

    def bdiag(m):
        return jnp.where(blk_eq, jnp.concatenate([m] * H, axis=0), 0.0).astype(BF16)

    chains = [(bi, c) for c in range(TB // C) for bi in range(bsz)]
    ew = lambda f: [f(i) for i in range(len(chains))]
    sl = [(bi, slice(c * C, (c + 1) * C)) for bi, c in chains]

    def finish_previous():
        ht = [ht_ref[bi] for bi in range(bsz)]
        ys = [[] for _ in range(bsz)]
        for i, (bi, c) in enumerate(chains):
            sh = _mm_nt(wr_s[i], ht[bi])
            u = sh[0:C] + u0_s[i]
            ys[bi].append(sh[C:2 * C] + y0_s[i] + _mm(arb_s[i], bdiag(u)))
            upd = _mm_tn(jnp.concatenate([v_s[sl[i]], u], axis=0), kb_s[i])
            ht[bi] = ht[bi] * dec_s[i] + jnp.where(head_eq, upd, 0.0)
            yield ht[bi][0:8, 0:LANE]
        for bi in range(bsz):
            ht_ref[bi] = ht[bi]
            y = jnp.concatenate(ys[bi], axis=0)
            mean = _mm_sel_l(y, gmean, 2)
            d = y - mean
            var = _mm_sel_l(d * d, gmean, 2)
            y = d * lax.rsqrt(var + RW_GN_EPS) * gng_ref[...] + gnb_ref[...]
            bonus = _mm_sel_l(r_s[bi] * k_s[bi] * rk_ref[...], gsum, 2) * v_s[bi]
            out = (y + bonus) * g_s[bi]
            o_ref[bi] = out.astype(BF16)
            tiles = [out[i:i + 8, j:j + LANE] for i in range(0, TB, 8) for j in range(0, W, LANE)]
            yield functools.reduce(lambda a, b: a + b, tiles)

    previous = finish_previous()

    tokens = []
    sites = [0]

    def tie(x, lag=2):
        token = next(previous, None)
        if token is not None:
            tokens.append(token)
        at_site = sites[0] - lag
        sites[0] += 1
        if not 0 <= at_site < len(tokens):
            return x
        top = x[0:8, 0:LANE] + _ordering_zero(tokens[at_site], zero_ref, LANE)
        if x.shape[1] > LANE:
            top = jnp.concatenate([top, x[0:8, LANE:]], axis=1)
        return jnp.concatenate([top, x[8:]], axis=0)

    rv, kv, vv, gv, av, bv, lwv, csv = [], [], [], [], [], [], [], []
    prev_new, prevm_new = [], []
    for bi in range(bsz):
        p = p_ref[bi]
        shifted = jnp.where(_iota(p.shape, 0) == 0, prev_ref[bi], pltpu.roll(p, 1, 0))
        prev_new.append(p[TB - 1:TB, :])
        xs = p + (shifted - p) * mu_ref[...]
        r = xs[:, 0:256]
        k = xs[:, 256:512]
        v = xs[:, 512:768]
        wa_low = xs[:, 768:896]
        g_low = xs[:, 896:1024]
        if has_vres:
            m = m_ref[bi]
            shifted_m = jnp.where(_iota(m.shape, 0) == 0, prevm_ref[bi], pltpu.roll(m, 1, 0))
            prevm_new.append(m[TB - 1:TB, :])
            v_low = m + (shifted_m - m) * mum_ref[...]
            v = v + (vf_ref[bi] - v) * jax.nn.sigmoid(v0_ref[...] + _mm(v_low, vup_ref[...]))
        else:
            vf_out_ref[bi] = v
        w_low = a_low = wa_low
        w = -_softplus(-(w0_ref[...] + _mm(jnp.tanh(w_low), wup_ref[...]))) - 0.5
        lw = tie(-jnp.exp(w))
        lwv.append(lw)
        csv.append(_mm_sel_r(tril_blk, lw, 3))
        a = jax.nn.sigmoid(a0_ref[...] + _mm(a_low, aup_ref[...]))
        gv.append(_mm(jax.nn.sigmoid(g_low), gup_ref[...]))
        kk = k * kk_ref[...]
        kk = tie(kk / jnp.maximum(jnp.sqrt(_mm_sel_l(kk * kk, gsum, 2)), 1e-12))
        rv.append(r)
        kv.append(tie(k * (1.0 + (a - 1.0) * ka_ref[...])))
        vv.append(v)
        av.append(-kk)
        bv.append(kk * a)

    cut = lambda vals: ew(lambda i: vals[sl[i][0]][sl[i][1]])
    cs = cut(csv)
    kc = cut(kv)
    bc = cut(bv)
    vc = cut(vv)
    ac = cut(av)
    lwc = cut(lwv)
    rc = cut(rv)
    e_neg = ew(lambda i: jnp.exp(-cs[i]))
    e_neg[0] = tie(e_neg[0])
    at = ew(lambda i: ac[i] * jnp.exp(cs[i] - lwc[i]))
    rt = ew(lambda i: (rc[i] * jnp.exp(cs[i])).astype(BF16))
    ar = ew(lambda i: jnp.concatenate([at[i].astype(BF16), rt[i]], axis=0))
    pk = ew(lambda i: _mm_nt(ar[i], bdiag(kc[i] * e_neg[i])))
    pb = ew(lambda i: _mm_nt(ar[i], bdiag(bc[i] * e_neg[i])))
    a_ak = ew(lambda i: jnp.where(strict, pk[i][0:C], 0.0))
    a_rk = ew(lambda i: jnp.where(incl, pk[i][C:2 * C], 0.0))
    a_ab = ew(lambda i: jnp.where(strict, pb[i][0:C], 0.0))
    a_ab[0] = tie(a_ab[0])
    a_rb = ew(lambda i: jnp.where(incl, pb[i][C:2 * C], 0.0).astype(BF16))
    t = ew(lambda i: eye + a_ab[i])
    pw = ew(lambda i: _mm(a_ab[i], bdiag(a_ab[i])))
    for level in range(5):
        pw[0] = tie(pw[0])
        pw_bd = ew(lambda i: bdiag(pw[i]))
        if level < 4:
            both = ew(lambda i: _mm(jnp.concatenate([t[i], pw[i]], axis=0), pw_bd[i]))
            t = ew(lambda i: t[i] + both[i][0:C])
            pw = ew(lambda i: both[i][C:2 * C])
        else:
            t = ew(lambda i: t[i] + _mm(t[i], pw_bd[i]))
    tokens.extend(previous)
    v_bd = ew(lambda i: bdiag(vc[i]))
    av_both = ew(lambda i: _mm(jnp.concatenate([a_ak[i], a_rk[i]], axis=0), v_bd[i]))
    wt = ew(lambda i: _mm(t[i], bdiag(at[i])).astype(BF16))
    u0 = ew(lambda i: _mm(t[i], bdiag(av_both[i][0:C])))
    y0 = ew(lambda i: av_both[i][C:2 * C])
    wr = ew(lambda i: jnp.concatenate([wt[i], rt[i]], axis=0))
    e_rem = ew(lambda i: jnp.exp(cs[i][C - 1:C, :] - cs[i]))
    kb = ew(lambda i: jnp.concatenate([kc[i] * e_rem[i], bc[i] * e_rem[i]], axis=0).astype(BF16))

    zwide = _ordering_zero(tokens[-1] + tokens[-2], zero_ref, p_ref.shape[-1])[0:1, :]
    zrow = zwide[:, 0:W]
    zrow_b = zrow.astype(BF16)
    for bi in range(bsz):
        prev_ref[bi] = prev_new[bi] + zwide
        if has_vres:
            prevm_ref[bi] = prevm_new[bi] + zwide[:, 0:LANE]
        r_s[bi] = rv[bi] + zrow
        k_s[bi] = kv[bi] + zrow
        v_s[bi] = vv[bi] + zrow
        g_s[bi] = gv[bi] + zrow
    for i in range(len(chains)):
        wr_s[i] = wr[i] + zrow_b
        kb_s[i] = kb[i] + zrow_b
        arb_s[i] = a_rb[i] + zrow_b
        u0_s[i] = u0[i] + zrow
        y0_s[i] = y0[i] + zrow
        dec_s[i] = jnp.exp(cs[i][C - 1:C, :]) + zrow


def _rwkv(p, misc, v_first, mu, mu_vres, w0, w_up, a0, a_up, g_up, k_k, k_a, r_k, gn_g, gn_b, v0, v_up):
    bsz, T, width = p.shape
    has_vres = v_first is not None
    row = lambda x: x.reshape(1, -1)
    nt = T // RW_TB
    nch = bsz * (RW_TB // RW_C)
    blk = lambda w: pl.BlockSpec((bsz, RW_TB, w), lambda t: (0, jnp.minimum(t, nt - 1), 0))
    done = lambda w: pl.BlockSpec((bsz, RW_TB, w), lambda t: (0, jnp.maximum(t - 1, 0), 0))
    vec = _const_spec((1, RW_W))
    args = [p]
    specs = [blk(width)]
    if has_vres:
        args += [misc, v_first]
        specs += [blk(LANE), blk(RW_W)]
    args.append(row(mu))
    specs.append(_const_spec((1, width)))
    if has_vres:
        args.append(_at_lanes(mu_vres, MISC_VRES))
        specs.append(_const_spec((1, LANE)))
    args += [row(w0), _at_rows(w_up, 0), row(a0), _at_rows(a_up, RW_LOW), g_up, row(k_k), row(k_a),
             row(r_k), row(gn_g), row(gn_b)]
    specs += [vec, _const_spec((LANE, RW_W)), vec, _const_spec((LANE, RW_W)),
              _const_spec((LANE, RW_W)), vec, vec, vec, vec, vec]
    if has_vres:
        args += [row(v0), _at_rows(v_up, MISC_VRES)]
        specs += [vec, _const_spec((LANE, RW_W))]
    args.append(jnp.zeros((8, LANE), jnp.uint32))
    specs.append(_const_spec((8, LANE)))
    if has_vres:
        out_shape = jax.ShapeDtypeStruct((bsz, T, RW_W), BF16)
        out_specs = done(RW_W)
    else:
        out_shape = [jax.ShapeDtypeStruct((bsz, T, RW_W), BF16),
                     jax.ShapeDtypeStruct((bsz, T + RW_TB, RW_W), F32)]
        out_specs = [done(RW_W), pl.BlockSpec((bsz, RW_TB, RW_W), lambda t: (0, t, 0))]
    seq = pltpu.VMEM((bsz, RW_TB, RW_W), F32)
    per_chain = lambda rows, dtype: pltpu.VMEM((nch, rows, RW_W), dtype)
    return pl.pallas_call(
        functools.partial(_rwkv_kernel, has_vres=has_vres, bsz=bsz),
        out_shape=out_shape,
        grid=(nt + 1,),
        in_specs=specs,
        out_specs=out_specs,
        scratch_shapes=[pltpu.VMEM((bsz, RW_W, RW_W), F32), pltpu.VMEM((bsz, 1, width), F32)]
        + ([pltpu.VMEM((bsz, 1, LANE), F32)] if has_vres else []) + [seq] * 4
        + [per_chain(2 * RW_C, BF16), per_chain(2 * RW_C, BF16), per_chain(RW_C, BF16),
           per_chain(RW_C, F32), per_chain(RW_C, F32), per_chain(1, F32)],
        compiler_params=pltpu.CompilerParams(dimension_semantics=("arbitrary",),
                                             vmem_limit_bytes=VMEM_LIMIT),
        name="rwkv7_mixer",
    )(*args)


def _mamba_kernel(p_ref, m_ref, cw_ref, cb_ref, dtb_ref, alog_ref, d_ref, ng_ref, o_ref, st_ref, xpad_ref, *, bsz):
    L, TB = MB_L, MB_TB

    @pl.when(pl.program_id(0) == 0)
    def _():
        st_ref[...] = jnp.zeros_like(st_ref)
        xpad_ref[:, 0:8, :] = jnp.zeros((bsz, 8, MB_CONV_DIM), F32)

    tril = _iota((L, L), 0) >= _iota((L, L), 1)
    tril_b = tril.astype(BF16)
    expand = (_iota((LANE, MB_DI), 0) == MISC_DT + (_iota((LANE, MB_DI), 1) // MB_P)).astype(BF16)
    lane_lo = _iota((1, LANE), 1) < MB_P
    gw = MB_DI // MB_G
    a_neg = -jnp.exp(alog_ref[...])

    xs, bms, cms, dts = [], [], [], []
    for bi in range(bsz):
        xpad_ref[bi, 8:8 + TB, :] = p_ref[bi, :, 512:1536]
        conv = cb_ref[...] + xpad_ref[bi, 8:8 + TB, :] * cw_ref[MB_CONV - 1:MB_CONV, :]
        for s in range(1, MB_CONV):
            conv = conv + xpad_ref[bi, 8 - s:8 - s + TB, :] * cw_ref[MB_CONV - 1 - s:MB_CONV - s, :]
        xpad_ref[bi, 0:8, :] = xpad_ref[bi, TB:TB + 8, :]
        xbc = _silu(conv)
        xs.append(xbc[:, 0:512])
        bms.append(xbc[:, 512:768])
        cms.append(xbc[:, 768:1024])
        dts.append(_softplus(m_ref[bi] + dtb_ref[...]))

    chains = [(bi, c) for c in range(TB // L) for bi in range(bsz)]
    pre = []
    for bi, c in chains:
        sl = slice(c * L, (c + 1) * L)
        dtc = dts[bi][sl]
        cs = _mm_sel_r(tril_b, dtc * a_neg, 3)
        cs_t = cs.T
        cs_x = _mm_sel_l(cs, expand, 3, stack=False)
        dt_x = _mm_sel_l(dtc, expand, 2, stack=False)
        cs_last = cs_x[L - 1:L, :]
        xc = xs[bi][sl]
        xdt = xc * dt_x
        xdec = xdt * jnp.exp(cs_last - cs_x)
        y_parts, upd, cgs = [], [], []
        for g in range(MB_G):
            bg = bms[bi][sl, g * MB_N:(g + 1) * MB_N]
            cg = cms[bi][sl, g * MB_N:(g + 1) * MB_N]
            cbm = _mm_nt(cg, bg)
            for hp in range(2):
                pair = g * 2 + hp
                xp = xdt[:, pair * LANE:(pair + 1) * LANE]
                lhs = []
                for e in range(2):
                    h = pair * 2 + e
                    lane = MISC_DT + h
                    diff = cs[:, lane:lane + 1] - cs_t[lane:lane + 1, :]
                    lhs.append((cbm * jnp.exp(jnp.where(tril, diff, -1e30))).astype(BF16))
                yh = _mm(jnp.concatenate(lhs, axis=0), xp)
                y_parts.append(jnp.where(lane_lo, yh[0:L], yh[L:2 * L]))
            upd.append(_mm_tn(bg, xdec[:, g * gw:(g + 1) * gw]))
            cgs.append(cg.astype(BF16))
        pre.append(dict(y=jnp.concatenate(y_parts, axis=1) + d_ref[...] * xc,
                        upd=jnp.concatenate(upd, axis=1), cg=cgs, e_in=jnp.exp(cs_x),
                        dec=jnp.exp(cs_last)))

    st = [st_ref[bi] for bi in range(bsz)]
    ys = [[] for _ in range(bsz)]
    for (bi, c), q in zip(chains, pre):
        stb = st[bi].astype(BF16)
        y_off = jnp.concatenate([_mm(q["cg"][g], stb[:, g * gw:(g + 1) * gw]) for g in range(MB_G)], axis=1)
        ys[bi].append(q["y"] + y_off * q["e_in"])
        st[bi] = st[bi] * q["dec"] + q["upd"]
    for bi in range(bsz):
        st_ref[bi] = st[bi]
        y = jnp.concatenate(ys[bi], axis=0) * _silu(p_ref[bi, :, 0:512])
        outs = []
        for g in range(MB_G):
            yg = y[:, g * gw:(g + 1) * gw]
            outs.append(yg * lax.rsqrt(jnp.mean(yg * yg, axis=-1, keepdims=True) + LN_EPS))
        o_ref[bi] = (jnp.concatenate(outs, axis=1) * ng_ref[...]).astype(BF16)


def _mamba(p, misc, conv_w, conv_b, dt_bias, a_log, d, norm_g):
    bsz, T, _ = p.shape
    pad = lambda vec: _at_lanes(vec, MISC_DT)
    return pl.pallas_call(
        functools.partial(_mamba_kernel, bsz=bsz),
        out_shape=jax.ShapeDtypeStruct((bsz, T, MB_DI), BF16),
        grid=(T // MB_TB,),
        in_specs=[pl.BlockSpec((bsz, MB_TB, MB_PAD), lambda t: (0, t, 0)),
                  pl.BlockSpec((bsz, MB_TB, LANE), lambda t: (0, t, 0)),
                  _const_spec((MB_CONV, MB_CONV_DIM)), _const_spec((1, MB_CONV_DIM)),
                  _const_spec((1, LANE)), _const_spec((1, LANE)), _const_spec((1, MB_DI)),
                  _const_spec((1, MB_DI))],
        out_specs=pl.BlockSpec((bsz, MB_TB, MB_DI), lambda t: (0, t, 0)),
        scratch_shapes=[pltpu.VMEM((bsz, MB_N, MB_DI), F32),
                        pltpu.VMEM((bsz, MB_TB + 8, MB_CONV_DIM), F32)],
        compiler_params=pltpu.CompilerParams(dimension_semantics=("arbitrary",),
                                             vmem_limit_bytes=VMEM_LIMIT),
        name="mamba2_mixer",
    )(p, misc, conv_w, conv_b.reshape(1, -1), pad(dt_bias), pad(a_log),
      jnp.repeat(d, MB_P).reshape(1, -1), norm_g.reshape(1, -1))


def _pad_cols(w, width):
    return jnp.pad(w, ((0, 0), (0, width - w.shape[1])))


def _at_lanes(vec, start, width=LANE):
    return jnp.zeros((1, width), F32).at[0, start:start + vec.shape[0]].set(vec)


def _at_rows(m, start):
    return jnp.zeros((LANE, m.shape[1]), F32).at[start:start + m.shape[0]].set(m)


def _proj_weights(w_in_l, w_vres_l):
    g = w_in_l[:, :GLA_COLS]
    gla = jnp.concatenate([g[:, 0:512], g[:, 528:784]], axis=1)
    rw = w_in_l[:, GLA_COLS:GLA_COLS + RW_COLS]
    m = w_in_l[:, GLA_COLS + RW_COLS:N_IN]
    misc = [g[:, 512:528], m[:, MB_PAD:MB_COLS]]
    if w_vres_l is not None:
        misc.append(w_vres_l)
    misc = _pad_cols(jnp.concatenate(misc, axis=1), LANE)
    return gla.astype(BF16), rw.astype(BF16), m[:, :MB_PAD].astype(BF16), misc.astype(BF16)


def kernel(x, ln_g, ln_b, ffn_w_in, ffn_w_down, w_in, w_in_vres, w_out, gla_gk_up, gla_gk_bias, gla_norm_g, rw_mu, rw_mu_vres, rw_w0, rw_w_up, rw_a0, rw_a_up, rw_g_up, rw_k_k, rw_k_a, rw_r_k, rw_gn_g, rw_gn_b, rw_v0, rw_v_up, mb_conv_w, mb_conv_b, mb_dt_bias, mb_A_log, mb_D, mb_norm_g):
    bsz, T, _ = x.shape
    n = bsz * T
    x2 = x.reshape(n, D_MODEL)
    v_first = None
    ffn_in, ffn_down = _ffn_weights(ffn_w_in, ffn_w_down)
    for l in range(DEPTH):
        x2 = _ffn_ln(x2, ffn_in, ffn_down, l, 0, ln_g[l, 0], ln_b[l, 0])
        vres = l > 0
        weights = _proj_weights(w_in[l], w_in_vres[l - 1] if vres else None)
        gla_p, rw_p, mb_p, misc = (t.reshape(bsz, T, -1) for t in _in_proj(x2, weights))
        gla_o = _gla(gla_p, misc, gla_gk_up[l], gla_gk_bias[l], gla_norm_g[l])
        rw_args = (rw_p, misc, v_first, rw_mu[l], rw_mu_vres[l - 1] if vres else None, rw_w0[l], rw_w_up[l],
                   rw_a0[l], rw_a_up[l], rw_g_up[l], rw_k_k[l], rw_k_a[l], rw_r_k[l].reshape(-1),
                   rw_gn_g[l], rw_gn_b[l])
        if vres:
            rw_o = _rwkv(*rw_args, rw_v0[l - 1], rw_v_up[l - 1])
        else:
            rw_o, v_first = _rwkv(*rw_args, None, None)
        mb_o = _mamba(mb_p, misc, mb_conv_w[l], mb_conv_b[l], mb_dt_bias[l], mb_A_log[l],
                      mb_D[l], mb_norm_g[l])
        x2 = _out_proj_ln(x2, gla_o.reshape(n, -1), rw_o.reshape(n, -1), mb_o.reshape(n, -1),
                          w_out[l], ln_g[l, 1], ln_b[l, 1])
        x2 = _ffn_ln(x2, ffn_in, ffn_down, l, 1, ln_g[l, 2], ln_b[l, 2])
    return x2.reshape(bsz, T, D_MODEL)
```

```python
import functools

import jax
import jax.numpy as jnp
from jax import lax
from jax.experimental import pallas as pl
from jax.experimental.pallas import tpu as pltpu

F32 = jnp.float32
BF16 = jnp.bfloat16

D_MODEL = 1024
DEPTH = 2
D_FF = 2816
LN_EPS = 1e-5
DN_ALPHA = (2 * DEPTH) ** 0.25

GLA_H, GLA_DK, GLA_DV = 4, 32, 64
GLA_GATE_RANK = 16
GLA_GATE_NORM = 16.0
GLA_QK = GLA_H * GLA_DK
GLA_W = GLA_H * GLA_DV
GLA_COLS = 784
GLA_PAD = 768

RW_H, RW_DH = 4, 64
RW_W = 256
RW_COLS = 1024
RW_V_LORA = 32
RW_GN_EPS = 64e-5
RW_LOW = 64

MB_H, MB_P, MB_DI, MB_G, MB_N = 8, 64, 512, 2, 128
MB_CONV = 4
MB_CONV_DIM = 1024
MB_COLS = 1544
MB_PAD = 1536
N_IN = GLA_COLS + RW_COLS + MB_COLS
MISC_GL = 0
MISC_DT = MISC_GL + GLA_GATE_RANK
MISC_VRES = MISC_DT + MB_H

LANE = 128
VMEM_LIMIT = 56 * 1024 * 1024

FFN_TM = 512
FFN_FC = 256
FFN_LN_ROWS = 32
PROJ_TM = 1024
OUT_TM = 1024
GLA_TB, GLA_C = 256, 64
RW_TB, RW_C = 256, 64
MB_TB, MB_L = 256, 128


def _mm(a, b):
    return jnp.dot(a.astype(BF16), b.astype(BF16), preferred_element_type=F32)


def _mm_nt(a, b):
    return lax.dot_general(a.astype(BF16), b.astype(BF16), (((1,), (1,)), ((), ())),
                           preferred_element_type=F32)


def _mm_tn(a, b):
    return lax.dot_general(a.astype(BF16), b.astype(BF16), (((0,), (0,)), ((), ())),
                           preferred_element_type=F32)


def _split(x, parts):
    out = []
    for _ in range(parts - 1):
        piece = x.astype(BF16)
        out.append(piece)
        x = x - piece.astype(F32)
    out.append(x.astype(BF16))
    return out


def _mm_sel_l(x, sel, parts):
    acc = None
    for piece in _split(x, parts):
        term = jnp.dot(piece, sel, preferred_element_type=F32)
        acc = term if acc is None else acc + term
    return acc


def _mm_sel_r(sel, x, parts):
    acc = None
    for piece in _split(x, parts):
        term = jnp.dot(sel, piece, preferred_element_type=F32)
        acc = term if acc is None else acc + term
    return acc


def _layer_norm(y, g, b):
    mu = jnp.mean(y, axis=-1, keepdims=True)
    d = y - mu
    var = jnp.mean(d * d, axis=-1, keepdims=True)
    return d * lax.rsqrt(var + LN_EPS) * g + b


def _silu(x):
    return x * jax.nn.sigmoid(x)


def _softplus(x):
    return jnp.maximum(x, 0.0) + jnp.log(1.0 + jnp.exp(-jnp.abs(x)))


def _ordering_zero(v, zero_ref, width):
    z = pltpu.bitcast(pltpu.bitcast(v, jnp.uint32) & zero_ref[...], F32)
    return jnp.concatenate([z] * (width // LANE), axis=1)


def _iota(shape, dim):
    return lax.broadcasted_iota(jnp.int32, shape, dim)


def _const_spec(shape):
    n = len(shape)
    return pl.BlockSpec(shape, lambda *_: (0,) * n, pipeline_mode=pl.Buffered(1))


def _ffn_ln_kernel(x_ref, xp_ref, win_ref, wd_ref, g_ref, b_ref, zero_ref, o_ref, acc_ref, accp_ref, *,
                   n_chunks, n_tiles):
    step = pl.program_id(0)

    @pl.when(step == 0)
    def _():
        accp_ref[...] = jnp.zeros_like(accp_ref)

    def finish_previous():
        token = None
        for r in range(0, FFN_TM, FFN_LN_ROWS):
            rows = pl.ds(r, FFN_LN_ROWS)
            out = _layer_norm(DN_ALPHA * xp_ref[rows, :] + accp_ref[rows, :], g_ref[...], b_ref[...])
            o_ref[rows, :] = out
            tiles = [out[i:i + 8, j:j + LANE] for i in range(0, FFN_LN_ROWS, 8) for j in range(0, D_MODEL, LANE)]
            if token is not None:
                tiles.append(token)
            token = functools.reduce(lambda a, b: a + b, tiles)
            yield token

    @pl.when(step < n_tiles)
    def _():
        previous = finish_previous()
        per_chunk = -(-(FFN_TM // FFN_LN_ROWS) // (n_chunks - 1))
        xb = x_ref[...].astype(BF16)
        pin = None
        for c in range(n_chunks):
            gate = jnp.dot(xb, win_ref[:, c * FFN_FC:(c + 1) * FFN_FC], preferred_element_type=F32)
            up = jnp.dot(xb, win_ref[:, D_FF + c * FFN_FC:D_FF + (c + 1) * FFN_FC], preferred_element_type=F32)
            act = _silu(gate) * up
            if pin is not None:
                act = jnp.concatenate([act[0:8, :] + _ordering_zero(pin, zero_ref, FFN_FC), act[8:, :]], axis=0)
            part = jnp.dot(act.astype(BF16), wd_ref[c], preferred_element_type=F32)
            if c == 0:
                acc_ref[...] = part
            elif c < n_chunks - 1:
                acc_ref[...] += part
            else:
                for pin in previous:
                    pass
                accp_ref[...] = acc_ref[...] + part + _ordering_zero(pin, zero_ref, D_MODEL)[0:1, :]
            for _ in range(per_chunk):
                pin = next(previous, pin)

    @pl.when(step == n_tiles)
    def _():
        for _ in finish_previous():
            pass


def _ffn_weights(ffn_w_in, ffn_w_down):
    nc = D_FF // FFN_FC
    wd = (0.5 * ffn_w_down).astype(BF16).reshape(DEPTH, 2, nc, FFN_FC, D_MODEL)
    return ffn_w_in.astype(BF16), wd


def _ffn_ln(x2, w_in_all, wd_all, layer, half, g, b):
    n = x2.shape[0]
    nc = D_FF // FFN_FC
    nt = n // FFN_TM
    pick = lambda shape: pl.BlockSpec((None, None) + shape, lambda i: (layer, half) + (0,) * len(shape),
                                      pipeline_mode=pl.Buffered(1))
    cur = pl.BlockSpec((FFN_TM, D_MODEL), lambda i: (jnp.minimum(i, nt - 1), 0))
    done = pl.BlockSpec((FFN_TM, D_MODEL), lambda i: (jnp.maximum(i - 1, 0), 0))
    return pl.pallas_call(
        functools.partial(_ffn_ln_kernel, n_chunks=nc, n_tiles=nt),
        out_shape=jax.ShapeDtypeStruct((n, D_MODEL), F32),
        grid=(nt + 1,),
        in_specs=[
            cur,
            done,
            pick((D_MODEL, 2 * D_FF)),
            pick((nc, FFN_FC, D_MODEL)),
            _const_spec((1, D_MODEL)),
            _const_spec((1, D_MODEL)),
            _const_spec((8, LANE)),
        ],
        out_specs=done,
        scratch_shapes=[pltpu.VMEM((FFN_TM, D_MODEL), F32), pltpu.VMEM((FFN_TM, D_MODEL), F32)],
        compiler_params=pltpu.CompilerParams(dimension_semantics=("arbitrary",),
                                             vmem_limit_bytes=VMEM_LIMIT),
        name="ffn_ln",
    )(x2, x2, w_in_all, wd_all, g.reshape(1, -1), b.reshape(1, -1), jnp.zeros((8, LANE), jnp.uint32))


def _in_proj_kernel(x_ref, *refs):
    xb = x_ref[...].astype(BF16)
    n_groups = len(refs) // 2
    for w_ref, o_ref in zip(refs[:n_groups], refs[n_groups:]):
        o_ref[...] = jnp.dot(xb, w_ref[...], preferred_element_type=F32)


def _in_proj(x2, weights):
    n = x2.shape[0]
    widths = [w.shape[1] for w in weights]
    return pl.pallas_call(
        _in_proj_kernel,
        out_shape=[jax.ShapeDtypeStruct((n, w), F32) for w in widths],
        grid=(n // PROJ_TM,),
        in_specs=[pl.BlockSpec((PROJ_TM, D_MODEL), lambda i: (i, 0))]
        + [_const_spec((D_MODEL, w)) for w in widths],
        out_specs=[pl.BlockSpec((PROJ_TM, w), lambda i: (i, 0)) for w in widths],
        compiler_params=pltpu.CompilerParams(dimension_semantics=("parallel",),
                                             vmem_limit_bytes=VMEM_LIMIT),
        name="in_proj",
    )(x2, *weights)


def _out_proj_ln_kernel(x_ref, ma_ref, mb_ref, mc_ref, wa_ref, wb_ref, wc_ref, g_ref, b_ref, o_ref):
    acc = jnp.dot(ma_ref[...], wa_ref[...], preferred_element_type=F32)
    acc += jnp.dot(mb_ref[...], wb_ref[...], preferred_element_type=F32)
    acc += jnp.dot(mc_ref[...], wc_ref[...], preferred_element_type=F32)
    o_ref[...] = acc
    for r in range(0, OUT_TM, FFN_LN_ROWS):
        rows = pl.ds(r, FFN_LN_ROWS)
        o_ref[rows, :] = _layer_norm(DN_ALPHA * x_ref[rows, :] + o_ref[rows, :], g_ref[...], b_ref[...])


def _out_proj_ln(x2, ma, mb, mc, w_out, g, b):
    n = x2.shape[0]
    wo = w_out.astype(BF16)
    wa, wb, wc = wo[:GLA_W], wo[GLA_W:GLA_W + RW_W], wo[GLA_W + RW_W:]
    tm = OUT_TM
    row = lambda w: pl.BlockSpec((tm, w), lambda i: (i, 0))
    return pl.pallas_call(
        _out_proj_ln_kernel,
        out_shape=jax.ShapeDtypeStruct((n, D_MODEL), F32),
        grid=(n // tm,),
        in_specs=[row(D_MODEL), row(GLA_W), row(RW_W), row(MB_DI),
                  _const_spec((GLA_W, D_MODEL)), _const_spec((RW_W, D_MODEL)),
                  _const_spec((MB_DI, D_MODEL)), _const_spec((1, D_MODEL)), _const_spec((1, D_MODEL))],
        out_specs=row(D_MODEL),
        compiler_params=pltpu.CompilerParams(dimension_semantics=("parallel",),
                                             vmem_limit_bytes=VMEM_LIMIT),
        name="out_proj_ln",
    )(x2, ma, mb, mc, wa, wb, wc, g.reshape(1, -1), b.reshape(1, -1))


def _gla_kernel(p_ref, m_ref, gkup_ref, gkb_ref, ng_ref, o_ref, st_ref, *, bsz):
    C = GLA_C

    @pl.when(pl.program_id(0) == 0)
    def _():
        st_ref[...] = jnp.zeros_like(st_ref)

    TB = GLA_TB
    tril_blk = (((_iota((TB, TB), 0) // C) == (_iota((TB, TB), 1) // C))
                & (_iota((TB, TB), 0) >= _iota((TB, TB), 1))).astype(BF16)
    tril4 = (_iota((GLA_H * C, C), 0) % C) >= _iota((GLA_H * C, C), 1)
    qhead = _iota((1, GLA_QK), 1) // GLA_DK
    vhead = _iota((1, GLA_W), 1) // GLA_DV
    smask = (_iota((GLA_W, GLA_QK), 0) // GLA_DV) == (_iota((GLA_W, GLA_QK), 1) // GLA_DK)
    gmat = ((_iota((GLA_W, GLA_W), 0) // GLA_DV) == (_iota((GLA_W, GLA_W), 1) // GLA_DV)).astype(BF16) / GLA_DV

    eb = lambda f: [f(bi) for bi in range(bsz)]
    z = eb(lambda bi: _mm(m_ref[bi], gkup_ref[...]) + gkb_ref[...])
    gk = eb(lambda bi: -_softplus(-z[bi]) / GLA_GATE_NORM)
    b_all = eb(lambda bi: _mm_sel_r(tril_blk, gk[bi], 3))
    qe_all = eb(lambda bi: p_ref[bi, :, 0:128] * (GLA_DK ** -0.5) * jnp.exp(b_all[bi]))
    ke_all = eb(lambda bi: p_ref[bi, :, 128:256] * jnp.exp(-b_all[bi]))

    n = TB // C
    chains = [(bi, c) for c in range(n) for bi in range(bsz)]
    ew = lambda f: [f(bi, slice(c * C, (c + 1) * C)) for bi, c in chains]
    b_last = ew(lambda bi, s: b_all[bi][s.stop - 1:s.stop, :])
    v = ew(lambda bi, s: p_ref[bi, s, 256:512])
    q_e = ew(lambda bi, s: qe_all[bi][s])
    k_e = ew(lambda bi, s: ke_all[bi][s])
    kdec = ew(lambda bi, s: p_ref[bi, s, 128:256] * jnp.exp(b_all[bi][s.stop - 1:s.stop, :] - b_all[bi][s]))
    ei = lambda f: [f(i) for i in range(len(chains))]
    q_st = ei(lambda i: jnp.concatenate([jnp.where(qhead == h, q_e[i], 0.0) for h in range(GLA_H)], axis=0))
    att = ei(lambda i: jnp.where(tril4, _mm_nt(q_st[i], k_e[i]), 0.0))
    r = ei(lambda i: _mm(att[i], v[i]))
    upd = ei(lambda i: jnp.where(smask, _mm_tn(v[i], kdec[i]), 0.0))

    st = [st_ref[bi] for bi in range(bsz)]
    outs = [[] for _ in range(bsz)]
    for i, (bi, c) in enumerate(chains):
        o = _mm_nt(q_e[i], st[bi])
        for h in range(GLA_H):
            o = o + jnp.where(vhead == h, r[i][h * C:(h + 1) * C], 0.0)
        outs[bi].append(o)
        st[bi] = st[bi] * jnp.exp(b_last[i]) + upd[i]
    for bi in range(bsz):
        st_ref[bi] = st[bi]
        o = jnp.concatenate(outs[bi], axis=0)
        ms = _mm_sel_l(o * o, gmat, 2)
        o = o * lax.rsqrt(ms + LN_EPS) * ng_ref[...]
        o_ref[bi] = (o * _silu(p_ref[bi, :, 512:768])).astype(BF16)


def _gla(p, misc, gk_up, gk_bias, norm_g):
    bsz, T, _ = p.shape
    gkup = _at_rows(gk_up, MISC_GL)
    return pl.pallas_call(
        functools.partial(_gla_kernel, bsz=bsz),
        out_shape=jax.ShapeDtypeStruct((bsz, T, GLA_W), BF16),
        grid=(T // GLA_TB,),
        in_specs=[pl.BlockSpec((bsz, GLA_TB, GLA_PAD), lambda t: (0, t, 0)),
                  pl.BlockSpec((bsz, GLA_TB, LANE), lambda t: (0, t, 0)),
                  _const_spec((LANE, GLA_QK)), _const_spec((1, GLA_QK)), _const_spec((1, GLA_W))],
        out_specs=pl.BlockSpec((bsz, GLA_TB, GLA_W), lambda t: (0, t, 0)),
        scratch_shapes=[pltpu.VMEM((bsz, GLA_W, GLA_QK), F32)],
        compiler_params=pltpu.CompilerParams(dimension_semantics=("arbitrary",),
                                             vmem_limit_bytes=VMEM_LIMIT),
        name="gla_mixer",
    )(p, misc, gkup, gk_bias.reshape(1, -1), jnp.tile(norm_g, GLA_H).reshape(1, -1))


def _rwkv_kernel(*refs, has_vres, bsz):
    if has_vres:
        (p_ref, m_ref, vf_ref, mu_ref, mum_ref, w0_ref, wup_ref, a0_ref, aup_ref, gup_ref, kk_ref, ka_ref, rk_ref,
         gng_ref, gnb_ref, v0_ref, vup_ref, zero_ref, o_ref,
         ht_ref, prev_ref, prevm_ref, r_s, k_s, v_s, g_s, wr_s, kb_s, arb_s, u0_s, y0_s, dec_s) = refs
        carried = (prevm_ref,)
    else:
        (p_ref, mu_ref, w0_ref, wup_ref, a0_ref, aup_ref, gup_ref, kk_ref, ka_ref, rk_ref,
         gng_ref, gnb_ref, zero_ref, o_ref, vf_out_ref,
         ht_ref, prev_ref, r_s, k_s, v_s, g_s, wr_s, kb_s, arb_s, u0_s, y0_s, dec_s) = refs
        carried = ()
    C, TB, W, H = RW_C, RW_TB, RW_W, RW_H
    carried += (ht_ref, prev_ref, r_s, k_s, v_s, g_s, wr_s, kb_s, arb_s, u0_s, y0_s, dec_s)

    @pl.when(pl.program_id(0) == 0)
    def _():
        for ref in carried:
            ref[...] = jnp.zeros_like(ref)

    head_eq = (_iota((W, W), 0) // RW_DH) == (_iota((W, W), 1) // RW_DH)
    gsum = head_eq.astype(BF16)
    gmean = (head_eq.astype(F32) / RW_DH).astype(BF16)
    tril_blk = (((_iota((TB, TB), 0) // C) == (_iota((TB, TB), 1) // C))
                & (_iota((TB, TB), 0) >= _iota((TB, TB), 1))).astype(BF16)
    ri = _iota((C, W), 0)
    li = _iota((C, W), 1) % C
    strict = ri > li
    incl = ri >= li
    eye = (ri == li).astype(F32)
    blk_eq = (_iota((H * C, W), 0) // C) == (_iota((H * C, W), 1) // RW_DH)

    def bdiag(m):
        return jnp.where(blk_eq, jnp.concatenate([m] * H, axis=0), 0.0).astype(BF16)

    chains = [(bi, c) for c in range(TB // C) for bi in range(bsz)]
    ew = lambda f: [f(i) for i in range(len(chains))]
    sl = [(bi, slice(c * C, (c + 1) * C)) for bi, c in chains]

    def finish_previous():
        ht = [ht_ref[bi] for bi in range(bsz)]
        ys = [[] for _ in range(bsz)]
        for i, (bi, c) in enumerate(chains):
            sh = _mm_nt(wr_s[i], ht[bi])
            u = sh[0:C] + u0_s[i]
            ys[bi].append(sh[C:2 * C] + y0_s[i] + _mm(arb_s[i], bdiag(u)))
            upd = _mm_tn(jnp.concatenate([v_s[sl[i]], u], axis=0), kb_s[i])
            ht[bi] = ht[bi] * dec_s[i] + jnp.where(head_eq, upd, 0.0)
            yield ht[bi][0:8, 0:LANE]
        for bi in range(bsz):
            ht_ref[bi] = ht[bi]
            y = jnp.concatenate(ys[bi], axis=0)
            mean = _mm_sel_l(y, gmean, 2)
            d = y - mean
            var = _mm_sel_l(d * d, gmean, 2)
            y = d * lax.rsqrt(var + RW_GN_EPS) * gng_ref[...] + gnb_ref[...]
            bonus = _mm_sel_l(r_s[bi] * k_s[bi] * rk_ref[...], gsum, 2) * v_s[bi]
            out = (y + bonus) * g_s[bi]
            o_ref[bi] = out.astype(BF16)
            tiles = [out[i:i + 8, j:j + LANE] for i in range(0, TB, 8) for j in range(0, W, LANE)]
            yield functools.reduce(lambda a, b: a + b, tiles)

    previous = finish_previous()

    tokens = []
    sites = [0]

    def tie(x, lag=2):
        token = next(previous, None)
        if token is not None:
            tokens.append(token)
        at_site = sites[0] - lag
        sites[0] += 1
        if not 0 <= at_site < len(tokens):
            return x
        top = x[0:8, 0:LANE] + _ordering_zero(tokens[at_site], zero_ref, LANE)
        if x.shape[1] > LANE:
            top = jnp.concatenate([top, x[0:8, LANE:]], axis=1)
        return jnp.concatenate([top, x[8:]], axis=0)

    rv, kv, vv, gv, av, bv, lwv, csv = [], [], [], [], [], [], [], []
    prev_new, prevm_new = [], []
    for bi in range(bsz):
        p = p_ref[bi]
        shifted = jnp.where(_iota(p.shape, 0) == 0, prev_ref[bi], pltpu.roll(p, 1, 0))
        xs = p + (shifted - p) * mu_ref[...]
        guard = jnp.concatenate([_ordering_zero(xs[0:8, j:j + LANE], zero_ref, LANE)
                                 for j in range(0, p.shape[1], LANE)], axis=1)
        prev_new.append(p[TB - 1:TB, :] + guard[0:1, :])
        r = xs[:, 0:256]
        k = xs[:, 256:512]
        v = xs[:, 512:768]
        wa_low = xs[:, 768:896]
        g_low = xs[:, 896:1024]
        if has_vres:
            m = m_ref[bi]
            shifted_m = jnp.where(_iota(m.shape, 0) == 0, prevm_ref[bi], pltpu.roll(m, 1, 0))
            v_low = m + (shifted_m - m) * mum_ref[...]
            prevm_new.append(m[TB - 1:TB, :] + _ordering_zero(v_low[0:8, :], zero_ref, LANE)[0:1, :])
            v = v + (vf_ref[bi] - v) * jax.nn.sigmoid(v0_ref[...] + _mm(v_low, vup_ref[...]))
        else:
            vf_out_ref[bi] = v
        w_low = a_low = wa_low
        w = -_softplus(-(w0_ref[...] + _mm(jnp.tanh(w_low), wup_ref[...]))) - 0.5
        lw = tie(-jnp.exp(w))
        lwv.append(lw)
        csv.append(_mm_sel_r(tril_blk, lw, 3))
        a = jax.nn.sigmoid(a0_ref[...] + _mm(a_low, aup_ref[...]))
        gv.append(_mm(jax.nn.sigmoid(g_low), gup_ref[...]))
        kk = k * kk_ref[...]
        kk = tie(kk / jnp.maximum(jnp.sqrt(_mm_sel_l(kk * kk, gsum, 2)), 1e-12))
        rv.append(r)
        kv.append(tie(k * (1.0 + (a - 1.0) * ka_ref[...])))
        vv.append(v)
        av.append(-kk)
        bv.append(kk * a)

    cut = lambda vals: ew(lambda i: vals[sl[i][0]][sl[i][1]])
    cs = cut(csv)
    kc = cut(kv)
    bc = cut(bv)
    vc = cut(vv)
    ac = cut(av)
    lwc = cut(lwv)
    rc = cut(rv)
    e_neg = ew(lambda i: jnp.exp(-cs[i]))
    e_neg[0] = tie(e_neg[0])
    at = ew(lambda i: ac[i] * jnp.exp(cs[i] - lwc[i]))
    rt = ew(lambda i: (rc[i] * jnp.exp(cs[i])).astype(BF16))
    ar = ew(lambda i: jnp.concatenate([at[i].astype(BF16), rt[i]], axis=0))
    pk = ew(lambda i: _mm_nt(ar[i], bdiag(kc[i] * e_neg[i])))
    pb = ew(lambda i: _mm_nt(ar[i], bdiag(bc[i] * e_neg[i])))
    a_ak = ew(lambda i: jnp.where(strict, pk[i][0:C], 0.0))
    a_rk = ew(lambda i: jnp.where(incl, pk[i][C:2 * C], 0.0))
    a_ab = ew(lambda i: jnp.where(strict, pb[i][0:C], 0.0))
    a_ab[0] = tie(a_ab[0])
    a_rb = ew(lambda i: jnp.where(incl, pb[i][C:2 * C], 0.0).astype(BF16))
    t = ew(lambda i: eye + a_ab[i])
    pw = ew(lambda i: _mm(a_ab[i], bdiag(a_ab[i])))
    for level in range(5):
        pw[0] = tie(pw[0])
        pw_bd = ew(lambda i: bdiag(pw[i]))
        if level < 4:
            both = ew(lambda i: _mm(jnp.concatenate([t[i], pw[i]], axis=0), pw_bd[i]))
            t = ew(lambda i: t[i] + both[i][0:C])
            pw = ew(lambda i: both[i][C:2 * C])
        else:
            t = ew(lambda i: t[i] + _mm(t[i], pw_bd[i]))
    tokens.extend(previous)
    v_bd = ew(lambda i: bdiag(vc[i]))
    av_both = ew(lambda i: _mm(jnp.concatenate([a_ak[i], a_rk[i]], axis=0), v_bd[i]))
    wt = ew(lambda i: _mm(t[i], bdiag(at[i])).astype(BF16))
    u0 = ew(lambda i: _mm(t[i], bdiag(av_both[i][0:C])))
    y0 = ew(lambda i: av_both[i][C:2 * C])
    wr = ew(lambda i: jnp.concatenate([wt[i], rt[i]], axis=0))
    e_rem = ew(lambda i: jnp.exp(cs[i][C - 1:C, :] - cs[i]))
    kb = ew(lambda i: jnp.concatenate([kc[i] * e_rem[i], bc[i] * e_rem[i]], axis=0).astype(BF16))

    zrow = _ordering_zero(functools.reduce(lambda a, b: a + b, tokens), zero_ref, W)[0:1, :]
    zrow_b = zrow.astype(BF16)
    for bi in range(bsz):
        prev_ref[bi] = prev_new[bi]
        if has_vres:
            prevm_ref[bi] = prevm_new[bi]
        r_s[bi] = rv[bi] + zrow
        k_s[bi] = kv[bi] + zrow
        v_s[bi] = vv[bi] + zrow
        g_s[bi] = gv[bi] + zrow
    for i in range(len(chains)):
        wr_s[i] = wr[i] + zrow_b
        kb_s[i] = kb[i] + zrow_b
        arb_s[i] = a_rb[i] + zrow_b
        u0_s[i] = u0[i] + zrow
        y0_s[i] = y0[i] + zrow
        dec_s[i] = jnp.exp(cs[i][C - 1:C, :]) + zrow


def _rwkv(p, misc, v_first, mu, mu_vres, w0, w_up, a0, a_up, g_up, k_k, k_a, r_k, gn_g, gn_b, v0, v_up):
    bsz, T, width = p.shape
    has_vres = v_first is not None
    row = lambda x: x.reshape(1, -1)
    nt = T // RW_TB
    nch = bsz * (RW_TB // RW_C)
    blk = lambda w: pl.BlockSpec((bsz, RW_TB, w), lambda t: (0, jnp.minimum(t, nt - 1), 0))
    done = lambda w: pl.BlockSpec((bsz, RW_TB, w), lambda t: (0, jnp.maximum(t - 1, 0), 0))
    vec = _const_spec((1, RW_W))
    args = [p]
    specs = [blk(width)]
    if has_vres:
        args += [misc, v_first]
        specs += [blk(LANE), blk(RW_W)]
    args.append(row(mu))
    specs.append(_const_spec((1, width)))
    if has_vres:
        args.append(_at_lanes(mu_vres, MISC_VRES))
        specs.append(_const_spec((1, LANE)))
    args += [row(w0), _at_rows(w_up, 0), row(a0), _at_rows(a_up, RW_LOW), g_up, row(k_k), row(k_a),
             row(r_k), row(gn_g), row(gn_b)]
    specs += [vec, _const_spec((LANE, RW_W)), vec, _const_spec((LANE, RW_W)),
              _const_spec((LANE, RW_W)), vec, vec, vec, vec, vec]
    if has_vres:
        args += [row(v0), _at_rows(v_up, MISC_VRES)]
        specs += [vec, _const_spec((LANE, RW_W))]
    args.append(jnp.zeros((8, LANE), jnp.uint32))
    specs.append(_const_spec((8, LANE)))
    if has_vres:
        out_shape = jax.ShapeDtypeStruct((bsz, T, RW_W), BF16)
        out_specs = done(RW_W)
    else:
        out_shape = [jax.ShapeDtypeStruct((bsz, T, RW_W), BF16),
                     jax.ShapeDtypeStruct((bsz, T + RW_TB, RW_W), F32)]
        out_specs = [done(RW_W), pl.BlockSpec((bsz, RW_TB, RW_W), lambda t: (0, t, 0))]
    seq = pltpu.VMEM((bsz, RW_TB, RW_W), F32)
    per_chain = lambda rows, dtype: pltpu.VMEM((nch, rows, RW_W), dtype)
    return pl.pallas_call(
        functools.partial(_rwkv_kernel, has_vres=has_vres, bsz=bsz),
        out_shape=out_shape,
        grid=(nt + 1,),
        in_specs=specs,
        out_specs=out_specs,
        scratch_shapes=[pltpu.VMEM((bsz, RW_W, RW_W), F32), pltpu.VMEM((bsz, 1, width), F32)]
        + ([pltpu.VMEM((bsz, 1, LANE), F32)] if has_vres else []) + [seq] * 4
        + [per_chain(2 * RW_C, BF16), per_chain(2 * RW_C, BF16), per_chain(RW_C, BF16),
           per_chain(RW_C, F32), per_chain(RW_C, F32), per_chain(1, F32)],
        compiler_params=pltpu.CompilerParams(dimension_semantics=("arbitrary",),
                                             vmem_limit_bytes=VMEM_LIMIT),
        name="rwkv7_mixer",
    )(*args)


def _mamba_kernel(p_ref, m_ref, cw_ref, cb_ref, dtb_ref, alog_ref, d_ref, ng_ref, zero_ref, o_ref, st_ref, xpad_ref,
                  *, bsz):
    L, TB = MB_L, MB_TB

    @pl.when(pl.program_id(0) == 0)
    def _():
        st_ref[...] = jnp.zeros_like(st_ref)
        xpad_ref[:, 0:8, :] = jnp.zeros((bsz, 8, MB_CONV_DIM), F32)

    tril = _iota((L, L), 0) >= _iota((L, L), 1)
    tril_b = tril.astype(BF16)
    expand = (_iota((LANE, MB_DI), 0) == MISC_DT + (_iota((LANE, MB_DI), 1) // MB_P)).astype(BF16)
    lane_lo = _iota((1, LANE), 1) < MB_P
    gw = MB_DI // MB_G
    a_neg = -jnp.exp(alog_ref[...])

    xs, bms, cms, dts = [], [], [], []
    for bi in range(bsz):
        xpad_ref[bi, 8:8 + TB, :] = p_ref[bi, :, 512:1536]
        conv = cb_ref[...] + xpad_ref[bi, 8:8 + TB, :] * cw_ref[MB_CONV - 1:MB_CONV, :]
        for s in range(1, MB_CONV):
            conv = conv + xpad_ref[bi, 8 - s:8 - s + TB, :] * cw_ref[MB_CONV - 1 - s:MB_CONV - s, :]
        guard = jnp.concatenate([_ordering_zero(conv[0:8, j:j + LANE], zero_ref, LANE)
                                 for j in range(0, MB_CONV_DIM, LANE)], axis=1)
        xpad_ref[bi, 0:8, :] = xpad_ref[bi, TB:TB + 8, :] + guard
        xbc = _silu(conv)
        xs.append(xbc[:, 0:512])
        bms.append(xbc[:, 512:768])
        cms.append(xbc[:, 768:1024])
        dts.append(_softplus(m_ref[bi] + dtb_ref[...]))

    chains = [(bi, c) for c in range(TB // L) for bi in range(bsz)]
    pre = []
    for bi, c in chains:
        sl = slice(c * L, (c + 1) * L)
        dtc = dts[bi][sl]
        cs = _mm_sel_r(tril_b, dtc * a_neg, 3)
        cs_t = cs.T
        cs_x = _mm_sel_l(cs, expand, 3)
        dt_x = _mm_sel_l(dtc, expand, 2)
        cs_last = cs_x[L - 1:L, :]
        xc = xs[bi][sl]
        xdt = xc * dt_x
        xdec = xdt * jnp.exp(cs_last - cs_x)
        y_parts, upd, cgs = [], [], []
        for g in range(MB_G):
            bg = bms[bi][sl, g * MB_N:(g + 1) * MB_N]
            cg = cms[bi][sl, g * MB_N:(g + 1) * MB_N]
            cbm = _mm_nt(cg, bg)
            for hp in range(2):
                pair = g * 2 + hp
                xp = xdt[:, pair * LANE:(pair + 1) * LANE]
                lhs = []
                for e in range(2):
                    h = pair * 2 + e
                    lane = MISC_DT + h
                    diff = cs[:, lane:lane + 1] - cs_t[lane:lane + 1, :]
                    lhs.append((cbm * jnp.exp(jnp.where(tril, diff, -1e30))).astype(BF16))
                yh = _mm(jnp.concatenate(lhs, axis=0), xp)
                y_parts.append(jnp.where(lane_lo, yh[0:L], yh[L:2 * L]))
            upd.append(_mm_tn(bg, xdec[:, g * gw:(g + 1) * gw]))
            cgs.append(cg.astype(BF16))
        pre.append(dict(y=jnp.concatenate(y_parts, axis=1) + d_ref[...] * xc,
                        upd=jnp.concatenate(upd, axis=1), cg=cgs, e_in=jnp.exp(cs_x),
                        dec=jnp.exp(cs_last)))

    st = [st_ref[bi] for bi in range(bsz)]
    ys = [[] for _ in range(bsz)]
    for (bi, c), q in zip(chains, pre):
        stb = st[bi].astype(BF16)
        y_off = jnp.concatenate([_mm(q["cg"][g], stb[:, g * gw:(g + 1) * gw]) for g in range(MB_G)], axis=1)
        ys[bi].append(q["y"] + y_off * q["e_in"])
        st[bi] = st[bi] * q["dec"] + q["upd"]
    for bi in range(bsz):
        st_ref[bi] = st[bi]
        y = jnp.concatenate(ys[bi], axis=0) * _silu(p_ref[bi, :, 0:512])
        outs = []
        for g in range(MB_G):
            yg = y[:, g * gw:(g + 1) * gw]
            outs.append(yg * lax.rsqrt(jnp.mean(yg * yg, axis=-1, keepdims=True) + LN_EPS))
        o_ref[bi] = (jnp.concatenate(outs, axis=1) * ng_ref[...]).astype(BF16)


def _mamba(p, misc, conv_w, conv_b, dt_bias, a_log, d, norm_g):
    bsz, T, _ = p.shape
    pad = lambda vec: _at_lanes(vec, MISC_DT)
    return pl.pallas_call(
        functools.partial(_mamba_kernel, bsz=bsz),
        out_shape=jax.ShapeDtypeStruct((bsz, T, MB_DI), BF16),
        grid=(T // MB_TB,),
        in_specs=[pl.BlockSpec((bsz, MB_TB, MB_PAD), lambda t: (0, t, 0)),
                  pl.BlockSpec((bsz, MB_TB, LANE), lambda t: (0, t, 0)),
                  _const_spec((MB_CONV, MB_CONV_DIM)), _const_spec((1, MB_CONV_DIM)),
                  _const_spec((1, LANE)), _const_spec((1, LANE)), _const_spec((1, MB_DI)),
                  _const_spec((1, MB_DI)), _const_spec((8, LANE))],
        out_specs=pl.BlockSpec((bsz, MB_TB, MB_DI), lambda t: (0, t, 0)),
        scratch_shapes=[pltpu.VMEM((bsz, MB_N, MB_DI), F32),
                        pltpu.VMEM((bsz, MB_TB + 8, MB_CONV_DIM), F32)],
        compiler_params=pltpu.CompilerParams(dimension_semantics=("arbitrary",),
                                             vmem_limit_bytes=VMEM_LIMIT),
        name="mamba2_mixer",
    )(p, misc, conv_w, conv_b.reshape(1, -1), pad(dt_bias), pad(a_log),
      jnp.repeat(d, MB_P).reshape(1, -1), norm_g.reshape(1, -1), jnp.zeros((8, LANE), jnp.uint32))


def _pad_cols(w, width):
    return jnp.pad(w, ((0, 0), (0, width - w.shape[1])))


def _at_lanes(vec, start, width=LANE):
    return jnp.zeros((1, width), F32).at[0, start:start + vec.shape[0]].set(vec)


def _at_rows(m, start):
    return jnp.zeros((LANE, m.shape[1]), F32).at[start:start + m.shape[0]].set(m)


def _proj_weights(w_in_l, w_vres_l):
    g = w_in_l[:, :GLA_COLS]
    gla = jnp.concatenate([g[:, 0:512], g[:, 528:784]], axis=1)
    rw = w_in_l[:, GLA_COLS:GLA_COLS + RW_COLS]
    m = w_in_l[:, GLA_COLS + RW_COLS:N_IN]
    misc = [g[:, 512:528], m[:, MB_PAD:MB_COLS]]
    if w_vres_l is not None:
        misc.append(w_vres_l)
    misc = _pad_cols(jnp.concatenate(misc, axis=1), LANE)
    return gla.astype(BF16), rw.astype(BF16), m[:, :MB_PAD].astype(BF16), misc.astype(BF16)


def kernel(x, ln_g, ln_b, ffn_w_in, ffn_w_down, w_in, w_in_vres, w_out, gla_gk_up, gla_gk_bias, gla_norm_g, rw_mu, rw_mu_vres, rw_w0, rw_w_up, rw_a0, rw_a_up, rw_g_up, rw_k_k, rw_k_a, rw_r_k, rw_gn_g, rw_gn_b, rw_v0, rw_v_up, mb_conv_w, mb_conv_b, mb_dt_bias, mb_A_log, mb_D, mb_norm_g):
    bsz, T, _ = x.shape
    n = bsz * T
    x2 = x.reshape(n, D_MODEL)
    v_first = None
    ffn_in, ffn_down = _ffn_weights(ffn_w_in, ffn_w_down)
    for l in range(DEPTH):
        x2 = _ffn_ln(x2, ffn_in, ffn_down, l, 0, ln_g[l, 0], ln_b[l, 0])
        vres = l > 0
        weights = _proj_weights(w_in[l], w_in_vres[l - 1] if vres else None)
        gla_p, rw_p, mb_p, misc = (t.reshape(bsz, T, -1) for t in _in_proj(x2, weights))
        gla_o = _gla(gla_p, misc, gla_gk_up[l], gla_gk_bias[l], gla_norm_g[l])
        rw_args = (rw_p, misc, v_first, rw_mu[l], rw_mu_vres[l - 1] if vres else None, rw_w0[l], rw_w_up[l],
                   rw_a0[l], rw_a_up[l], rw_g_up[l], rw_k_k[l], rw_k_a[l], rw_r_k[l].reshape(-1),
                   rw_gn_g[l], rw_gn_b[l])
        if vres:
            rw_o = _rwkv(*rw_args, rw_v0[l - 1], rw_v_up[l - 1])
        else:
            rw_o, v_first = _rwkv(*rw_args, None, None)
        mb_o = _mamba(mb_p, misc, mb_conv_w[l], mb_conv_b[l], mb_dt_bias[l], mb_A_log[l],
                      mb_D[l], mb_norm_g[l])
        x2 = _out_proj_ln(x2, gla_o.reshape(n, -1), rw_o.reshape(n, -1), mb_o.reshape(n, -1),
                          w_out[l], ln_g[l, 1], ln_b[l, 1])
        x2 = _ffn_ln(x2, ffn_in, ffn_down, l, 1, ln_g[l, 2], ln_b[l, 2])
    return x2.reshape(bsz, T, D_MODEL)
```

```python
import functools

import jax
import jax.numpy as jnp
from jax import lax
from jax.experimental import pallas as pl
from jax.experimental.pallas import tpu as pltpu

F32 = jnp.float32
BF16 = jnp.bfloat16

D_MODEL = 1024
DEPTH = 2
D_FF = 2816
LN_EPS = 1e-5
DN_ALPHA = (2 * DEPTH) ** 0.25

GLA_H, GLA_DK, GLA_DV = 4, 32, 64
GLA_GATE_RANK = 16
GLA_GATE_NORM = 16.0
GLA_QK = GLA_H * GLA_DK
GLA_W = GLA_H * GLA_DV
GLA_COLS = 784
GLA_PAD = 768

RW_H, RW_DH = 4, 64
RW_W = 256
RW_COLS = 1024
RW_V_LORA = 32
RW_GN_EPS = 64e-5
RW_LOW = 64

MB_H, MB_P, MB_DI, MB_G, MB_N = 8, 64, 512, 2, 128
MB_CONV = 4
MB_CONV_DIM = 1024
MB_COLS = 1544
MB_PAD = 1536
N_IN = GLA_COLS + RW_COLS + MB_COLS
MISC_GL = 0
MISC_DT = MISC_GL + GLA_GATE_RANK
MISC_VRES = MISC_DT + MB_H

LANE = 128
VMEM_LIMIT = 56 * 1024 * 1024

FFN_TM = 512
FFN_FC = 256
FFN_LN_ROWS = 32
PROJ_TM = 1024
OUT_TM = 1024
GLA_TB, GLA_C = 256, 64
RW_TB, RW_C = 256, 64
MB_TB, MB_L = 256, 128


def _mm(a, b):
    return jnp.dot(a.astype(BF16), b.astype(BF16), preferred_element_type=F32)


def _mm_nt(a, b):
    return lax.dot_general(a.astype(BF16), b.astype(BF16), (((1,), (1,)), ((), ())),
                           preferred_element_type=F32)


def _mm_tn(a, b):
    return lax.dot_general(a.astype(BF16), b.astype(BF16), (((0,), (0,)), ((), ())),
                           preferred_element_type=F32)


def _split(x, parts):
    out = []
    for _ in range(parts - 1):
        piece = x.astype(BF16)
        out.append(piece)
        x = x - piece.astype(F32)
    out.append(x.astype(BF16))
    return out


def _mm_sel_l(x, sel, parts):
    acc = None
    for piece in _split(x, parts):
        term = jnp.dot(piece, sel, preferred_element_type=F32)
        acc = term if acc is None else acc + term
    return acc


def _mm_sel_r(sel, x, parts):
    acc = None
    for piece in _split(x, parts):
        term = jnp.dot(sel, piece, preferred_element_type=F32)
        acc = term if acc is None else acc + term
    return acc


def _layer_norm(y, g, b):
    mu = jnp.mean(y, axis=-1, keepdims=True)
    d = y - mu
    var = jnp.mean(d * d, axis=-1, keepdims=True)
    return d * lax.rsqrt(var + LN_EPS) * g + b


def _silu(x):
    return x * jax.nn.sigmoid(x)


def _softplus(x):
    return jnp.maximum(x, 0.0) + jnp.log(1.0 + jnp.exp(-jnp.abs(x)))


def _ordering_zero(v, zero_ref, width):
    z = pltpu.bitcast(pltpu.bitcast(v, jnp.uint32) & zero_ref[...], F32)
    return jnp.concatenate([z] * (width // LANE), axis=1)


def _iota(shape, dim):
    return lax.broadcasted_iota(jnp.int32, shape, dim)


def _const_spec(shape):
    n = len(shape)
    return pl.BlockSpec(shape, lambda *_: (0,) * n, pipeline_mode=pl.Buffered(1))


def _ffn_ln_kernel(x_ref, xp_ref, win_ref, wd_ref, g_ref, b_ref, zero_ref, o_ref, acc_ref, accp_ref, *,
                   n_chunks, n_tiles):
    step = pl.program_id(0)

    @pl.when(step == 0)
    def _():
        accp_ref[...] = jnp.zeros_like(accp_ref)

    def finish_previous():
        token = None
        for r in range(0, FFN_TM, FFN_LN_ROWS):
            rows = pl.ds(r, FFN_LN_ROWS)
            out = _layer_norm(DN_ALPHA * xp_ref[rows, :] + accp_ref[rows, :], g_ref[...], b_ref[...])
            o_ref[rows, :] = out
            tiles = [out[i:i + 8, j:j + LANE] for i in range(0, FFN_LN_ROWS, 8) for j in range(0, D_MODEL, LANE)]
            if token is not None:
                tiles.append(token)
            token = functools.reduce(lambda a, b: a + b, tiles)
            yield token

    @pl.when(step < n_tiles)
    def _():
        previous = finish_previous()
        per_chunk = -(-(FFN_TM // FFN_LN_ROWS) // (n_chunks - 1))
        xb = x_ref[...].astype(BF16)
        pin = None
        for c in range(n_chunks):
            gate = jnp.dot(xb, win_ref[:, c * FFN_FC:(c + 1) * FFN_FC], preferred_element_type=F32)
            up = jnp.dot(xb, win_ref[:, D_FF + c * FFN_FC:D_FF + (c + 1) * FFN_FC], preferred_element_type=F32)
            act = _silu(gate) * up
            if pin is not None:
                act = jnp.concatenate([act[0:8, :] + _ordering_zero(pin, zero_ref, FFN_FC), act[8:, :]], axis=0)
            part = jnp.dot(act.astype(BF16), wd_ref[c], preferred_element_type=F32)
            if c == 0:
                acc_ref[...] = part
            elif c < n_chunks - 1:
                acc_ref[...] += part
            else:
                for pin in previous:
                    pass
                accp_ref[...] = acc_ref[...] + part + _ordering_zero(pin, zero_ref, D_MODEL)[0:1, :]
            for _ in range(per_chunk):
                pin = next(previous, pin)

    @pl.when(step == n_tiles)
    def _():
        for _ in finish_previous():
            pass


def _ffn_weights(ffn_w_in, ffn_w_down):
    nc = D_FF // FFN_FC
    wd = (0.5 * ffn_w_down).astype(BF16).reshape(DEPTH, 2, nc, FFN_FC, D_MODEL)
    return ffn_w_in.astype(BF16), wd


def _ffn_ln(x2, w_in_all, wd_all, layer, half, g, b):
    n = x2.shape[0]
    nc = D_FF // FFN_FC
    nt = n // FFN_TM
    pick = lambda shape: pl.BlockSpec((None, None) + shape, lambda i: (layer, half) + (0,) * len(shape),
                                      pipeline_mode=pl.Buffered(1))
    cur = pl.BlockSpec((FFN_TM, D_MODEL), lambda i: (jnp.minimum(i, nt - 1), 0))
    done = pl.BlockSpec((FFN_TM, D_MODEL), lambda i: (jnp.maximum(i - 1, 0), 0))
    return pl.pallas_call(
        functools.partial(_ffn_ln_kernel, n_chunks=nc, n_tiles=nt),
        out_shape=jax.ShapeDtypeStruct((n, D_MODEL), F32),
        grid=(nt + 1,),
        in_specs=[
            cur,
            done,
            pick((D_MODEL, 2 * D_FF)),
            pick((nc, FFN_FC, D_MODEL)),
            _const_spec((1, D_MODEL)),
            _const_spec((1, D_MODEL)),
            _const_spec((8, LANE)),
        ],
        out_specs=done,
        scratch_shapes=[pltpu.VMEM((FFN_TM, D_MODEL), F32), pltpu.VMEM((FFN_TM, D_MODEL), F32)],
        compiler_params=pltpu.CompilerParams(dimension_semantics=("arbitrary",),
                                             vmem_limit_bytes=VMEM_LIMIT),
        name="ffn_ln",
    )(x2, x2, w_in_all, wd_all, g.reshape(1, -1), b.reshape(1, -1), jnp.zeros((8, LANE), jnp.uint32))


def _in_proj_kernel(x_ref, *refs):
    xb = x_ref[...].astype(BF16)
    n_groups = len(refs) // 2
    for w_ref, o_ref in zip(refs[:n_groups], refs[n_groups:]):
        o_ref[...] = jnp.dot(xb, w_ref[...], preferred_element_type=F32)


def _in_proj(x2, weights):
    n = x2.shape[0]
    widths = [w.shape[1] for w in weights]
    return pl.pallas_call(
        _in_proj_kernel,
        out_shape=[jax.ShapeDtypeStruct((n, w), F32) for w in widths],
        grid=(n // PROJ_TM,),
        in_specs=[pl.BlockSpec((PROJ_TM, D_MODEL), lambda i: (i, 0))]
        + [_const_spec((D_MODEL, w)) for w in widths],
        out_specs=[pl.BlockSpec((PROJ_TM, w), lambda i: (i, 0)) for w in widths],
        compiler_params=pltpu.CompilerParams(dimension_semantics=("parallel",),
                                             vmem_limit_bytes=VMEM_LIMIT),
        name="in_proj",
    )(x2, *weights)


def _out_proj_ln_kernel(x_ref, ma_ref, mb_ref, mc_ref, wa_ref, wb_ref, wc_ref, g_ref, b_ref, o_ref):
    acc = jnp.dot(ma_ref[...], wa_ref[...], preferred_element_type=F32)
    acc += jnp.dot(mb_ref[...], wb_ref[...], preferred_element_type=F32)
    acc += jnp.dot(mc_ref[...], wc_ref[...], preferred_element_type=F32)
    o_ref[...] = acc
    for r in range(0, OUT_TM, FFN_LN_ROWS):
        rows = pl.ds(r, FFN_LN_ROWS)
        o_ref[rows, :] = _layer_norm(DN_ALPHA * x_ref[rows, :] + o_ref[rows, :], g_ref[...], b_ref[...])


def _out_proj_ln(x2, ma, mb, mc, w_out, g, b):
    n = x2.shape[0]
    wo = w_out.astype(BF16)
    wa, wb, wc = wo[:GLA_W], wo[GLA_W:GLA_W + RW_W], wo[GLA_W + RW_W:]
    tm = OUT_TM
    row = lambda w: pl.BlockSpec((tm, w), lambda i: (i, 0))
    return pl.pallas_call(
        _out_proj_ln_kernel,
        out_shape=jax.ShapeDtypeStruct((n, D_MODEL), F32),
        grid=(n // tm,),
        in_specs=[row(D_MODEL), row(GLA_W), row(RW_W), row(MB_DI),
                  _const_spec((GLA_W, D_MODEL)), _const_spec((RW_W, D_MODEL)),
                  _const_spec((MB_DI, D_MODEL)), _const_spec((1, D_MODEL)), _const_spec((1, D_MODEL))],
        out_specs=row(D_MODEL),
        compiler_params=pltpu.CompilerParams(dimension_semantics=("parallel",),
                                             vmem_limit_bytes=VMEM_LIMIT),
        name="out_proj_ln",
    )(x2, ma, mb, mc, wa, wb, wc, g.reshape(1, -1), b.reshape(1, -1))


def _gla_kernel(p_ref, m_ref, gkup_ref, gkb_ref, ng_ref, o_ref, st_ref, *, bsz):
    C = GLA_C

    @pl.when(pl.program_id(0) == 0)
    def _():
        st_ref[...] = jnp.zeros_like(st_ref)

    TB = GLA_TB
    tril_blk = (((_iota((TB, TB), 0) // C) == (_iota((TB, TB), 1) // C))
                & (_iota((TB, TB), 0) >= _iota((TB, TB), 1))).astype(BF16)
    tril4 = (_iota((GLA_H * C, C), 0) % C) >= _iota((GLA_H * C, C), 1)
    qhead = _iota((1, GLA_QK), 1) // GLA_DK
    vhead = _iota((1, GLA_W), 1) // GLA_DV
    smask = (_iota((GLA_W, GLA_QK), 0) // GLA_DV) == (_iota((GLA_W, GLA_QK), 1) // GLA_DK)
    gmat = ((_iota((GLA_W, GLA_W), 0) // GLA_DV) == (_iota((GLA_W, GLA_W), 1) // GLA_DV)).astype(BF16) / GLA_DV

    eb = lambda f: [f(bi) for bi in range(bsz)]
    z = eb(lambda bi: _mm(m_ref[bi], gkup_ref[...]) + gkb_ref[...])
    gk = eb(lambda bi: -_softplus(-z[bi]) / GLA_GATE_NORM)
    b_all = eb(lambda bi: _mm_sel_r(tril_blk, gk[bi], 3))
    qe_all = eb(lambda bi: p_ref[bi, :, 0:128] * (GLA_DK ** -0.5) * jnp.exp(b_all[bi]))
    ke_all = eb(lambda bi: p_ref[bi, :, 128:256] * jnp.exp(-b_all[bi]))

    n = TB // C
    chains = [(bi, c) for c in range(n) for bi in range(bsz)]
    ew = lambda f: [f(bi, slice(c * C, (c + 1) * C)) for bi, c in chains]
    b_last = ew(lambda bi, s: b_all[bi][s.stop - 1:s.stop, :])
    v = ew(lambda bi, s: p_ref[bi, s, 256:512])
    q_e = ew(lambda bi, s: qe_all[bi][s])
    k_e = ew(lambda bi, s: ke_all[bi][s])
    kdec = ew(lambda bi, s: p_ref[bi, s, 128:256] * jnp.exp(b_all[bi][s.stop - 1:s.stop, :] - b_all[bi][s]))
    ei = lambda f: [f(i) for i in range(len(chains))]
    q_st = ei(lambda i: jnp.concatenate([jnp.where(qhead == h, q_e[i], 0.0) for h in range(GLA_H)], axis=0))
    att = ei(lambda i: jnp.where(tril4, _mm_nt(q_st[i], k_e[i]), 0.0))
    r = ei(lambda i: _mm(att[i], v[i]))
    upd = ei(lambda i: jnp.where(smask, _mm_tn(v[i], kdec[i]), 0.0))

    st = [st_ref[bi] for bi in range(bsz)]
    outs = [[] for _ in range(bsz)]
    for i, (bi, c) in enumerate(chains):
        o = _mm_nt(q_e[i], st[bi])
        for h in range(GLA_H):
            o = o + jnp.where(vhead == h, r[i][h * C:(h + 1) * C], 0.0)
        outs[bi].append(o)
        st[bi] = st[bi] * jnp.exp(b_last[i]) + upd[i]
    for bi in range(bsz):
        st_ref[bi] = st[bi]
        o = jnp.concatenate(outs[bi], axis=0)
        ms = _mm_sel_l(o * o, gmat, 2)
        o = o * lax.rsqrt(ms + LN_EPS) * ng_ref[...]
        o_ref[bi] = (o * _silu(p_ref[bi, :, 512:768])).astype(BF16)


def _gla(p, misc, gk_up, gk_bias, norm_g):
    bsz, T, _ = p.shape
    gkup = _at_rows(gk_up, MISC_GL)
    return pl.pallas_call(
        functools.partial(_gla_kernel, bsz=bsz),
        out_shape=jax.ShapeDtypeStruct((bsz, T, GLA_W), BF16),
        grid=(T // GLA_TB,),
        in_specs=[pl.BlockSpec((bsz, GLA_TB, GLA_PAD), lambda t: (0, t, 0)),
                  pl.BlockSpec((bsz, GLA_TB, LANE), lambda t: (0, t, 0)),
                  _const_spec((LANE, GLA_QK)), _const_spec((1, GLA_QK)), _const_spec((1, GLA_W))],
        out_specs=pl.BlockSpec((bsz, GLA_TB, GLA_W), lambda t: (0, t, 0)),
        scratch_shapes=[pltpu.VMEM((bsz, GLA_W, GLA_QK), F32)],
        compiler_params=pltpu.CompilerParams(dimension_semantics=("arbitrary",),
                                             vmem_limit_bytes=VMEM_LIMIT),
        name="gla_mixer",
    )(p, misc, gkup, gk_bias.reshape(1, -1), jnp.tile(norm_g, GLA_H).reshape(1, -1))


def _rwkv_kernel(*refs, has_vres, bsz):
    if has_vres:
        (p_ref, m_ref, vf_ref, mu_ref, mum_ref, w0_ref, wup_ref, a0_ref, aup_ref, gup_ref, kk_ref, ka_ref, rk_ref,
         gng_ref, gnb_ref, v0_ref, vup_ref, zero_ref, o_ref,
         ht_ref, prev_ref, prevm_ref, r_s, k_s, v_s, g_s, wr_s, kb_s, arb_s, u0_s, y0_s, dec_s) = refs
        carried = (prevm_ref,)
    else:
        (p_ref, mu_ref, w0_ref, wup_ref, a0_ref, aup_ref, gup_ref, kk_ref, ka_ref, rk_ref,
         gng_ref, gnb_ref, zero_ref, o_ref, vf_out_ref,
         ht_ref, prev_ref, r_s, k_s, v_s, g_s, wr_s, kb_s, arb_s, u0_s, y0_s, dec_s) = refs
        carried = ()
    C, TB, W, H = RW_C, RW_TB, RW_W, RW_H
    carried += (ht_ref, prev_ref, r_s, k_s, v_s, g_s, wr_s, kb_s, arb_s, u0_s, y0_s, dec_s)

    @pl.when(pl.program_id(0) == 0)
    def _():
        for ref in carried:
            ref[...] = jnp.zeros_like(ref)

    head_eq = (_iota((W, W), 0) // RW_DH) == (_iota((W, W), 1) // RW_DH)
    gsum = head_eq.astype(BF16)
    gmean = (head_eq.astype(F32) / RW_DH).astype(BF16)
    tril_blk = (((_iota((TB, TB), 0) // C) == (_iota((TB, TB), 1) // C))
                & (_iota((TB, TB), 0) >= _iota((TB, TB), 1))).astype(BF16)
    ri = _iota((C, W), 0)
    li = _iota((C, W), 1) % C
    strict = ri > li
    incl = ri >= li
    eye = (ri == li).astype(F32)
    blk_one = ((_iota((H * C, W), 0) // C) == (_iota((H * C, W), 1) // RW_DH)).astype(BF16)

    def bdiag(m):
        return jnp.concatenate([m.astype(BF16)] * H, axis=0) * blk_one

    chains = [(bi, c) for c in range(TB // C) for bi in range(bsz)]
    ew = lambda f: [f(i) for i in range(len(chains))]
    sl = [(bi, slice(c * C, (c + 1) * C)) for bi, c in chains]

    def finish_previous():
        ht = [ht_ref[bi] for bi in range(bsz)]
        ys = [[] for _ in range(bsz)]
        for i, (bi, c) in enumerate(chains):
            sh = _mm_nt(wr_s[i], ht[bi])
            u = sh[0:C] + u0_s[i]
            ys[bi].append(sh[C:2 * C] + y0_s[i] + _mm(arb_s[i], bdiag(u)))
            upd = _mm_tn(jnp.concatenate([v_s[sl[i]], u], axis=0), kb_s[i])
            ht[bi] = ht[bi] * dec_s[i] + jnp.where(head_eq, upd, 0.0)
            yield ht[bi][0:8, 0:LANE]
        for bi in range(bsz):
            ht_ref[bi] = ht[bi]
            y = jnp.concatenate(ys[bi], axis=0)
            mean = _mm_sel_l(y, gmean, 2)
            d = y - mean
            var = _mm_sel_l(d * d, gmean, 2)
            y = d * lax.rsqrt(var + RW_GN_EPS) * gng_ref[...] + gnb_ref[...]
            bonus = _mm_sel_l(r_s[bi] * k_s[bi] * rk_ref[...], gsum, 2) * v_s[bi]
            out = (y + bonus) * g_s[bi]
            o_ref[bi] = out.astype(BF16)
            tiles = [out[i:i + 8, j:j + LANE] for i in range(0, TB, 8) for j in range(0, W, LANE)]
            yield functools.reduce(lambda a, b: a + b, tiles)

    previous = finish_previous()

    tokens = []
    sites = [0]

    def tie(x, lag=2):
        token = next(previous, None)
        if token is not None:
            tokens.append(token)
        at_site = sites[0] - lag
        sites[0] += 1
        if not 0 <= at_site < len(tokens):
            return x
        top = x[0:8, 0:LANE] + _ordering_zero(tokens[at_site], zero_ref, LANE)
        if x.shape[1] > LANE:
            top = jnp.concatenate([top, x[0:8, LANE:]], axis=1)
        return jnp.concatenate([top, x[8:]], axis=0)

    rv, kv, vv, gv, av, bv, lwv, csv = [], [], [], [], [], [], [], []
    prev_new, prevm_new = [], []
    for bi in range(bsz):
        p = p_ref[bi]
        shifted = jnp.where(_iota(p.shape, 0) == 0, prev_ref[bi], pltpu.roll(p, 1, 0))
        xs = p + (shifted - p) * mu_ref[...]
        guard = jnp.concatenate([_ordering_zero(xs[0:8, j:j + LANE], zero_ref, LANE)
                                 for j in range(0, p.shape[1], LANE)], axis=1)
        prev_new.append(p[TB - 1:TB, :] + guard[0:1, :])
        r = xs[:, 0:256]
        k = xs[:, 256:512]
        v = xs[:, 512:768]
        wa_low = xs[:, 768:896]
        g_low = xs[:, 896:1024]
        if has_vres:
            m = m_ref[bi]
            shifted_m = jnp.where(_iota(m.shape, 0) == 0, prevm_ref[bi], pltpu.roll(m, 1, 0))
            v_low = m + (shifted_m - m) * mum_ref[...]
            prevm_new.append(m[TB - 1:TB, :] + _ordering_zero(v_low[0:8, :], zero_ref, LANE)[0:1, :])
            v = v + (vf_ref[bi] - v) * jax.nn.sigmoid(v0_ref[...] + _mm(v_low, vup_ref[...]))
        else:
            vf_out_ref[bi] = v
        w_low = a_low = wa_low
        w = -_softplus(-(w0_ref[...] + _mm(jnp.tanh(w_low), wup_ref[...]))) - 0.5
        lw = tie(-jnp.exp(w))
        lwv.append(lw)
        csv.append(_mm_sel_r(tril_blk, lw, 3))
        a = jax.nn.sigmoid(a0_ref[...] + _mm(a_low, aup_ref[...]))
        gv.append(_mm(jax.nn.sigmoid(g_low), gup_ref[...]))
        kk = k * kk_ref[...]
        kk = tie(kk / jnp.maximum(jnp.sqrt(_mm_sel_l(kk * kk, gsum, 2)), 1e-12))
        rv.append(r)
        kv.append(tie(k * (1.0 + (a - 1.0) * ka_ref[...])))
        vv.append(v)
        av.append(-kk)
        bv.append(kk * a)

    cut = lambda vals: ew(lambda i: vals[sl[i][0]][sl[i][1]])
    cs = cut(csv)
    kc = cut(kv)
    bc = cut(bv)
    vc = cut(vv)
    ac = cut(av)
    lwc = cut(lwv)
    rc = cut(rv)
    e_neg = ew(lambda i: jnp.exp(-cs[i]))
    e_neg[0] = tie(e_neg[0])
    at = ew(lambda i: ac[i] * jnp.exp(cs[i] - lwc[i]))
    rt = ew(lambda i: (rc[i] * jnp.exp(cs[i])).astype(BF16))
    ar = ew(lambda i: jnp.concatenate([at[i].astype(BF16), rt[i]], axis=0))
    pk = ew(lambda i: _mm_nt(ar[i], bdiag(kc[i] * e_neg[i])))
    pb = ew(lambda i: _mm_nt(ar[i], bdiag(bc[i] * e_neg[i])))
    a_ak = ew(lambda i: jnp.where(strict, pk[i][0:C], 0.0))
    a_rk = ew(lambda i: jnp.where(incl, pk[i][C:2 * C], 0.0))
    a_ab = ew(lambda i: jnp.where(strict, pb[i][0:C], 0.0))
    a_ab[0] = tie(a_ab[0])
    a_rb = ew(lambda i: jnp.where(incl, pb[i][C:2 * C], 0.0).astype(BF16))
    t = ew(lambda i: eye + a_ab[i])
    pw = ew(lambda i: _mm(a_ab[i], bdiag(a_ab[i])))
    for level in range(5):
        pw[0] = tie(pw[0])
        pw_bd = ew(lambda i: bdiag(pw[i]))
        if level < 4:
            both = ew(lambda i: _mm(jnp.concatenate([t[i], pw[i]], axis=0), pw_bd[i]))
            t = ew(lambda i: t[i] + both[i][0:C])
            pw = ew(lambda i: both[i][C:2 * C])
        else:
            t = ew(lambda i: t[i] + _mm(t[i], pw_bd[i]))
    tokens.extend(previous)
    v_bd = ew(lambda i: bdiag(vc[i]))
    av_both = ew(lambda i: _mm(jnp.concatenate([a_ak[i], a_rk[i]], axis=0), v_bd[i]))
    wt = ew(lambda i: _mm(t[i], bdiag(at[i])).astype(BF16))
    u0 = ew(lambda i: _mm(t[i], bdiag(av_both[i][0:C])))
    y0 = ew(lambda i: av_both[i][C:2 * C])
    wr = ew(lambda i: jnp.concatenate([wt[i], rt[i]], axis=0))
    e_rem = ew(lambda i: jnp.exp(cs[i][C - 1:C, :] - cs[i]))
    kb = ew(lambda i: jnp.concatenate([kc[i] * e_rem[i], bc[i] * e_rem[i]], axis=0).astype(BF16))

    zrow = _ordering_zero(functools.reduce(lambda a, b: a + b, tokens), zero_ref, W)[0:1, :]
    zrow_b = zrow.astype(BF16)
    for bi in range(bsz):
        prev_ref[bi] = prev_new[bi]
        if has_vres:
            prevm_ref[bi] = prevm_new[bi]
        r_s[bi] = rv[bi] + zrow
        k_s[bi] = kv[bi] + zrow
        v_s[bi] = vv[bi] + zrow
        g_s[bi] = gv[bi] + zrow
    for i in range(len(chains)):
        wr_s[i] = wr[i] + zrow_b
        kb_s[i] = kb[i] + zrow_b
        arb_s[i] = a_rb[i] + zrow_b
        u0_s[i] = u0[i] + zrow
        y0_s[i] = y0[i] + zrow
        dec_s[i] = jnp.exp(cs[i][C - 1:C, :]) + zrow


def _rwkv(p, misc, v_first, mu, mu_vres, w0, w_up, a0, a_up, g_up, k_k, k_a, r_k, gn_g, gn_b, v0, v_up):
    bsz, T, width = p.shape
    has_vres = v_first is not None
    row = lambda x: x.reshape(1, -1)
    nt = T // RW_TB
    nch = bsz * (RW_TB // RW_C)
    blk = lambda w: pl.BlockSpec((bsz, RW_TB, w), lambda t: (0, jnp.minimum(t, nt - 1), 0))
    done = lambda w: pl.BlockSpec((bsz, RW_TB, w), lambda t: (0, jnp.maximum(t - 1, 0), 0))
    vec = _const_spec((1, RW_W))
    args = [p]
    specs = [blk(width)]
    if has_vres:
        args += [misc, v_first]
        specs += [blk(LANE), blk(RW_W)]
    args.append(row(mu))
    specs.append(_const_spec((1, width)))
    if has_vres:
        args.append(_at_lanes(mu_vres, MISC_VRES))
        specs.append(_const_spec((1, LANE)))
    args += [row(w0), _at_rows(w_up, 0), row(a0), _at_rows(a_up, RW_LOW), g_up, row(k_k), row(k_a),
             row(r_k), row(gn_g), row(gn_b)]
    specs += [vec, _const_spec((LANE, RW_W)), vec, _const_spec((LANE, RW_W)),
              _const_spec((LANE, RW_W)), vec, vec, vec, vec, vec]
    if has_vres:
        args += [row(v0), _at_rows(v_up, MISC_VRES)]
        specs += [vec, _const_spec((LANE, RW_W))]
    args.append(jnp.zeros((8, LANE), jnp.uint32))
    specs.append(_const_spec((8, LANE)))
    if has_vres:
        out_shape = jax.ShapeDtypeStruct((bsz, T, RW_W), BF16)
        out_specs = done(RW_W)
    else:
        out_shape = [jax.ShapeDtypeStruct((bsz, T, RW_W), BF16),
                     jax.ShapeDtypeStruct((bsz, T + RW_TB, RW_W), F32)]
        out_specs = [done(RW_W), pl.BlockSpec((bsz, RW_TB, RW_W), lambda t: (0, t, 0))]
    seq = pltpu.VMEM((bsz, RW_TB, RW_W), F32)
    per_chain = lambda rows, dtype: pltpu.VMEM((nch, rows, RW_W), dtype)
    return pl.pallas_call(
        functools.partial(_rwkv_kernel, has_vres=has_vres, bsz=bsz),
        out_shape=out_shape,
        grid=(nt + 1,),
        in_specs=specs,
        out_specs=out_specs,
        scratch_shapes=[pltpu.VMEM((bsz, RW_W, RW_W), F32), pltpu.VMEM((bsz, 1, width), F32)]
        + ([pltpu.VMEM((bsz, 1, LANE), F32)] if has_vres else []) + [seq] * 4
        + [per_chain(2 * RW_C, BF16), per_chain(2 * RW_C, BF16), per_chain(RW_C, BF16),
           per_chain(RW_C, F32), per_chain(RW_C, F32), per_chain(1, F32)],
        compiler_params=pltpu.CompilerParams(dimension_semantics=("arbitrary",),
                                             vmem_limit_bytes=VMEM_LIMIT),
        name="rwkv7_mixer",
    )(*args)


def _mamba_kernel(p_ref, m_ref, cw_ref, cb_ref, dtb_ref, alog_ref, d_ref, ng_ref, zero_ref, o_ref, st_ref, xpad_ref,
                  *, bsz):
    L, TB = MB_L, MB_TB

    @pl.when(pl.program_id(0) == 0)
    def _():
        st_ref[...] = jnp.zeros_like(st_ref)
        xpad_ref[:, 0:8, :] = jnp.zeros((bsz, 8, MB_CONV_DIM), F32)

    tril = _iota((L, L), 0) >= _iota((L, L), 1)
    tril_b = tril.astype(BF16)
    expand = (_iota((LANE, MB_DI), 0) == MISC_DT + (_iota((LANE, MB_DI), 1) // MB_P)).astype(BF16)
    lane_lo = _iota((1, LANE), 1) < MB_P
    gw = MB_DI // MB_G
    a_neg = -jnp.exp(alog_ref[...])

    xs, bms, cms, dts = [], [], [], []
    for bi in range(bsz):
        xpad_ref[bi, 8:8 + TB, :] = p_ref[bi, :, 512:1536]
        conv = cb_ref[...] + xpad_ref[bi, 8:8 + TB, :] * cw_ref[MB_CONV - 1:MB_CONV, :]
        for s in range(1, MB_CONV):
            conv = conv + xpad_ref[bi, 8 - s:8 - s + TB, :] * cw_ref[MB_CONV - 1 - s:MB_CONV - s, :]
        guard = jnp.concatenate([_ordering_zero(conv[0:8, j:j + LANE], zero_ref, LANE)
                                 for j in range(0, MB_CONV_DIM, LANE)], axis=1)
        xpad_ref[bi, 0:8, :] = xpad_ref[bi, TB:TB + 8, :] + guard
        xbc = _silu(conv)
        xs.append(xbc[:, 0:512])
        bms.append(xbc[:, 512:768])
        cms.append(xbc[:, 768:1024])
        dts.append(_softplus(m_ref[bi] + dtb_ref[...]))

    chains = [(bi, c) for c in range(TB // L) for bi in range(bsz)]
    pre = []
    for bi, c in chains:
        sl = slice(c * L, (c + 1) * L)
        dtc = dts[bi][sl]
        cs = _mm_sel_r(tril_b, dtc * a_neg, 3)
        cs_t = cs.T
        cs_x = _mm_sel_l(cs, expand, 3)
        dt_x = _mm_sel_l(dtc, expand, 2)
        cs_last = cs_x[L - 1:L, :]
        xc = xs[bi][sl]
        xdt = xc * dt_x
        xdec = xdt * jnp.exp(cs_last - cs_x)
        y_parts, upd, cgs = [], [], []
        for g in range(MB_G):
            bg = bms[bi][sl, g * MB_N:(g + 1) * MB_N]
            cg = cms[bi][sl, g * MB_N:(g + 1) * MB_N]
            cbm = _mm_nt(cg, bg)
            for hp in range(2):
                pair = g * 2 + hp
                xp = xdt[:, pair * LANE:(pair + 1) * LANE]
                lhs = []
                for e in range(2):
                    h = pair * 2 + e
                    lane = MISC_DT + h
                    diff = cs[:, lane:lane + 1] - cs_t[lane:lane + 1, :]
                    lhs.append((cbm * jnp.exp(jnp.where(tril, diff, -1e30))).astype(BF16))
                yh = _mm(jnp.concatenate(lhs, axis=0), xp)
                y_parts.append(jnp.where(lane_lo, yh[0:L], yh[L:2 * L]))
            upd.append(_mm_tn(bg, xdec[:, g * gw:(g + 1) * gw]))
            cgs.append(cg.astype(BF16))
        pre.append(dict(y=jnp.concatenate(y_parts, axis=1) + d_ref[...] * xc,
                        upd=jnp.concatenate(upd, axis=1), cg=cgs, e_in=jnp.exp(cs_x),
                        dec=jnp.exp(cs_last)))

    st = [st_ref[bi] for bi in range(bsz)]
    ys = [[] for _ in range(bsz)]
    for (bi, c), q in zip(chains, pre):
        stb = st[bi].astype(BF16)
        y_off = jnp.concatenate([_mm(q["cg"][g], stb[:, g * gw:(g + 1) * gw]) for g in range(MB_G)], axis=1)
        ys[bi].append(q["y"] + y_off * q["e_in"])
        st[bi] = st[bi] * q["dec"] + q["upd"]
    for bi in range(bsz):
        st_ref[bi] = st[bi]
        y = jnp.concatenate(ys[bi], axis=0) * _silu(p_ref[bi, :, 0:512])
        outs = []
        for g in range(MB_G):
            yg = y[:, g * gw:(g + 1) * gw]
            outs.append(yg * lax.rsqrt(jnp.mean(yg * yg, axis=-1, keepdims=True) + LN_EPS))
        o_ref[bi] = (jnp.concatenate(outs, axis=1) * ng_ref[...]).astype(BF16)


def _mamba(p, misc, conv_w, conv_b, dt_bias, a_log, d, norm_g):
    bsz, T, _ = p.shape
    pad = lambda vec: _at_lanes(vec, MISC_DT)
    return pl.pallas_call(
        functools.partial(_mamba_kernel, bsz=bsz),
        out_shape=jax.ShapeDtypeStruct((bsz, T, MB_DI), BF16),
        grid=(T // MB_TB,),
        in_specs=[pl.BlockSpec((bsz, MB_TB, MB_PAD), lambda t: (0, t, 0)),
                  pl.BlockSpec((bsz, MB_TB, LANE), lambda t: (0, t, 0)),
                  _const_spec((MB_CONV, MB_CONV_DIM)), _const_spec((1, MB_CONV_DIM)),
                  _const_spec((1, LANE)), _const_spec((1, LANE)), _const_spec((1, MB_DI)),
                  _const_spec((1, MB_DI)), _const_spec((8, LANE))],
        out_specs=pl.BlockSpec((bsz, MB_TB, MB_DI), lambda t: (0, t, 0)),
        scratch_shapes=[pltpu.VMEM((bsz, MB_N, MB_DI), F32),
                        pltpu.VMEM((bsz, MB_TB + 8, MB_CONV_DIM), F32)],
        compiler_params=pltpu.CompilerParams(dimension_semantics=("arbitrary",),
                                             vmem_limit_bytes=VMEM_LIMIT),
        name="mamba2_mixer",
    )(p, misc, conv_w, conv_b.reshape(1, -1), pad(dt_bias), pad(a_log),
      jnp.repeat(d, MB_P).reshape(1, -1), norm_g.reshape(1, -1), jnp.zeros((8, LANE), jnp.uint32))


def _pad_cols(w, width):
    return jnp.pad(w, ((0, 0), (0, width - w.shape[1])))


def _at_lanes(vec, start, width=LANE):
    return jnp.zeros((1, width), F32).at[0, start:start + vec.shape[0]].set(vec)


def _at_rows(m, start):
    return jnp.zeros((LANE, m.shape[1]), F32).at[start:start + m.shape[0]].set(m)


def _proj_weights(w_in_l, w_vres_l):
    g = w_in_l[:, :GLA_COLS]
    gla = jnp.concatenate([g[:, 0:512], g[:, 528:784]], axis=1)
    rw = w_in_l[:, GLA_COLS:GLA_COLS + RW_COLS]
    m = w_in_l[:, GLA_COLS + RW_COLS:N_IN]
    misc = [g[:, 512:528], m[:, MB_PAD:MB_COLS]]
    if w_vres_l is not None:
        misc.append(w_vres_l)
    misc = _pad_cols(jnp.concatenate(misc, axis=1), LANE)
    return gla.astype(BF16), rw.astype(BF16), m[:, :MB_PAD].astype(BF16), misc.astype(BF16)


def kernel(x, ln_g, ln_b, ffn_w_in, ffn_w_down, w_in, w_in_vres, w_out, gla_gk_up, gla_gk_bias, gla_norm_g, rw_mu, rw_mu_vres, rw_w0, rw_w_up, rw_a0, rw_a_up, rw_g_up, rw_k_k, rw_k_a, rw_r_k, rw_gn_g, rw_gn_b, rw_v0, rw_v_up, mb_conv_w, mb_conv_b, mb_dt_bias, mb_A_log, mb_D, mb_norm_g):
    bsz, T, _ = x.shape
    n = bsz * T
    x2 = x.reshape(n, D_MODEL)
    v_first = None
    ffn_in, ffn_down = _ffn_weights(ffn_w_in, ffn_w_down)
    for l in range(DEPTH):
        x2 = _ffn_ln(x2, ffn_in, ffn_down, l, 0, ln_g[l, 0], ln_b[l, 0])
        vres = l > 0
        weights = _proj_weights(w_in[l], w_in_vres[l - 1] if vres else None)
        gla_p, rw_p, mb_p, misc = (t.reshape(bsz, T, -1) for t in _in_proj(x2, weights))
        gla_o = _gla(gla_p, misc, gla_gk_up[l], gla_gk_bias[l], gla_norm_g[l])
        rw_args = (rw_p, misc, v_first, rw_mu[l], rw_mu_vres[l - 1] if vres else None, rw_w0[l], rw_w_up[l],
                   rw_a0[l], rw_a_up[l], rw_g_up[l], rw_k_k[l], rw_k_a[l], rw_r_k[l].reshape(-1),
                   rw_gn_g[l], rw_gn_b[l])
        if vres:
            rw_o = _rwkv(*rw_args, rw_v0[l - 1], rw_v_up[l - 1])
        else:
            rw_o, v_first = _rwkv(*rw_args, None, None)
        mb_o = _mamba(mb_p, misc, mb_conv_w[l], mb_conv_b[l], mb_dt_bias[l], mb_A_log[l],
                      mb_D[l], mb_norm_g[l])
        x2 = _out_proj_ln(x2, gla_o.reshape(n, -1), rw_o.reshape(n, -1), mb_o.reshape(n, -1),
                          w_out[l], ln_g[l, 1], ln_b[l, 1])
        x2 = _ffn_ln(x2, ffn_in, ffn_down, l, 1, ln_g[l, 2], ln_b[l, 2])
    return x2.reshape(bsz, T, D_MODEL)
```
